```python
import math
import jax, jax.numpy as jnp
from jax import lax
import numpy as np

D_MODEL = 1024
BATCH = 16
SEQ = 2048
DEPTH = 1

CHUNK = 64
LEFT_CHUNKS = 8
BAND = LEFT_CHUNKS + 1
ATTN_HEADS = 8
HEAD_DIM = 64
D_ATTN = ATTN_HEADS * HEAD_DIM
MAX_REL = 256
REL_FUTURE = CHUNK - 1
N_REL = REL_FUTURE + MAX_REL + 1
D_SSM = D_MODEL // 2
SSM_GROUP = 16
SSM_GROUPS = D_SSM // SSM_GROUP
SSM_STATE = 64
STEP_MIN = 0.001
STEP_MAX = 0.1
N_BRANCH = 2
D_IN = 3 * D_ATTN + D_SSM + N_BRANCH * D_MODEL
N_GROUPS = 4
EXPERTS_PER_GROUP = 4
N_EXPERTS = N_GROUPS * EXPERTS_PER_GROUP
TOP_K = 2
D_EXPERT = D_MODEL // 4
EPS = 1e-6
NEG_INF = -1e30

kernel_name = 'hybrid_chunk_attn_s5_hmoe'


def rms_norm(x, gain):
    xf = x.astype(jnp.float32)
    y = xf * lax.rsqrt(jnp.mean(xf * xf, axis=-1, keepdims=True) + EPS)
    return (y * gain.astype(jnp.float32)).astype(x.dtype)


def chunked_relpos_attention(q, k, v, q_gain, k_gain, rel_bias):
    b, l = q.shape[0], q.shape[1]
    nc = l // CHUNK
    q = rms_norm(q, q_gain)
    k = rms_norm(k, k_gain)
    pad = ((0, 0), (LEFT_CHUNKS * CHUNK, 0), (0, 0), (0, 0))
    kp = jnp.pad(k, pad).reshape(b, nc + LEFT_CHUNKS, CHUNK, ATTN_HEADS, HEAD_DIM)
    vp = jnp.pad(v, pad).reshape(b, nc + LEFT_CHUNKS, CHUNK, ATTN_HEADS, HEAD_DIM)
    band = jnp.arange(nc)[:, None] + jnp.arange(BAND)[None, :]
    kb = kp[:, band].reshape(b, nc, BAND * CHUNK, ATTN_HEADS, HEAD_DIM)
    vb = vp[:, band].reshape(b, nc, BAND * CHUNK, ATTN_HEADS, HEAD_DIM)
    qc = q.reshape(b, nc, CHUNK, ATTN_HEADS, HEAD_DIM)
    s = jnp.einsum('bnqhd,bnkhd->bhnqk', qc, kb).astype(jnp.float32) * (HEAD_DIM ** -0.5)
    q_off = jnp.arange(CHUNK)[:, None] + LEFT_CHUNKS * CHUNK
    k_off = jnp.arange(BAND * CHUNK)[None, :]
    rel_idx = jnp.clip(q_off - k_off, -REL_FUTURE, MAX_REL) + REL_FUTURE
    bias = rel_bias.astype(jnp.float32)[:, rel_idx]
    key_pos = (jnp.arange(nc)[:, None] - LEFT_CHUNKS) * CHUNK + k_off
    valid = key_pos >= 0
    s = jnp.where(valid[None, None, :, None, :], s + bias[None, :, None], NEG_INF)
    p = jax.nn.softmax(s, axis=-1).astype(v.dtype)
    o = jnp.einsum('bhnqk,bnkhd->bnqhd', p, vb)
    return o.reshape(b, l, D_ATTN)


def _complex_linear_combine(e1, e2):
    a1r, a1i, b1r, b1i = e1
    a2r, a2i, b2r, b2i = e2
    ar = a2r * a1r - a2i * a1i
    ai = a2r * a1i + a2i * a1r
    br = a2r * b1r - a2i * b1i + b2r
    bi = a2r * b1i + a2i * b1r + b2i
    return (ar, ai, br, bi)


def s5_ssm_glu(u, lambda_re, lambda_im, log_step, b_re, b_im, c_re, c_im, d_skip, w_glu, b_glu):
    bsz, l = u.shape[0], u.shape[1]
    uf = u.astype(jnp.float32).reshape(bsz, l, SSM_GROUPS, SSM_GROUP)
    lre = lambda_re.astype(jnp.float32)
    lim = lambda_im.astype(jnp.float32)
    step = jnp.exp(log_step.astype(jnp.float32))[:, None]
    mag = jnp.exp(lre * step)
    ang = lim * step
    a_re = mag * jnp.cos(ang)
    a_im = mag * jnp.sin(ang)
    num_re = a_re - 1.0
    num_im = a_im
    den = lre * lre + lim * lim
    f_re = (num_re * lre + num_im * lim) / den
    f_im = (num_im * lre - num_re * lim) / den
    br = b_re.astype(jnp.float32)
    bi = b_im.astype(jnp.float32)
    bb_re = f_re[..., None] * br - f_im[..., None] * bi
    bb_im = f_re[..., None] * bi + f_im[..., None] * br
    bu_re = jnp.einsum('blgc,gpc->blgp', uf, bb_re)
    bu_im = jnp.einsum('blgc,gpc->blgp', uf, bb_im)
    a_re_t = jnp.broadcast_to(a_re[None, None], (1, l, SSM_GROUPS, SSM_STATE))
    a_im_t = jnp.broadcast_to(a_im[None, None], (1, l, SSM_GROUPS, SSM_STATE))
    _, _, s_re, s_im = lax.associative_scan(
        _complex_linear_combine, (a_re_t, a_im_t, bu_re, bu_im), axis=1)
    y = (jnp.einsum('blgp,gcp->blgc', s_re, c_re.astype(jnp.float32))
         - jnp.einsum('blgp,gcp->blgc', s_im, c_im.astype(jnp.float32))
         + d_skip.astype(jnp.float32).reshape(SSM_GROUPS, SSM_GROUP) * uf)
    y = y.reshape(bsz, l, D_SSM).astype(u.dtype)
    z = jax.nn.gelu(y)
    return z * jax.nn.sigmoid(z @ w_glu + b_glu)


def hierarchical_moe(h, w_group_router, group_bias, w_expert_router, expert_bias, w_e_gate, w_e_up, w_e_down):
    bsz, l, d = h.shape
    t = bsz * l
    ht = h.reshape(t, d)
    group_logits = (ht @ w_group_router).astype(jnp.float32) + group_bias.astype(jnp.float32)
    group_prob = jax.nn.softmax(group_logits, axis=-1)
    g_prob, g_idx = lax.top_k(group_prob, 1)
    expert_logits = (ht @ w_expert_router).astype(jnp.float32) + expert_bias.astype(jnp.float32)
    expert_logits = expert_logits.reshape(t, N_GROUPS, EXPERTS_PER_GROUP)
    in_group = jnp.take_along_axis(expert_logits, g_idx[:, :, None], axis=1)[:, 0]
    e_prob = jax.nn.softmax(in_group, axis=-1)
    e_val, e_loc = lax.top_k(e_prob, TOP_K)
    e_val = e_val / jnp.sum(e_val, axis=-1, keepdims=True)
    weights = g_prob * e_val
    e_glob = g_idx * EXPERTS_PER_GROUP + e_loc
    gates = jnp.sum(jax.nn.one_hot(e_glob, N_EXPERTS, dtype=jnp.float32) * weights[..., None], axis=1)
    gates = gates.astype(h.dtype)
    a = jnp.einsum('td,edf->tef', ht, w_e_gate)
    u = jnp.einsum('td,edf->tef', ht, w_e_up)
    hid = jax.nn.silu(a) * u * gates[:, :, None]
    out = hid.reshape(t, N_EXPERTS * D_EXPERT) @ w_e_down.reshape(N_EXPERTS * D_EXPERT, d)
    return out.reshape(bsz, l, d)


def setup_inputs(seed: int = 0) -> dict:
    key = jax.random.key(seed)
    ks = jax.random.split(key, 32)
    f32 = jnp.float32

    def nrm(k, shape, scale):
        return jax.random.normal(k, shape, f32) * scale

    n_idx = jnp.arange(SSM_STATE, dtype=f32)
    return {
        'x': nrm(ks[0], (BATCH, SEQ, D_MODEL), 1.0),
        'mix_norm_gain': 1.0 + nrm(ks[1], (DEPTH, D_MODEL), 0.02),
        'w_in': nrm(ks[2], (DEPTH, D_MODEL, D_IN), D_MODEL ** -0.5),
        'b_gate': nrm(ks[3], (DEPTH, N_BRANCH * D_MODEL), 0.02),
        'q_gain': 1.0 + nrm(ks[4], (DEPTH, HEAD_DIM), 0.02),
        'k_gain': 1.0 + nrm(ks[5], (DEPTH, HEAD_DIM), 0.02),
        'rel_bias': nrm(ks[6], (DEPTH, ATTN_HEADS, N_REL), 0.1),
        'ssm_lambda_re': -0.5 + nrm(ks[7], (DEPTH, SSM_GROUPS, SSM_STATE), 0.01),
        'ssm_lambda_im': math.pi * n_idx[None, None, :] + nrm(ks[8], (DEPTH, SSM_GROUPS, SSM_STATE), 0.01),
        'ssm_log_step': jax.random.uniform(ks[9], (DEPTH, SSM_GROUPS), f32, math.log(STEP_MIN), math.log(STEP_MAX)),
        'ssm_b_re': nrm(ks[10], (DEPTH, SSM_GROUPS, SSM_STATE, SSM_GROUP), (2 * SSM_GROUP) ** -0.5),
        'ssm_b_im': nrm(ks[11], (DEPTH, SSM_GROUPS, SSM_STATE, SSM_GROUP), (2 * SSM_GROUP) ** -0.5),
        'ssm_c_re': nrm(ks[12], (DEPTH, SSM_GROUPS, SSM_GROUP, SSM_STATE), (2 * SSM_STATE) ** -0.5),
        'ssm_c_im': nrm(ks[13], (DEPTH, SSM_GROUPS, SSM_GROUP, SSM_STATE), (2 * SSM_STATE) ** -0.5),
        'ssm_d': nrm(ks[14], (DEPTH, D_SSM), 1.0),
        'w_glu': nrm(ks[15], (DEPTH, D_SSM, D_SSM), D_SSM ** -0.5),
        'b_glu': nrm(ks[16], (DEPTH, D_SSM), 0.02),
        'w_branch': nrm(ks[17], (DEPTH, D_ATTN + D_SSM, D_MODEL), D_ATTN ** -0.5),
        'w_out': nrm(ks[18], (DEPTH, D_MODEL, D_MODEL), D_MODEL ** -0.5),
        'ffn_norm_gain': 1.0 + nrm(ks[19], (DEPTH, D_MODEL), 0.02),
        'w_group_router': nrm(ks[20], (DEPTH, D_MODEL, N_GROUPS), D_MODEL ** -0.5),
        'group_bias': nrm(ks[21], (DEPTH, N_GROUPS), 0.01),
        'w_expert_router': nrm(ks[22], (DEPTH, D_MODEL, N_EXPERTS), D_MODEL ** -0.5),
        'expert_bias': nrm(ks[23], (DEPTH, N_EXPERTS), 0.01),
        'w_e_gate': nrm(ks[24], (DEPTH, N_EXPERTS, D_MODEL, D_EXPERT), D_MODEL ** -0.5),
        'w_e_up': nrm(ks[25], (DEPTH, N_EXPERTS, D_MODEL, D_EXPERT), D_MODEL ** -0.5),
        'w_e_down': nrm(ks[26], (DEPTH, N_EXPERTS, D_EXPERT, D_MODEL), D_EXPERT ** -0.5),
    }


def reference(x, mix_norm_gain, w_in, b_gate, q_gain, k_gain, rel_bias,
              ssm_lambda_re, ssm_lambda_im, ssm_log_step, ssm_b_re, ssm_b_im,
              ssm_c_re, ssm_c_im, ssm_d, w_glu, b_glu, w_branch, w_out,
              ffn_norm_gain, w_group_router, group_bias, w_expert_router, expert_bias,
              w_e_gate, w_e_up, w_e_down):
    bsz, l, _ = x.shape
    for i in range(DEPTH):
        h = rms_norm(x, mix_norm_gain[i])
        proj = h @ w_in[i]
        q = proj[..., :D_ATTN].reshape(bsz, l, ATTN_HEADS, HEAD_DIM)
        k = proj[..., D_ATTN:2 * D_ATTN].reshape(bsz, l, ATTN_HEADS, HEAD_DIM)
        v = proj[..., 2 * D_ATTN:3 * D_ATTN].reshape(bsz, l, ATTN_HEADS, HEAD_DIM)
        u = proj[..., 3 * D_ATTN:3 * D_ATTN + D_SSM]
        gate = jax.nn.sigmoid(proj[..., 3 * D_ATTN + D_SSM:] + b_gate[i])
        gate = gate.reshape(bsz, l, N_BRANCH, D_MODEL)
        y_attn = chunked_relpos_attention(q, k, v, q_gain[i], k_gain[i], rel_bias[i])
        y_ssm = s5_ssm_glu(u, ssm_lambda_re[i], ssm_lambda_im[i], ssm_log_step[i],
                           ssm_b_re[i], ssm_b_im[i], ssm_c_re[i], ssm_c_im[i],
                           ssm_d[i], w_glu[i], b_glu[i])
        wb = w_branch[i]
        merged = (gate[:, :, 0] * (y_attn @ wb[:D_ATTN])
                  + gate[:, :, 1] * (y_ssm @ wb[D_ATTN:]))
        x = x + merged @ w_out[i]
        h = rms_norm(x, ffn_norm_gain[i])
        x = x + hierarchical_moe(h, w_group_router[i], group_bias[i], w_expert_router[i],
                                 expert_bias[i], w_e_gate[i], w_e_up[i], w_e_down[i])
    return x
```

```python
import functools
import math

import jax
import jax.numpy as jnp
from jax import lax
from jax.experimental import pallas as pl
from jax.experimental.pallas import tpu as pltpu

F32 = jnp.float32
BF16 = jnp.bfloat16

D_MODEL = 1024
CHUNK = 64
LEFT_CHUNKS = 8
BAND = LEFT_CHUNKS + 1
BAND_KEYS = BAND * CHUNK
PAD_KEYS = LEFT_CHUNKS * CHUNK
ATTN_HEADS = 8
HEAD_DIM = 64
D_ATTN = ATTN_HEADS * HEAD_DIM
MAX_REL = 256
REL_FUTURE = CHUNK - 1
D_SSM = D_MODEL // 2
SSM_GROUP = 16
SSM_GROUPS = D_SSM // SSM_GROUP
SSM_STATE = 64
N_BRANCH = 2
D_IN = 3 * D_ATTN + D_SSM + N_BRANCH * D_MODEL
N_GROUPS = 4
EXPERTS_PER_GROUP = 4
N_EXPERTS = N_GROUPS * EXPERTS_PER_GROUP
D_EXPERT = D_MODEL // 4
EPS = 1e-6
NEG_INF = -1e30

LANES = 128
HEAD_PAIRS = D_ATTN // LANES
SSM_CHUNK = 16
SSM_ROW = SSM_CHUNK * SSM_GROUP
ROUTER_LANES = LANES
VMEM_LIMIT = 56 * 1024 * 1024


def _tiles(n_tokens):
    return dict(inproj=512, merge=512, moe=512)


def _compiler_params(n_axes):
    return pltpu.CompilerParams(
        dimension_semantics=("arbitrary",) * n_axes,
        vmem_limit_bytes=VMEM_LIMIT)


def _full(shape):
    nd = len(shape)
    return pl.BlockSpec(shape, lambda *_: (0,) * nd)


def _inproj_kernel(x_ref, gain_ref, w_ref, bgate_ref, qg_ref, kg_ref, ones_ref,
                   q_ref, k_ref, v_ref, u_ref, gate_ref):
    x = x_ref[...]
    ms = jnp.mean(x * x, axis=-1, keepdims=True)
    h = (x * lax.rsqrt(ms + EPS) * gain_ref[...]).astype(BF16)

    def proj(lo, n):
        return jnp.dot(h, w_ref[:, lo:lo + n], preferred_element_type=F32)

    def head_norm(t, g_ref):
        ss = jnp.dot((t * t).astype(BF16), ones_ref[...], preferred_element_type=F32)
        return t * lax.rsqrt(ss * (1.0 / HEAD_DIM) + EPS) * g_ref[...]

    q_ref[...] = head_norm(proj(0, D_ATTN), qg_ref).astype(BF16)
    k_ref[...] = head_norm(proj(D_ATTN, D_ATTN), kg_ref).astype(BF16)
    v_ref[...] = proj(2 * D_ATTN, D_ATTN).astype(BF16)
    u_ref[...] = proj(3 * D_ATTN, D_SSM).astype(BF16)
    g0 = 3 * D_ATTN + D_SSM
    for s in range(N_BRANCH):
        cols = slice(s * D_MODEL, (s + 1) * D_MODEL)
        logits = proj(g0 + s * D_MODEL, D_MODEL) + bgate_ref[:, cols]
        gate_ref[:, cols] = jax.nn.sigmoid(logits).astype(BF16)


def _inproj(x2, gain, w_in, b_gate, q_gain, k_gain, tm):
    t = x2.shape[0]
    ones = jnp.kron(jnp.eye(ATTN_HEADS, dtype=F32), jnp.ones((HEAD_DIM, HEAD_DIM), F32)).astype(BF16)
    qg = jnp.tile(q_gain, ATTN_HEADS)[None, :] * (HEAD_DIM ** -0.5)
    kg = jnp.tile(k_gain, ATTN_HEADS)[None, :]
    row = lambda n: pl.BlockSpec((tm, n), lambda i: (i, 0))
    return pl.pallas_call(
        _inproj_kernel,
        grid=(t // tm,),
        in_specs=[row(D_MODEL), _full((1, D_MODEL)), _full((D_MODEL, D_IN)),
                  _full((1, N_BRANCH * D_MODEL)), _full((1, D_ATTN)), _full((1, D_ATTN)),
                  _full((D_ATTN, D_ATTN))],
        out_specs=[row(D_ATTN), row(D_ATTN), row(D_ATTN), row(D_SSM), row(N_BRANCH * D_MODEL)],
        out_shape=[jax.ShapeDtypeStruct((t, D_ATTN), BF16)] * 3
        + [jax.ShapeDtypeStruct((t, D_SSM), BF16),
           jax.ShapeDtypeStruct((t, N_BRANCH * D_MODEL), BF16)],
        compiler_params=_compiler_params(1),
        name="inproj",
    )(x2, gain[None, :], w_in.astype(BF16), b_gate[None, :], qg, kg, ones)


def _attn_kernel(q_ref, k_ref, v_ref, bias_ref, o_ref, kpad, vpad):
    seq = q_ref.shape[0]
    zeros = jnp.zeros((PAD_KEYS, D_ATTN), BF16)
    kpad[0:PAD_KEYS, :] = zeros
    vpad[0:PAD_KEYS, :] = zeros
    kpad[PAD_KEYS:, :] = k_ref[...]
    vpad[PAD_KEYS:, :] = v_ref[...]

    lane = lax.broadcasted_iota(jnp.int32, (CHUNK, LANES), 1)
    first_head = lane < HEAD_DIM
    key_col = lax.broadcasted_iota(jnp.int32, (CHUNK, BAND_KEYS), 1)

    def chunk_body(n, carry):
        r0 = pl.multiple_of(n * CHUNK, CHUNK)
        valid = key_col >= (LEFT_CHUNKS - n) * CHUNK
        for j in range(HEAD_PAIRS):
            cols = slice(j * LANES, (j + 1) * LANES)
            qp = q_ref[pl.ds(r0, CHUNK), cols]
            kb = kpad[pl.ds(r0, BAND_KEYS), cols]
            vb = vpad[pl.ds(r0, BAND_KEYS), cols]
            outs = []
            for hh in range(2):
                mask = first_head if hh == 0 else jnp.logical_not(first_head)
                qh = jnp.where(mask, qp, jnp.zeros_like(qp))
                s = lax.dot_general(qh, kb, (((1,), (1,)), ((), ())),
                                    preferred_element_type=F32)
                s = jnp.where(valid, s + bias_ref[2 * j + hh], NEG_INF)
                m = jnp.max(s, axis=-1, keepdims=True)
                p = jnp.exp(s - m)
                l = jnp.sum(p, axis=-1, keepdims=True)
                o = jnp.dot(p.astype(BF16), vb, preferred_element_type=F32)
                outs.append(o / l)
            o_ref[pl.ds(r0, CHUNK), cols] = jnp.where(first_head, outs[0], outs[1]).astype(BF16)
        return carry

    lax.fori_loop(0, seq // CHUNK, chunk_body, 0)


def _attention(q, k, v, bias, bsz, seq):
    q3, k3, v3 = (a.reshape(bsz, seq, D_ATTN) for a in (q, k, v))
    seq_spec = pl.BlockSpec((None, seq, D_ATTN), lambda b: (b, 0, 0))
    out = pl.pallas_call(
        _attn_kernel,
        grid=(bsz,),
        in_specs=[seq_spec, seq_spec, seq_spec, _full((ATTN_HEADS, CHUNK, BAND_KEYS))],
        out_specs=seq_spec,
        out_shape=jax.ShapeDtypeStruct((bsz, seq, D_ATTN), BF16),
        scratch_shapes=[pltpu.VMEM((PAD_KEYS + seq, D_ATTN), BF16),
                        pltpu.VMEM((PAD_KEYS + seq, D_ATTN), BF16)],
        compiler_params=_compiler_params(1),
        name="attention",
    )(q3, k3, v3, bias)
    return out.reshape(bsz * seq, D_ATTN)


def _rel_bias_table(rel_bias):
    q_off = jnp.arange(CHUNK)[:, None] + PAD_KEYS
    k_off = jnp.arange(BAND_KEYS)[None, :]
    rel_idx = jnp.clip(q_off - k_off, -REL_FUTURE, MAX_REL) + REL_FUTURE
    return rel_bias[:, rel_idx]


def _ssm_tables(lambda_re, lambda_im, log_step, b_re, b_im, c_re, c_im, d_skip):
    hp = lax.Precision.HIGHEST
    step = jnp.exp(log_step)[:, None]
    mag = jnp.exp(lambda_re * step)
    ang = lambda_im * step
    a_re = mag * jnp.cos(ang)
    a_im = mag * jnp.sin(ang)
    num_re = a_re - 1.0
    num_im = a_im
    den = lambda_re * lambda_re + lambda_im * lambda_im
    f_re = (num_re * lambda_re + num_im * lambda_im) / den
    f_im = (num_im * lambda_re - num_re * lambda_im) / den
    bb_re = f_re[..., None] * b_re - f_im[..., None] * b_im
    bb_im = f_re[..., None] * b_im + f_im[..., None] * b_re
    tau = jnp.arange(SSM_CHUNK + 1, dtype=F32)[:, None, None]
    pmag = jnp.exp(tau * (lambda_re * step)[None])
    pw_re = pmag * jnp.cos(tau * ang[None])
    pw_im = pmag * jnp.sin(tau * ang[None])
    cp_re = c_re[None] * pw_re[:, :, None, :] - c_im[None] * pw_im[:, :, None, :]
    cp_im = c_re[None] * pw_im[:, :, None, :] + c_im[None] * pw_re[:, :, None, :]
    kern = (jnp.einsum('tgop,gpi->tgoi', cp_re[:SSM_CHUNK], bb_re, precision=hp)
            - jnp.einsum('tgop,gpi->tgoi', cp_im[:SSM_CHUNK], bb_im, precision=hp))
    tj = jnp.arange(SSM_CHUNK)
    lag = tj[None, :] - tj[:, None]
    kj = kern[jnp.clip(lag, 0, SSM_CHUNK - 1)]
    kj = jnp.where((lag >= 0)[:, :, None, None, None], kj, 0.0)
    m1 = jnp.transpose(kj, (2, 0, 4, 1, 3))
    d_g = d_skip.reshape(SSM_GROUPS, SSM_GROUP)
    eye_t = jnp.eye(SSM_CHUNK, dtype=F32)
    eye_c = jnp.eye(SSM_GROUP, dtype=F32)
    m1 = m1 + (eye_t[None, :, None, :, None] * eye_c[None, None, :, None, :]
               * d_g[:, None, None, None, :])
    m1 = m1.reshape(SSM_GROUPS, SSM_ROW, SSM_ROW)
    rev_re = pw_re[SSM_CHUNK - 1::-1][:SSM_CHUNK]
    rev_im = pw_im[SSM_CHUNK - 1::-1][:SSM_CHUNK]
    bt_re = jnp.transpose(bb_re, (0, 2, 1))
    bt_im = jnp.transpose(bb_im, (0, 2, 1))
    rr = jnp.transpose(rev_re, (1, 0, 2))[:, :, None, :]
    ri = jnp.transpose(rev_im, (1, 0, 2))[:, :, None, :]
    m2_re = rr * bt_re[:, None] - ri * bt_im[:, None]
    m2_im = rr * bt_im[:, None] + ri * bt_re[:, None]
    m2 = jnp.concatenate([m2_re, m2_im], axis=-1).reshape(SSM_GROUPS, SSM_ROW, 2 * SSM_STATE)
    m3_re = jnp.transpose(cp_re[1:], (1, 3, 0, 2)).reshape(SSM_GROUPS, SSM_STATE, SSM_ROW)
    m3_im = -jnp.transpose(cp_im[1:], (1, 3, 0, 2)).reshape(SSM_GROUPS, SSM_STATE, SSM_ROW)
    a_chunk = jnp.stack([pw_re[SSM_CHUNK], pw_im[SSM_CHUNK]], axis=1)
    return m1.astype(BF16), m2.astype(BF16), m3_re.astype(BF16), m3_im.astype(BF16), a_chunk


def _ssm_kernel(u_ref, m1_ref, m2_ref, m3r_ref, m3i_ref, a_ref, y_ref,
                w_re, w_im, s_re, s_im, *, bsz, n_chunks):
    u = u_ref[...]
    w = jnp.dot(u, m2_ref[...], preferred_element_type=F32)
    w_re[...] = w[:, :SSM_STATE]
    w_im[...] = w[:, SSM_STATE:]
    ar = a_ref[0:1, :]
    ai = a_ref[1:2, :]

    def step(c, carry):
        cr, ci = carry
        r = pl.multiple_of(c * bsz, bsz)
        s_re[pl.ds(r, bsz), :] = cr
        s_im[pl.ds(r, bsz), :] = ci
        nr = ar * cr - ai * ci + w_re[pl.ds(r, bsz), :]
        ni = ar * ci + ai * cr + w_im[pl.ds(r, bsz), :]
        return nr, ni

    zero = jnp.zeros((bsz, SSM_STATE), F32)
    lax.fori_loop(0, n_chunks, step, (zero, zero))
    y = jnp.dot(u, m1_ref[...], preferred_element_type=F32)
    y = y + jnp.dot(s_re[...].astype(BF16), m3r_ref[...], preferred_element_type=F32)
    y = y + jnp.dot(s_im[...].astype(BF16), m3i_ref[...], preferred_element_type=F32)
    y_ref[...] = y


def _ssm(u, tables, bsz, seq):
    m1, m2, m3r, m3i, a_chunk = tables
    n_chunks = seq // SSM_CHUNK
    rows = n_chunks * bsz
    ug = u.reshape(bsz, n_chunks, SSM_CHUNK, SSM_GROUPS, SSM_GROUP)
    ug = jnp.transpose(ug, (3, 1, 0, 2, 4)).reshape(SSM_GROUPS, rows, SSM_ROW)
    per_group = lambda a, b: pl.BlockSpec((None, a, b), lambda g: (g, 0, 0))
    yg = pl.pallas_call(
        functools.partial(_ssm_kernel, bsz=bsz, n_chunks=n_chunks),
        grid=(SSM_GROUPS,),
        in_specs=[per_group(rows, SSM_ROW), per_group(SSM_ROW, SSM_ROW),
                  per_group(SSM_ROW, 2 * SSM_STATE), per_group(SSM_STATE, SSM_ROW),
                  per_group(SSM_STATE, SSM_ROW), per_group(2, SSM_STATE)],
        out_specs=per_group(rows, SSM_ROW),
        out_shape=jax.ShapeDtypeStruct((SSM_GROUPS, rows, SSM_ROW), F32),
        scratch_shapes=[pltpu.VMEM((rows, SSM_STATE), F32)] * 4,
        compiler_params=_compiler_params(1),
        name="ssm",
    )(ug, m1, m2, m3r, m3i, a_chunk)
    y = yg.reshape(SSM_GROUPS, n_chunks, bsz, SSM_CHUNK, SSM_GROUP)
    return jnp.transpose(y, (2, 1, 3, 0, 4)).reshape(bsz * seq, D_SSM)


def _split_bf16(a):
    hi = a.astype(BF16)
    lo = (a - hi.astype(F32)).astype(BF16)
    return hi, lo


def _merge_kernel(x_ref, ya_ref, ys_ref, gate_ref, wglu_ref, bglu_ref, wba_ref, wbs_ref,
                  wout_ref, gain_ref, rhi_ref, rlo_ref, rbias_ref,
                  x1_ref, h_ref, gates_ref):
    z = jax.nn.gelu(ys_ref[...])
    glu = jnp.dot(z.astype(BF16), wglu_ref[...], preferred_element_type=F32) + bglu_ref[...]
    y_ssm = z * jax.nn.sigmoid(glu)
    br_a = jnp.dot(ya_ref[...], wba_ref[...], preferred_element_type=F32)
    br_s = jnp.dot(y_ssm.astype(BF16), wbs_ref[...], preferred_element_type=F32)
    merged = (gate_ref[:, :D_MODEL].astype(F32) * br_a
              + gate_ref[:, D_MODEL:].astype(F32) * br_s)
    x1 = x_ref[...] + jnp.dot(merged.astype(BF16), wout_ref[...], preferred_element_type=F32)
    x1_ref[...] = x1

    ms = jnp.mean(x1 * x1, axis=-1, keepdims=True)
    h = x1 * lax.rsqrt(ms + EPS) * gain_ref[...]
    h_ref[...] = h.astype(BF16)

    h_hi, h_lo = _split_bf16(h)
    logits = (jnp.dot(h_hi, rhi_ref[...], preferred_element_type=F32)
              + jnp.dot(h_hi, rlo_ref[...], preferred_element_type=F32)
              + jnp.dot(h_lo, rhi_ref[...], preferred_element_type=F32)
              + rbias_ref[...])
    lane = lax.broadcasted_iota(jnp.int32, logits.shape, 1)
    lane_f = lane.astype(F32)
    big = float(ROUTER_LANES)

    def first_argmax(vals, vmax):
        return jnp.min(jnp.where(vals == vmax, lane_f, big), axis=-1, keepdims=True)

    g_mask = (lane >= N_EXPERTS) & (lane < N_EXPERTS + N_GROUPS)
    g_log = jnp.where(g_mask, logits, NEG_INF)
    g_max = jnp.max(g_log, axis=-1, keepdims=True)
    g_sum = jnp.sum(jnp.where(g_mask, jnp.exp(g_log - g_max), 0.0), axis=-1, keepdims=True)
    g_prob = 1.0 / g_sum
    g_idx = first_argmax(g_log, g_max) - float(N_EXPERTS)
    e_lo = g_idx * float(EXPERTS_PER_GROUP)
    e_mask = (lane_f >= e_lo) & (lane_f < e_lo + float(EXPERTS_PER_GROUP))
    e_log = jnp.where(e_mask, logits, NEG_INF)
    e_max1 = jnp.max(e_log, axis=-1, keepdims=True)
    idx1 = first_argmax(e_log, e_max1)
    e_log2 = jnp.where(lane_f == idx1, NEG_INF, e_log)
    e_max2 = jnp.max(e_log2, axis=-1, keepdims=True)
    idx2 = first_argmax(e_log2, e_max2)
    r = jnp.exp(e_max2 - e_max1)
    w1 = g_prob / (1.0 + r)
    w2 = g_prob * r / (1.0 + r)
    gates_ref[...] = (jnp.where(lane_f == idx1, w1, 0.0) + jnp.where(lane_f == idx2, w2, 0.0))


def _merge(x2, y_attn, y_pre, gate, w_glu, b_glu, w_branch, w_out, ffn_gain,
           w_group_router, group_bias, w_expert_router, expert_bias, tm):
    t = x2.shape[0]
    pad = ROUTER_LANES - N_EXPERTS - N_GROUPS
    w_r = jnp.concatenate([w_expert_router, w_group_router,
                           jnp.zeros((D_MODEL, pad), F32)], axis=1)
    r_hi, r_lo = _split_bf16(w_r)
    r_bias = jnp.concatenate([expert_bias, group_bias, jnp.zeros((pad,), F32)])[None, :]
    row = lambda n: pl.BlockSpec((tm, n), lambda i: (i, 0))
    return pl.pallas_call(
        _merge_kernel,
        grid=(t // tm,),
        in_specs=[row(D_MODEL), row(D_ATTN), row(D_SSM), row(N_BRANCH * D_MODEL),
                  _full((D_SSM, D_SSM)), _full((1, D_SSM)),
                  _full((D_ATTN, D_MODEL)), _full((D_SSM, D_MODEL)),
                  _full((D_MODEL, D_MODEL)), _full((1, D_MODEL)),
                  _full((D_MODEL, ROUTER_LANES)), _full((D_MODEL, ROUTER_LANES)),
                  _full((1, ROUTER_LANES))],
        out_specs=[row(D_MODEL), row(D_MODEL), row(ROUTER_LANES)],
        out_shape=[jax.ShapeDtypeStruct((t, D_MODEL), F32),
                   jax.ShapeDtypeStruct((t, D_MODEL), BF16),
                   jax.ShapeDtypeStruct((t, ROUTER_LANES), F32)],
        compiler_params=_compiler_params(1),
        name="merge",
    )(x2, y_attn, y_pre, gate, w_glu.astype(BF16), b_glu[None, :],
      w_branch[:D_ATTN].astype(BF16), w_branch[D_ATTN:].astype(BF16),
      w_out.astype(BF16), ffn_gain[None, :], r_hi, r_lo, r_bias)


def _moe_kernel(x1_ref, h_ref, gates_ref, wg_ref, wu_ref, wd_ref, o_ref):
    h = h_ref[...]
    gates = gates_ref[...]
    acc = x1_ref[...]
    width = EXPERTS_PER_GROUP * D_EXPERT
    for g in range(N_GROUPS):
        cols = slice(g * width, (g + 1) * width)
        a = jnp.dot(h, wg_ref[:, cols], preferred_element_type=F32)
        u = jnp.dot(h, wu_ref[:, cols], preferred_element_type=F32)
        hid = jax.nn.silu(a) * u
        parts = []
        for j in range(EXPERTS_PER_GROUP):
            e = g * EXPERTS_PER_GROUP + j
            parts.append(hid[:, j * D_EXPERT:(j + 1) * D_EXPERT] * gates[:, e:e + 1])
        hid = jnp.concatenate(parts, axis=1).astype(BF16)
        acc = acc + jnp.dot(hid, wd_ref[cols, :], preferred_element_type=F32)
    o_ref[...] = acc


def _moe(x1, h, gates, w_e_gate, w_e_up, w_e_down, tm):
    t = x1.shape[0]
    hidden = N_EXPERTS * D_EXPERT
    wg = jnp.transpose(w_e_gate, (1, 0, 2)).reshape(D_MODEL, hidden).astype(BF16)
    wu = jnp.transpose(w_e_up, (1, 0, 2)).reshape(D_MODEL, hidden).astype(BF16)
    wd = w_e_down.reshape(hidden, D_MODEL).astype(BF16)
    row = lambda n: pl.BlockSpec((tm, n), lambda i: (i, 0))
    resident = lambda shape: pl.BlockSpec(shape, lambda i: (0, 0), pipeline_mode=pl.Buffered(1))
    return pl.pallas_call(
        _moe_kernel,
        grid=(t // tm,),
        in_specs=[row(D_MODEL), row(D_MODEL), row(ROUTER_LANES),
                  resident((D_MODEL, hidden)), resident((D_MODEL, hidden)),
                  resident((hidden, D_MODEL))],
        out_specs=row(D_MODEL),
        out_shape=jax.ShapeDtypeStruct((t, D_MODEL), F32),
        compiler_params=_compiler_params(1),
        name="moe",
    )(x1, h, gates, wg, wu, wd)


def kernel(x, mix_norm_gain, w_in, b_gate, q_gain, k_gain, rel_bias, ssm_lambda_re, ssm_lambda_im, ssm_log_step, ssm_b_re, ssm_b_im, ssm_c_re, ssm_c_im, ssm_d, w_glu, b_glu, w_branch, w_out, ffn_norm_gain, w_group_router, group_bias, w_expert_router, expert_bias, w_e_gate, w_e_up, w_e_down):
    bsz, seq, _ = x.shape
    depth = w_in.shape[0]
    tiles = _tiles(bsz * seq)
    x2 = x.reshape(bsz * seq, D_MODEL)
    for i in range(depth):
        q, k, v, u, gate = _inproj(x2, mix_norm_gain[i], w_in[i], b_gate[i], q_gain[i],
                                   k_gain[i], tiles["inproj"])
        y_attn = _attention(q, k, v, _rel_bias_table(rel_bias[i]), bsz, seq)
        tables = _ssm_tables(ssm_lambda_re[i], ssm_lambda_im[i], ssm_log_step[i],
                             ssm_b_re[i], ssm_b_im[i], ssm_c_re[i], ssm_c_im[i], ssm_d[i])
        y_pre = _ssm(u, tables, bsz, seq)
        x1, h, gates = _merge(x2, y_attn, y_pre, gate, w_glu[i], b_glu[i], w_branch[i],
                              w_out[i], ffn_norm_gain[i], w_group_router[i], group_bias[i],
                              w_expert_router[i], expert_bias[i], tiles["merge"])
        x2 = _moe(x1, h, gates, w_e_gate[i], w_e_up[i], w_e_down[i], tiles["moe"])
    return x2.reshape(bsz, seq, D_MODEL)
```

```python
import functools

import jax
import jax.numpy as jnp
from jax import lax
from jax.experimental import pallas as pl
from jax.experimental.pallas import tpu as pltpu

F32 = jnp.float32
BF16 = jnp.bfloat16

D_MODEL = 1024
CHUNK = 64
LEFT_CHUNKS = 8
BAND = LEFT_CHUNKS + 1
BAND_KEYS = BAND * CHUNK
PAD_KEYS = LEFT_CHUNKS * CHUNK
ATTN_HEADS = 8
HEAD_DIM = 64
D_ATTN = ATTN_HEADS * HEAD_DIM
MAX_REL = 256
REL_FUTURE = CHUNK - 1
D_SSM = D_MODEL // 2
SSM_GROUP = 16
SSM_GROUPS = D_SSM // SSM_GROUP
SSM_STATE = 64
N_BRANCH = 2
D_QKV = 3 * D_ATTN
D_IN = D_QKV + D_SSM + N_BRANCH * D_MODEL
N_GROUPS = 4
EXPERTS_PER_GROUP = 4
N_EXPERTS = N_GROUPS * EXPERTS_PER_GROUP
D_EXPERT = D_MODEL // 4
EPS = 1e-6
NEG_INF = -1e30
LOG2_E = 1.4426950408889634

LANES = 128
HEAD_PAIRS = D_ATTN // LANES
SSM_CHUNK = 16
SSM_ROW = SSM_CHUNK * SSM_GROUP
SSM_KBLOCK = 16
SSM_PHASES = 4
ROUTER_LANES = LANES
VMEM_LIMIT = 56 * 1024 * 1024


def _tiles(n_tokens):
    return dict(inproj=512, moe=512)


def _compiler_params(n_axes):
    return pltpu.CompilerParams(
        dimension_semantics=("arbitrary",) * n_axes,
        vmem_limit_bytes=VMEM_LIMIT)


def _full(shape):
    nd = len(shape)
    return pl.BlockSpec(shape, lambda *_: (0,) * nd)


def _rms_norm(x, gain):
    ms = jnp.mean(x * x, axis=-1, keepdims=True)
    return x * lax.rsqrt(ms + EPS) * gain


NT_DIMS = (((1,), (1,)), ((), ()))


def _inproj_kernel(x_ref, gain_ref, w_ref, qg_ref, kg_ref, ones_ref, q_ref, k_ref, v_ref):
    h = _rms_norm(x_ref[...], gain_ref[...]).astype(BF16)

    def proj(lo, n):
        return jnp.dot(h, w_ref[:, lo:lo + n], preferred_element_type=F32)

    def head_norm(t, g_ref):
        ss = jnp.dot((t * t).astype(BF16), ones_ref[...], preferred_element_type=F32)
        return t * lax.rsqrt(ss * (1.0 / HEAD_DIM) + EPS) * g_ref[...]

    q_ref[...] = head_norm(proj(0, D_ATTN), qg_ref).astype(BF16)
    k_ref[...] = head_norm(proj(D_ATTN, D_ATTN), kg_ref).astype(BF16)
    v_ref[...] = proj(2 * D_ATTN, D_ATTN).astype(BF16)


def _inproj(x2, gain, w_qkv, q_gain, k_gain, tm):
    t = x2.shape[0]
    ones = jnp.kron(jnp.eye(ATTN_HEADS, dtype=F32), jnp.ones((HEAD_DIM, HEAD_DIM), F32)).astype(BF16)
    qg = jnp.tile(q_gain, ATTN_HEADS)[None, :] * (HEAD_DIM ** -0.5 * LOG2_E)
    kg = jnp.tile(k_gain, ATTN_HEADS)[None, :]
    row = lambda n: pl.BlockSpec((tm, n), lambda i: (i, 0))
    return pl.pallas_call(
        _inproj_kernel,
        grid=(t // tm,),
        in_specs=[row(D_MODEL), _full((1, D_MODEL)), _full((D_MODEL, D_QKV)),
                  _full((1, D_ATTN)), _full((1, D_ATTN)), _full((D_ATTN, D_ATTN))],
        out_specs=[row(D_ATTN)] * 3,
        out_shape=[jax.ShapeDtypeStruct((t, D_ATTN), BF16)] * 3,
        compiler_params=_compiler_params(1),
        name="inproj",
    )(x2, gain[None, :], w_qkv.astype(BF16), qg, kg, ones)


def _attn_kernel(q_ref, k_ref, v_ref, bias_ref, o_ref, kpad, v_even, v_odd, s_a, s_b):
    seq = q_ref.shape[0]
    n_chunks = seq // CHUNK
    lane_v = lax.broadcasted_iota(jnp.int32, (seq, D_ATTN), 1)
    even_v = (lane_v % LANES) < HEAD_DIM
    zeros = jnp.zeros((PAD_KEYS, D_ATTN), BF16)
    v = v_ref[...]
    kpad[0:PAD_KEYS, :] = zeros
    v_even[0:PAD_KEYS, :] = zeros
    v_odd[0:PAD_KEYS, :] = zeros
    kpad[PAD_KEYS:, :] = k_ref[...]
    v_even[PAD_KEYS:, :] = jnp.where(even_v, v, jnp.zeros_like(v))
    v_odd[PAD_KEYS:, :] = jnp.where(even_v, jnp.zeros_like(v), v)

    lane = lax.broadcasted_iota(jnp.int32, (CHUNK, LANES), 1)
    first_head = lane < HEAD_DIM
    key_col = lax.broadcasted_iota(jnp.int32, (2 * CHUNK, BAND_KEYS), 1)

    def scores(n, s_buf):
        r0 = pl.multiple_of(n * CHUNK, CHUNK)
        for j in range(HEAD_PAIRS):
            cols = slice(j * LANES, (j + 1) * LANES)
            qp = q_ref[pl.ds(r0, CHUNK), cols]
            zq = jnp.zeros_like(qp)
            q2 = jnp.concatenate([jnp.where(first_head, qp, zq), jnp.where(first_head, zq, qp)], axis=0)
            kb = kpad[pl.ds(r0, BAND_KEYS), cols]
            s_buf[j * 2 * CHUNK:(j + 1) * 2 * CHUNK, :] = lax.dot_general(
                q2, kb, NT_DIMS, preferred_element_type=F32)

    def softmax_pv(n, s_buf, masked):
        r0 = pl.multiple_of(n * CHUNK, CHUNK)
        for j in range(HEAD_PAIRS):
            cols = slice(j * LANES, (j + 1) * LANES)
            rows = slice(j * 2 * CHUNK, (j + 1) * 2 * CHUNK)
            s = s_buf[rows, :] + bias_ref[rows, :]
            if masked:
                s = jnp.where(key_col >= (LEFT_CHUNKS - n) * CHUNK, s, NEG_INF)
            m = jnp.max(s, axis=-1, keepdims=True)
            p = jnp.exp2(s - m)
            inv_l = 1.0 / jnp.sum(p, axis=-1, keepdims=True)
            pb = p.astype(BF16)
            o0 = jnp.dot(pb[:CHUNK], v_even[pl.ds(r0, BAND_KEYS), cols], preferred_element_type=F32)
            o1 = jnp.dot(pb[CHUNK:], v_odd[pl.ds(r0, BAND_KEYS), cols], preferred_element_type=F32)
            o_ref[pl.ds(r0, CHUNK), cols] = (o0 * inv_l[:CHUNK] + o1 * inv_l[CHUNK:]).astype(BF16)

    def pair_body(i, carry, *, masked):
        n0 = 2 * i
        scores(n0 + 1, s_b)
        softmax_pv(n0, s_a, masked)
        scores(jnp.minimum(n0 + 2, n_chunks - 1), s_a)
        softmax_pv(n0 + 1, s_b, masked)
        return carry

    assert n_chunks % 2 == 0 and LEFT_CHUNKS % 2 == 0
    masked_pairs = min(LEFT_CHUNKS, n_chunks) // 2
    scores(0, s_a)
    lax.fori_loop(0, masked_pairs, functools.partial(pair_body, masked=True), 0)
    lax.fori_loop(masked_pairs, n_chunks // 2, functools.partial(pair_body, masked=False), 0)


def _attention(q, k, v, bias, bsz, seq):
    q3, k3, v3 = (a.reshape(bsz, seq, D_ATTN) for a in (q, k, v))
    seq_spec = pl.BlockSpec((None, seq, D_ATTN), lambda b: (b, 0, 0))
    padded = pltpu.VMEM((PAD_KEYS + seq, D_ATTN), BF16)
    out = pl.pallas_call(
        _attn_kernel,
        grid=(bsz,),
        in_specs=[seq_spec, seq_spec, seq_spec, _full((ATTN_HEADS * CHUNK, BAND_KEYS))],
        out_specs=seq_spec,
        out_shape=jax.ShapeDtypeStruct((bsz, seq, D_ATTN), BF16),
        scratch_shapes=[padded, padded, padded,
                        pltpu.VMEM((ATTN_HEADS * CHUNK, BAND_KEYS), F32),
                        pltpu.VMEM((ATTN_HEADS * CHUNK, BAND_KEYS), F32)],
        compiler_params=_compiler_params(1),
        name="attention",
    )(q3, k3, v3, bias)
    return out.reshape(bsz * seq, D_ATTN)


def _rel_bias_table(rel_bias):
    q_off = jnp.arange(CHUNK)[:, None] + PAD_KEYS
    k_off = jnp.arange(BAND_KEYS)[None, :]
    rel_idx = jnp.clip(q_off - k_off, -REL_FUTURE, MAX_REL) + REL_FUTURE
    return (rel_bias[:, rel_idx] * LOG2_E).reshape(ATTN_HEADS * CHUNK, BAND_KEYS)


def _phase_shape(bsz, seq, width):
    return (bsz, seq // (SSM_KBLOCK * SSM_CHUNK), SSM_KBLOCK, SSM_CHUNK * width)


def _phase_view(a, bsz, seq, width):
    return a.reshape(_phase_shape(bsz, seq, width))


def _phase_spec(bsz, width):
    return pl.BlockSpec((bsz, None, SSM_KBLOCK, SSM_PHASES * width),
                        lambda kb, pg: (0, kb, 0, pg))


def _phase_grid(seq):
    return (seq // (SSM_KBLOCK * SSM_CHUNK), SSM_CHUNK // SSM_PHASES)


def _transposed_spec(bsz):
    return pl.BlockSpec((SSM_GROUPS, SSM_PHASES * SSM_GROUP, bsz * SSM_KBLOCK),
                        lambda kb, pg: (0, pg, kb))


def _ssm_inproj_kernel(x_ref, gain_ref, wut_ref, ut_ref):
    n_tok = x_ref.shape[0] * x_ref.shape[1]
    for p in range(SSM_PHASES):
        x = x_ref[:, :, p * D_MODEL:(p + 1) * D_MODEL].reshape(n_tok, D_MODEL)
        h = _rms_norm(x, gain_ref[...]).astype(BF16)
        ut = lax.dot_general(wut_ref[...], h, NT_DIMS, preferred_element_type=F32).astype(BF16)
        for g in range(SSM_GROUPS):
            ut_ref[g, p * SSM_GROUP:(p + 1) * SSM_GROUP, :] = ut[g * SSM_GROUP:(g + 1) * SSM_GROUP, :]


def _ssm_inproj(x2, gain, w_u, bsz, seq):
    n_tok_cols = bsz * seq // SSM_CHUNK
    return pl.pallas_call(
        _ssm_inproj_kernel,
        grid=_phase_grid(seq),
        in_specs=[_phase_spec(bsz, D_MODEL), _full((1, D_MODEL)), _full((D_SSM, D_MODEL))],
        out_specs=_transposed_spec(bsz),
        out_shape=jax.ShapeDtypeStruct((SSM_GROUPS, SSM_ROW, n_tok_cols), BF16),
        compiler_params=_compiler_params(2),
        name="ssm_inproj",
    )(_phase_view(x2, bsz, seq, D_MODEL), gain[None, :], w_u.T.astype(BF16))


def _ssm_tables(lambda_re, lambda_im, log_step, b_re, b_im, c_re, c_im, d_skip):
    hp = lax.Precision.HIGHEST
    step = jnp.exp(log_step)[:, None]
    mag = jnp.exp(lambda_re * step)
    ang = lambda_im * step
    a_re = mag * jnp.cos(ang)
    a_im = mag * jnp.sin(ang)
    num_re = a_re - 1.0
    num_im = a_im
    den = lambda_re * lambda_re + lambda_im * lambda_im
    f_re = (num_re * lambda_re + num_im * lambda_im) / den
    f_im = (num_im * lambda_re - num_re * lambda_im) / den
    bb_re = f_re[..., None] * b_re - f_im[..., None] * b_im
    bb_im = f_re[..., None] * b_im + f_im[..., None] * b_re
    tau = jnp.arange(SSM_CHUNK + 1, dtype=F32)[:, None, None]
    pmag = jnp.exp(tau * (lambda_re * step)[None])
    pw_re = pmag * jnp.cos(tau * ang[None])
    pw_im = pmag * jnp.sin(tau * ang[None])
    cp_re = c_re[None] * pw_re[:, :, None, :] - c_im[None] * pw_im[:, :, None, :]
    cp_im = c_re[None] * pw_im[:, :, None, :] + c_im[None] * pw_re[:, :, None, :]
    kern = (jnp.einsum('tgop,gpi->tgoi', cp_re[:SSM_CHUNK], bb_re, precision=hp)
            - jnp.einsum('tgop,gpi->tgoi', cp_im[:SSM_CHUNK], bb_im, precision=hp))
    tj = jnp.arange(SSM_CHUNK)
    lag = tj[None, :] - tj[:, None]
    kj = kern[jnp.clip(lag, 0, SSM_CHUNK - 1)]
    kj = jnp.where((lag >= 0)[:, :, None, None, None], kj, 0.0)
    m1 = jnp.transpose(kj, (2, 0, 4, 1, 3))
    d_g = d_skip.reshape(SSM_GROUPS, SSM_GROUP)
    eye_t = jnp.eye(SSM_CHUNK, dtype=F32)
    eye_c = jnp.eye(SSM_GROUP, dtype=F32)
    m1 = m1 + (eye_t[None, :, None, :, None] * eye_c[None, None, :, None, :]
               * d_g[:, None, None, None, :])
    m1 = m1.reshape(SSM_GROUPS, SSM_ROW, SSM_ROW)
    rev_re = pw_re[SSM_CHUNK - 1::-1]
    rev_im = pw_im[SSM_CHUNK - 1::-1]
    bt_re = jnp.transpose(bb_re, (0, 2, 1))
    bt_im = jnp.transpose(bb_im, (0, 2, 1))
    rr = jnp.transpose(rev_re, (1, 0, 2))[:, :, None, :]
    ri = jnp.transpose(rev_im, (1, 0, 2))[:, :, None, :]
    m2_re = rr * bt_re[:, None] - ri * bt_im[:, None]
    m2_im = rr * bt_im[:, None] + ri * bt_re[:, None]
    m2 = jnp.concatenate([m2_re, m2_im], axis=-1).reshape(SSM_GROUPS, SSM_ROW, 2 * SSM_STATE)
    m3_re = jnp.transpose(cp_re[1:], (1, 3, 0, 2)).reshape(SSM_GROUPS, SSM_STATE, SSM_ROW)
    m3_im = -jnp.transpose(cp_im[1:], (1, 3, 0, 2)).reshape(SSM_GROUPS, SSM_STATE, SSM_ROW)
    m3 = jnp.concatenate([m3_re, m3_im], axis=1)
    ar, ai = pw_re[SSM_CHUNK], pw_im[SSM_CHUNK]
    a_chunk = jnp.stack([jnp.concatenate([ar, ar], -1), jnp.concatenate([-ai, ai], -1),
                         jnp.concatenate([ai, -ai], -1)], axis=1)
    tr = lambda m: jnp.transpose(m, (0, 2, 1)).astype(BF16)
    return tr(m1), tr(m2), tr(m3), a_chunk


def _ssm_kernel(ut_ref, m1t_ref, m2t_ref, m3t_ref, a_ref, yt_ref, p_scr, q_scr, s_scr, *, bsz):
    ut = ut_ref[...]
    n_tok = ut.shape[1]
    wt = jnp.dot(m2t_ref[...], ut, preferred_element_type=F32)
    w = wt.T
    p_scr[...] = w
    q_scr[...] = pltpu.roll(w, SSM_STATE, 1)
    a1 = a_ref[0:1, :]
    a2 = a_ref[1:2, :]
    a3 = a_ref[2:3, :]

    def block(kb, carry):
        cp, cq = carry
        base = pl.multiple_of(kb * (bsz * SSM_KBLOCK), bsz * SSM_KBLOCK)
        for k in range(SSM_KBLOCK):
            rows = pl.ds(base + k, bsz, stride=SSM_KBLOCK)
            s_scr[rows, :] = cp
            cp, cq = (a1 * cp + a2 * cq + p_scr[rows, :],
                      a1 * cq + a3 * cp + q_scr[rows, :])
        return cp, cq

    zero = jnp.zeros((bsz, 2 * SSM_STATE), F32)
    lax.fori_loop(0, n_tok // (bsz * SSM_KBLOCK), block, (zero, zero))
    yt = jnp.dot(m1t_ref[...], ut, preferred_element_type=F32)
    yt = yt + lax.dot_general(m3t_ref[...], s_scr[...].astype(BF16), NT_DIMS,
                              preferred_element_type=F32)
    yt_ref[...] = yt


def _ssm(ut, tables, bsz):
    m1t, m2t, m3t, a_chunk = tables
    n_tok = ut.shape[2]
    per_group = lambda a, b: pl.BlockSpec((None, a, b), lambda g: (g, 0, 0))
    state = pltpu.VMEM((n_tok, 2 * SSM_STATE), F32)
    return pl.pallas_call(
        functools.partial(_ssm_kernel, bsz=bsz),
        grid=(SSM_GROUPS,),
        in_specs=[per_group(SSM_ROW, n_tok), per_group(SSM_ROW, SSM_ROW),
                  per_group(2 * SSM_STATE, SSM_ROW), per_group(SSM_ROW, 2 * SSM_STATE),
                  per_group(3, 2 * SSM_STATE)],
        out_specs=per_group(SSM_ROW, n_tok),
        out_shape=jax.ShapeDtypeStruct((SSM_GROUPS, SSM_ROW, n_tok), F32),
        scratch_shapes=[state, state, state],
        compiler_params=_compiler_params(1),
        name="ssm",
    )(ut, m1t, m2t, m3t, a_chunk)


def _split_bf16(a):
    hi = a.astype(BF16)
    lo = (a - hi.astype(F32)).astype(BF16)
    return hi, lo


def _route(logits):
    lane = lax.broadcasted_iota(jnp.int32, logits.shape, 1)
    lane_f = lane.astype(F32)
    big = float(ROUTER_LANES)

    def first_argmax(vals, vmax):
        return jnp.min(jnp.where(vals == vmax, lane_f, big), axis=-1, keepdims=True)

    g_mask = (lane >= N_EXPERTS) & (lane < N_EXPERTS + N_GROUPS)
    g_log = jnp.where(g_mask, logits, NEG_INF)
    g_max = jnp.max(g_log, axis=-1, keepdims=True)
    g_sum = jnp.sum(jnp.where(g_mask, jnp.exp(g_log - g_max), 0.0), axis=-1, keepdims=True)
    g_prob = 1.0 / g_sum
    g_idx = first_argmax(g_log, g_max) - float(N_EXPERTS)
    e_lo = g_idx * float(EXPERTS_PER_GROUP)
    e_mask = (lane_f >= e_lo) & (lane_f < e_lo + float(EXPERTS_PER_GROUP))
    e_log = jnp.where(e_mask, logits, NEG_INF)
    e_max1 = jnp.max(e_log, axis=-1, keepdims=True)
    idx1 = first_argmax(e_log, e_max1)
    e_log2 = jnp.where(lane_f == idx1, NEG_INF, e_log)
    e_max2 = jnp.max(e_log2, axis=-1, keepdims=True)
    idx2 = first_argmax(e_log2, e_max2)
    r = jnp.exp(e_max2 - e_max1)
    w1 = g_prob / (1.0 + r)
    w2 = g_prob * r / (1.0 + r)
    return jnp.where(lane_f == idx1, w1, 0.0) + jnp.where(lane_f == idx2, w2, 0.0)


def _merge_kernel(x_ref, ya_ref, yt_ref, gain1_ref, wgate_ref, bgate_ref, wglut_ref, bglu_ref,
                  wba_ref, wbs_ref, wout_ref, gain2_ref, rhi_ref, rlo_ref, rbias_ref,
                  x1_ref, h_ref, gates_ref):
    blk = x_ref.shape[:2]
    n_tok = blk[0] * blk[1]
    for p in range(SSM_PHASES):
        x = x_ref[:, :, p * D_MODEL:(p + 1) * D_MODEL].reshape(n_tok, D_MODEL)
        hn = _rms_norm(x, gain1_ref[...]).astype(BF16)
        y_t = jnp.concatenate(
            [yt_ref[g, p * SSM_GROUP:(p + 1) * SSM_GROUP, :] for g in range(SSM_GROUPS)], axis=0)
        z_t = jax.nn.gelu(y_t)
        glu_t = jnp.dot(wglut_ref[...], z_t.astype(BF16), preferred_element_type=F32) + bglu_ref[...]
        y_ssm = (z_t * jax.nn.sigmoid(glu_t)).T
        y_attn = ya_ref[:, :, p * D_ATTN:(p + 1) * D_ATTN].reshape(n_tok, D_ATTN)
        merged = None
        for s, (y_b, wb_ref) in enumerate(((y_attn, wba_ref), (y_ssm.astype(BF16), wbs_ref))):
            cols = slice(s * D_MODEL, (s + 1) * D_MODEL)
            gate = jax.nn.sigmoid(jnp.dot(hn, wgate_ref[:, cols], preferred_element_type=F32)
                                  + bgate_ref[:, cols])
            term = gate * jnp.dot(y_b, wb_ref[...], preferred_element_type=F32)
            merged = term if merged is None else merged + term
        x1 = x + jnp.dot(merged.astype(BF16), wout_ref[...], preferred_element_type=F32)
        x1_ref[:, :, p * D_MODEL:(p + 1) * D_MODEL] = x1.reshape(*blk, D_MODEL)

        h = _rms_norm(x1, gain2_ref[...])
        h_ref[:, :, p * D_MODEL:(p + 1) * D_MODEL] = h.astype(BF16).reshape(*blk, D_MODEL)
        h_hi, h_lo = _split_bf16(h)
        logits = (jnp.dot(h_hi, rhi_ref[...], preferred_element_type=F32)
                  + jnp.dot(h_hi, rlo_ref[...], preferred_element_type=F32)
                  + jnp.dot(h_lo, rhi_ref[...], preferred_element_type=F32)
                  + rbias_ref[...])
        gates_ref[:, :, p * ROUTER_LANES:(p + 1) * ROUTER_LANES] = (
            _route(logits).reshape(*blk, ROUTER_LANES))


def _merge(x2, y_attn, y_t, gain1, w_gate, b_gate, w_glu, b_glu, w_branch, w_out, ffn_gain,
           w_group_router, group_bias, w_expert_router, expert_bias, bsz, seq):
    t = x2.shape[0]
    pad = ROUTER_LANES - N_EXPERTS - N_GROUPS
    w_r = jnp.concatenate([w_expert_router, w_group_router,
                           jnp.zeros((D_MODEL, pad), F32)], axis=1)
    r_hi, r_lo = _split_bf16(w_r)
    r_bias = jnp.concatenate([expert_bias, group_bias, jnp.zeros((pad,), F32)])[None, :]
    resident = lambda shape: pl.BlockSpec(shape, lambda kb, pg: (0, 0),
                                          pipeline_mode=pl.Buffered(1))
    x1, h, gates = pl.pallas_call(
        _merge_kernel,
        grid=_phase_grid(seq),
        in_specs=[_phase_spec(bsz, D_MODEL), _phase_spec(bsz, D_ATTN), _transposed_spec(bsz),
                  resident((1, D_MODEL)), resident((D_MODEL, N_BRANCH * D_MODEL)),
                  resident((1, N_BRANCH * D_MODEL)),
                  resident((D_SSM, D_SSM)), resident((D_SSM, 1)),
                  resident((D_ATTN, D_MODEL)), resident((D_SSM, D_MODEL)),
                  resident((D_MODEL, D_MODEL)), resident((1, D_MODEL)),
                  resident((D_MODEL, ROUTER_LANES)), resident((D_MODEL, ROUTER_LANES)),
                  resident((1, ROUTER_LANES))],
        out_specs=[_phase_spec(bsz, D_MODEL), _phase_spec(bsz, D_MODEL),
                   _phase_spec(bsz, ROUTER_LANES)],
        out_shape=[jax.ShapeDtypeStruct(_phase_shape(bsz, seq, D_MODEL), F32),
                   jax.ShapeDtypeStruct(_phase_shape(bsz, seq, D_MODEL), BF16),
                   jax.ShapeDtypeStruct(_phase_shape(bsz, seq, ROUTER_LANES), F32)],
        compiler_params=_compiler_params(2),
        name="merge",
    )(_phase_view(x2, bsz, seq, D_MODEL), _phase_view(y_attn, bsz, seq, D_ATTN), y_t,
      gain1[None, :], w_gate.astype(BF16), b_gate[None, :], w_glu.T.astype(BF16), b_glu[:, None],
      w_branch[:D_ATTN].astype(BF16), w_branch[D_ATTN:].astype(BF16),
      w_out.astype(BF16), ffn_gain[None, :], r_hi, r_lo, r_bias)
    return x1.reshape(t, D_MODEL), h.reshape(t, D_MODEL), gates.reshape(t, ROUTER_LANES)


def _moe_kernel(x1_ref, h_ref, gates_ref, wg_ref, wu_ref, wd_ref, o_ref):
    h = h_ref[...]
    gates = gates_ref[...]
    acc = x1_ref[...]
    width = EXPERTS_PER_GROUP * D_EXPERT
    for g in range(N_GROUPS):
        cols = slice(g * width, (g + 1) * width)
        a = jnp.dot(h, wg_ref[:, cols], preferred_element_type=F32)
        u = jnp.dot(h, wu_ref[:, cols], preferred_element_type=F32)
        hid = jax.nn.silu(a) * u
        parts = []
        for j in range(EXPERTS_PER_GROUP):
            e = g * EXPERTS_PER_GROUP + j
            parts.append(hid[:, j * D_EXPERT:(j + 1) * D_EXPERT] * gates[:, e:e + 1])
        hid = jnp.concatenate(parts, axis=1).astype(BF16)
        acc = acc + jnp.dot(hid, wd_ref[cols, :], preferred_element_type=F32)
    o_ref[...] = acc


def _moe(x1, h, gates, w_e_gate, w_e_up, w_e_down, tm):
    t = x1.shape[0]
    hidden = N_EXPERTS * D_EXPERT
    wg = jnp.transpose(w_e_gate, (1, 0, 2)).reshape(D_MODEL, hidden).astype(BF16)
    wu = jnp.transpose(w_e_up, (1, 0, 2)).reshape(D_MODEL, hidden).astype(BF16)
    wd = w_e_down.reshape(hidden, D_MODEL).astype(BF16)
    row = lambda n: pl.BlockSpec((tm, n), lambda i: (i, 0))
    resident = lambda shape: pl.BlockSpec(shape, lambda i: (0, 0), pipeline_mode=pl.Buffered(1))
    return pl.pallas_call(
        _moe_kernel,
        grid=(t // tm,),
        in_specs=[row(D_MODEL), row(D_MODEL), row(ROUTER_LANES),
                  resident((D_MODEL, hidden)), resident((D_MODEL, hidden)),
                  resident((hidden, D_MODEL))],
        out_specs=row(D_MODEL),
        out_shape=jax.ShapeDtypeStruct((t, D_MODEL), F32),
        compiler_params=_compiler_params(1),
        name="moe",
    )(x1, h, gates, wg, wu, wd)


def kernel(x, mix_norm_gain, w_in, b_gate, q_gain, k_gain, rel_bias, ssm_lambda_re, ssm_lambda_im, ssm_log_step, ssm_b_re, ssm_b_im, ssm_c_re, ssm_c_im, ssm_d, w_glu, b_glu, w_branch, w_out, ffn_norm_gain, w_group_router, group_bias, w_expert_router, expert_bias, w_e_gate, w_e_up, w_e_down):
    bsz, seq, _ = x.shape
    assert seq % (SSM_KBLOCK * SSM_CHUNK) == 0 and seq % (2 * CHUNK) == 0
    depth = w_in.shape[0]
    tiles = _tiles(bsz * seq)
    x2 = x.reshape(bsz * seq, D_MODEL)
    for i in range(depth):
        w_qkv = w_in[i][:, :D_QKV]
        w_u = w_in[i][:, D_QKV:D_QKV + D_SSM]
        w_gate = w_in[i][:, D_QKV + D_SSM:]
        q, k, v = _inproj(x2, mix_norm_gain[i], w_qkv, q_gain[i], k_gain[i], tiles["inproj"])
        y_attn = _attention(q, k, v, _rel_bias_table(rel_bias[i]), bsz, seq)
        u_t = _ssm_inproj(x2, mix_norm_gain[i], w_u, bsz, seq)
        tables = _ssm_tables(ssm_lambda_re[i], ssm_lambda_im[i], ssm_log_step[i],
                             ssm_b_re[i], ssm_b_im[i], ssm_c_re[i], ssm_c_im[i], ssm_d[i])
        y_t = _ssm(u_t, tables, bsz)
        x1, h, gates = _merge(x2, y_attn, y_t, mix_norm_gain[i], w_gate, b_gate[i], w_glu[i],
                              b_glu[i], w_branch[i], w_out[i], ffn_norm_gain[i],
                              w_group_router[i], group_bias[i], w_expert_router[i],
                              expert_bias[i], bsz, seq)
        x2 = _moe(x1, h, gates, w_e_gate[i], w_e_up[i], w_e_down[i], tiles["moe"])
    return x2.reshape(bsz, seq, D_MODEL)
```

```python
import functools

import jax
import jax.numpy as jnp
from jax import lax
from jax.experimental import pallas as pl
from jax.experimental.pallas import tpu as pltpu

F32 = jnp.float32
BF16 = jnp.bfloat16

D_MODEL = 1024
CHUNK = 64
LEFT_CHUNKS = 8
BAND = LEFT_CHUNKS + 1
BAND_KEYS = BAND * CHUNK
PAD_KEYS = LEFT_CHUNKS * CHUNK
ATTN_HEADS = 8
HEAD_DIM = 64
D_ATTN = ATTN_HEADS * HEAD_DIM
MAX_REL = 256
REL_FUTURE = CHUNK - 1
D_SSM = D_MODEL // 2
SSM_GROUP = 16
SSM_GROUPS = D_SSM // SSM_GROUP
SSM_STATE = 64
N_BRANCH = 2
D_QKV = 3 * D_ATTN
D_IN = D_QKV + D_SSM + N_BRANCH * D_MODEL
N_GROUPS = 4
EXPERTS_PER_GROUP = 4
N_EXPERTS = N_GROUPS * EXPERTS_PER_GROUP
D_EXPERT = D_MODEL // 4
EPS = 1e-6
NEG_INF = -1e30
LOG2_E = 1.4426950408889634

LANES = 128
HEAD_PAIRS = D_ATTN // LANES
SSM_CHUNK = 16
SSM_ROW = SSM_CHUNK * SSM_GROUP
SSM_KBLOCK = 16
SSM_BGROUP = LANES // SSM_KBLOCK
ROUTER_LANES = LANES
VMEM_LIMIT = 56 * 1024 * 1024


def _tiles(n_tokens):
    return dict(inproj=512, merge=512, moe=512)


def _compiler_params(n_axes):
    return pltpu.CompilerParams(
        dimension_semantics=("arbitrary",) * n_axes,
        vmem_limit_bytes=VMEM_LIMIT)


def _full(shape):
    nd = len(shape)
    return pl.BlockSpec(shape, lambda *_: (0,) * nd)


def _rms_norm(x, gain):
    ms = jnp.mean(x * x, axis=-1, keepdims=True)
    return x * lax.rsqrt(ms + EPS) * gain


NT_DIMS = (((1,), (1,)), ((), ()))


def _inproj_kernel(x_ref, gain_ref, w_ref, qg_ref, kg_ref, ones_ref, q_ref, k_ref, v_ref):
    h = _rms_norm(x_ref[...], gain_ref[...]).astype(BF16)

    def proj(lo, n):
        return jnp.dot(h, w_ref[:, lo:lo + n], preferred_element_type=F32)

    def head_norm(t, g_ref):
        ss = jnp.dot((t * t).astype(BF16), ones_ref[...], preferred_element_type=F32)
        return t * lax.rsqrt(ss * (1.0 / HEAD_DIM) + EPS) * g_ref[...]

    q_ref[...] = head_norm(proj(0, D_ATTN), qg_ref).astype(BF16)
    k_ref[...] = head_norm(proj(D_ATTN, D_ATTN), kg_ref).astype(BF16)
    v_ref[...] = proj(2 * D_ATTN, D_ATTN).astype(BF16)


def _inproj(x2, gain, w_qkv, q_gain, k_gain, tm):
    t = x2.shape[0]
    ones = jnp.kron(jnp.eye(ATTN_HEADS, dtype=F32), jnp.ones((HEAD_DIM, HEAD_DIM), F32)).astype(BF16)
    qg = jnp.tile(q_gain, ATTN_HEADS)[None, :] * (HEAD_DIM ** -0.5 * LOG2_E)
    kg = jnp.tile(k_gain, ATTN_HEADS)[None, :]
    row = lambda n: pl.BlockSpec((tm, n), lambda i: (i, 0))
    return pl.pallas_call(
        _inproj_kernel,
        grid=(t // tm,),
        in_specs=[row(D_MODEL), _full((1, D_MODEL)), _full((D_MODEL, D_QKV)),
                  _full((1, D_ATTN)), _full((1, D_ATTN)), _full((D_ATTN, D_ATTN))],
        out_specs=[row(D_ATTN)] * 3,
        out_shape=[jax.ShapeDtypeStruct((t, D_ATTN), BF16)] * 3,
        compiler_params=_compiler_params(1),
        name="inproj",
    )(x2, gain[None, :], w_qkv.astype(BF16), qg, kg, ones)


def _attn_kernel(q_ref, k_ref, v_ref, bias_ref, o_ref, kpad, v_even, v_odd, s_a, s_b):
    seq = q_ref.shape[0]
    n_chunks = seq // CHUNK
    lane_v = lax.broadcasted_iota(jnp.int32, (seq, D_ATTN), 1)
    even_v = (lane_v % LANES) < HEAD_DIM
    zeros = jnp.zeros((PAD_KEYS, D_ATTN), BF16)
    v = v_ref[...]
    kpad[0:PAD_KEYS, :] = zeros
    v_even[0:PAD_KEYS, :] = zeros
    v_odd[0:PAD_KEYS, :] = zeros
    kpad[PAD_KEYS:, :] = k_ref[...]
    v_even[PAD_KEYS:, :] = jnp.where(even_v, v, jnp.zeros_like(v))
    v_odd[PAD_KEYS:, :] = jnp.where(even_v, jnp.zeros_like(v), v)

    lane = lax.broadcasted_iota(jnp.int32, (CHUNK, LANES), 1)
    first_head = lane < HEAD_DIM
    key_col = lax.broadcasted_iota(jnp.int32, (2 * CHUNK, BAND_KEYS), 1)

    def scores(n, s_buf):
        r0 = pl.multiple_of(n * CHUNK, CHUNK)
        for j in range(HEAD_PAIRS):
            cols = slice(j * LANES, (j + 1) * LANES)
            qp = q_ref[pl.ds(r0, CHUNK), cols]
            zq = jnp.zeros_like(qp)
            q2 = jnp.concatenate([jnp.where(first_head, qp, zq), jnp.where(first_head, zq, qp)], axis=0)
            kb = kpad[pl.ds(r0, BAND_KEYS), cols]
            s_buf[j * 2 * CHUNK:(j + 1) * 2 * CHUNK, :] = lax.dot_general(
                q2, kb, NT_DIMS, preferred_element_type=F32)

    def softmax_pv(n, s_buf, masked):
        r0 = pl.multiple_of(n * CHUNK, CHUNK)
        for j in range(HEAD_PAIRS):
            cols = slice(j * LANES, (j + 1) * LANES)
            rows = slice(j * 2 * CHUNK, (j + 1) * 2 * CHUNK)
            s = s_buf[rows, :] + bias_ref[rows, :]
            if masked:
                s = jnp.where(key_col >= (LEFT_CHUNKS - n) * CHUNK, s, NEG_INF)
            m = jnp.max(s, axis=-1, keepdims=True)
            p = jnp.exp2(s - m)
            inv_l = 1.0 / jnp.sum(p, axis=-1, keepdims=True)
            pb = p.astype(BF16)
            o0 = jnp.dot(pb[:CHUNK], v_even[pl.ds(r0, BAND_KEYS), cols], preferred_element_type=F32)
            o1 = jnp.dot(pb[CHUNK:], v_odd[pl.ds(r0, BAND_KEYS), cols], preferred_element_type=F32)
            o_ref[pl.ds(r0, CHUNK), cols] = (o0 * inv_l[:CHUNK] + o1 * inv_l[CHUNK:]).astype(BF16)

    def pair_body(i, carry, *, masked):
        n0 = 2 * i
        scores(n0 + 1, s_b)
        softmax_pv(n0, s_a, masked)
        scores(jnp.minimum(n0 + 2, n_chunks - 1), s_a)
        softmax_pv(n0 + 1, s_b, masked)
        return carry

    assert n_chunks % 2 == 0 and LEFT_CHUNKS % 2 == 0
    masked_pairs = min(LEFT_CHUNKS, n_chunks) // 2
    scores(0, s_a)
    lax.fori_loop(0, masked_pairs, functools.partial(pair_body, masked=True), 0)
    lax.fori_loop(masked_pairs, n_chunks // 2, functools.partial(pair_body, masked=False), 0)


def _attention(q, k, v, bias, bsz, seq):
    q3, k3, v3 = (a.reshape(bsz, seq, D_ATTN) for a in (q, k, v))
    seq_spec = pl.BlockSpec((None, seq, D_ATTN), lambda b: (b, 0, 0))
    padded = pltpu.VMEM((PAD_KEYS + seq, D_ATTN), BF16)
    out = pl.pallas_call(
        _attn_kernel,
        grid=(bsz,),
        in_specs=[seq_spec, seq_spec, seq_spec, _full((ATTN_HEADS * CHUNK, BAND_KEYS))],
        out_specs=seq_spec,
        out_shape=jax.ShapeDtypeStruct((bsz, seq, D_ATTN), BF16),
        scratch_shapes=[padded, padded, padded,
                        pltpu.VMEM((ATTN_HEADS * CHUNK, BAND_KEYS), F32),
                        pltpu.VMEM((ATTN_HEADS * CHUNK, BAND_KEYS), F32)],
        compiler_params=_compiler_params(1),
        name="attention",
    )(q3, k3, v3, bias)
    return out.reshape(bsz * seq, D_ATTN)


def _rel_bias_table(rel_bias):
    q_off = jnp.arange(CHUNK)[:, None] + PAD_KEYS
    k_off = jnp.arange(BAND_KEYS)[None, :]
    rel_idx = jnp.clip(q_off - k_off, -REL_FUTURE, MAX_REL) + REL_FUTURE
    return (rel_bias[:, rel_idx] * LOG2_E).reshape(ATTN_HEADS * CHUNK, BAND_KEYS)


SSM_BLOCK_LEN = SSM_KBLOCK * SSM_CHUNK
SSM_BLOCK_TOK = SSM_BGROUP * SSM_KBLOCK


def _token_block_spec(width):
    return pl.BlockSpec((SSM_BGROUP, SSM_BLOCK_LEN, width), lambda kb, bg: (bg, kb, 0))


def _transposed_spec(n_bgroups):
    return pl.BlockSpec((SSM_GROUPS, SSM_ROW, SSM_BLOCK_TOK),
                        lambda kb, bg: (0, 0, kb * n_bgroups + bg))


def _token_grid(bsz, seq):
    assert bsz % SSM_BGROUP == 0 and seq % SSM_BLOCK_LEN == 0
    return (seq // SSM_BLOCK_LEN, bsz // SSM_BGROUP)


def _ssm_inproj_kernel(x_ref, gain_ref, wu_ref, ut_ref, u_scr):
    n_rows = SSM_BGROUP * SSM_BLOCK_LEN
    x = x_ref[...].reshape(n_rows, D_MODEL)
    h = _rms_norm(x, gain_ref[...]).astype(BF16)
    u = jnp.dot(h, wu_ref[...], preferred_element_type=F32)
    for q in range(D_SSM // LANES):
        u_scr[q] = u[:, q * LANES:(q + 1) * LANES]
    groups_per_tile = LANES // SSM_GROUP
    for t in range(SSM_CHUNK):
        for q in range(D_SSM // LANES):
            ut = u_scr[q, pl.ds(t, SSM_BLOCK_TOK, stride=SSM_CHUNK), :].T.astype(BF16)
            for gl in range(groups_per_tile):
                ut_ref[q * groups_per_tile + gl, t * SSM_GROUP:(t + 1) * SSM_GROUP, :] = (
                    ut[gl * SSM_GROUP:(gl + 1) * SSM_GROUP, :])


def _ssm_inproj(x2, gain, w_u, bsz, seq):
    grid = _token_grid(bsz, seq)
    return pl.pallas_call(
        _ssm_inproj_kernel,
        grid=grid,
        in_specs=[_token_block_spec(D_MODEL), _full((1, D_MODEL)), _full((D_MODEL, D_SSM))],
        out_specs=_transposed_spec(grid[1]),
        out_shape=jax.ShapeDtypeStruct((SSM_GROUPS, SSM_ROW, bsz * seq // SSM_CHUNK), BF16),
        scratch_shapes=[pltpu.VMEM((D_SSM // LANES, SSM_BGROUP * SSM_BLOCK_LEN, LANES), F32)],
        compiler_params=_compiler_params(2),
        name="ssm_inproj",
    )(x2.reshape(bsz, seq, D_MODEL), gain[None, :], w_u.astype(BF16))


def _ssm_out_kernel(yt_ref, z_ref, y_scr):
    groups_per_tile = LANES // SSM_GROUP
    for t in range(SSM_CHUNK):
        for q in range(D_SSM // LANES):
            yt = jnp.concatenate(
                [yt_ref[q * groups_per_tile + gl, t * SSM_GROUP:(t + 1) * SSM_GROUP, :]
                 for gl in range(groups_per_tile)], axis=0)
            y_scr[q, pl.ds(t, SSM_BLOCK_TOK, stride=SSM_CHUNK), :] = yt.T
    for q in range(D_SSM // LANES):
        z = jax.nn.gelu(y_scr[q])
        z_ref[:, :, q * LANES:(q + 1) * LANES] = z.astype(BF16).reshape(
            SSM_BGROUP, SSM_BLOCK_LEN, LANES)


def _ssm_out(y_t, bsz, seq):
    grid = _token_grid(bsz, seq)
    z = pl.pallas_call(
        _ssm_out_kernel,
        grid=grid,
        in_specs=[_transposed_spec(grid[1])],
        out_specs=_token_block_spec(D_SSM),
        out_shape=jax.ShapeDtypeStruct((bsz, seq, D_SSM), BF16),
        scratch_shapes=[pltpu.VMEM((D_SSM // LANES, SSM_BGROUP * SSM_BLOCK_LEN, LANES), F32)],
        compiler_params=_compiler_params(2),
        name="ssm_out",
    )(y_t)
    return z.reshape(bsz * seq, D_SSM)


def _ssm_tables(lambda_re, lambda_im, log_step, b_re, b_im, c_re, c_im, d_skip):
    hp = lax.Precision.HIGHEST
    step = jnp.exp(log_step)[:, None]
    mag = jnp.exp(lambda_re * step)
    ang = lambda_im * step
    a_re = mag * jnp.cos(ang)
    a_im = mag * jnp.sin(ang)
    num_re = a_re - 1.0
    num_im = a_im
    den = lambda_re * lambda_re + lambda_im * lambda_im
    f_re = (num_re * lambda_re + num_im * lambda_im) / den
    f_im = (num_im * lambda_re - num_re * lambda_im) / den
    bb_re = f_re[..., None] * b_re - f_im[..., None] * b_im
    bb_im = f_re[..., None] * b_im + f_im[..., None] * b_re
    tau = jnp.arange(SSM_CHUNK + 1, dtype=F32)[:, None, None]
    pmag = jnp.exp(tau * (lambda_re * step)[None])
    pw_re = pmag * jnp.cos(tau * ang[None])
    pw_im = pmag * jnp.sin(tau * ang[None])
    cp_re = c_re[None] * pw_re[:, :, None, :] - c_im[None] * pw_im[:, :, None, :]
    cp_im = c_re[None] * pw_im[:, :, None, :] + c_im[None] * pw_re[:, :, None, :]
    kern = (jnp.einsum('tgop,gpi->tgoi', cp_re[:SSM_CHUNK], bb_re, precision=hp)
            - jnp.einsum('tgop,gpi->tgoi', cp_im[:SSM_CHUNK], bb_im, precision=hp))
    tj = jnp.arange(SSM_CHUNK)
    lag = tj[None, :] - tj[:, None]
    kj = kern[jnp.clip(lag, 0, SSM_CHUNK - 1)]
    kj = jnp.where((lag >= 0)[:, :, None, None, None], kj, 0.0)
    m1 = jnp.transpose(kj, (2, 0, 4, 1, 3))
    d_g = d_skip.reshape(SSM_GROUPS, SSM_GROUP)
    eye_t = jnp.eye(SSM_CHUNK, dtype=F32)
    eye_c = jnp.eye(SSM_GROUP, dtype=F32)
    m1 = m1 + (eye_t[None, :, None, :, None] * eye_c[None, None, :, None, :]
               * d_g[:, None, None, None, :])
    m1 = m1.reshape(SSM_GROUPS, SSM_ROW, SSM_ROW)
    rev_re = pw_re[SSM_CHUNK - 1::-1]
    rev_im = pw_im[SSM_CHUNK - 1::-1]
    bt_re = jnp.transpose(bb_re, (0, 2, 1))
    bt_im = jnp.transpose(bb_im, (0, 2, 1))
    rr = jnp.transpose(rev_re, (1, 0, 2))[:, :, None, :]
    ri = jnp.transpose(rev_im, (1, 0, 2))[:, :, None, :]
    m2_re = rr * bt_re[:, None] - ri * bt_im[:, None]
    m2_im = rr * bt_im[:, None] + ri * bt_re[:, None]
    m2 = jnp.concatenate([m2_re, m2_im], axis=-1).reshape(SSM_GROUPS, SSM_ROW, 2 * SSM_STATE)
    m3_re = jnp.transpose(cp_re[1:], (1, 3, 0, 2)).reshape(SSM_GROUPS, SSM_STATE, SSM_ROW)
    m3_im = -jnp.transpose(cp_im[1:], (1, 3, 0, 2)).reshape(SSM_GROUPS, SSM_STATE, SSM_ROW)
    m3 = jnp.concatenate([m3_re, m3_im], axis=1)
    ar, ai = pw_re[SSM_CHUNK], pw_im[SSM_CHUNK]
    a_chunk = jnp.stack([jnp.concatenate([ar, ar], -1), jnp.concatenate([-ai, ai], -1),
                         jnp.concatenate([ai, -ai], -1)], axis=1)
    tr = lambda m: jnp.transpose(m, (0, 2, 1)).astype(BF16)
    return tr(m1), tr(m2), tr(m3), a_chunk


def _ssm_kernel(ut_ref, m1t_ref, m2t_ref, m3t_ref, a_ref, yt_ref, p_scr, q_scr, s_scr, *, bsz):
    ut = ut_ref[...]
    n_tok = ut.shape[1]
    wt = jnp.dot(m2t_ref[...], ut, preferred_element_type=F32)
    w = wt.T
    p_scr[...] = w
    q_scr[...] = pltpu.roll(w, SSM_STATE, 1)
    a1 = a_ref[0:1, :]
    a2 = a_ref[1:2, :]
    a3 = a_ref[2:3, :]

    def block(kb, carry):
        cp, cq = carry
        base = pl.multiple_of(kb * (bsz * SSM_KBLOCK), bsz * SSM_KBLOCK)
        for k in range(SSM_KBLOCK):
            rows = pl.ds(base + k, bsz, stride=SSM_KBLOCK)
            s_scr[rows, :] = cp
            cp, cq = (a1 * cp + a2 * cq + p_scr[rows, :],
                      a1 * cq + a3 * cp + q_scr[rows, :])
        return cp, cq

    zero = jnp.zeros((bsz, 2 * SSM_STATE), F32)
    lax.fori_loop(0, n_tok // (bsz * SSM_KBLOCK), block, (zero, zero))
    yt = jnp.dot(m1t_ref[...], ut, preferred_element_type=F32)
    yt = yt + lax.dot_general(m3t_ref[...], s_scr[...].astype(BF16), NT_DIMS,
                              preferred_element_type=F32)
    yt_ref[...] = yt


def _ssm(ut, tables, bsz):
    m1t, m2t, m3t, a_chunk = tables
    n_tok = ut.shape[2]
    per_group = lambda a, b: pl.BlockSpec((None, a, b), lambda g: (g, 0, 0))
    state = pltpu.VMEM((n_tok, 2 * SSM_STATE), F32)
    return pl.pallas_call(
        functools.partial(_ssm_kernel, bsz=bsz),
        grid=(SSM_GROUPS,),
        in_specs=[per_group(SSM_ROW, n_tok), per_group(SSM_ROW, SSM_ROW),
                  per_group(2 * SSM_STATE, SSM_ROW), per_group(SSM_ROW, 2 * SSM_STATE),
                  per_group(3, 2 * SSM_STATE)],
        out_specs=per_group(SSM_ROW, n_tok),
        out_shape=jax.ShapeDtypeStruct((SSM_GROUPS, SSM_ROW, n_tok), F32),
        scratch_shapes=[state, state, state],
        compiler_params=_compiler_params(1),
        name="ssm",
    )(ut, m1t, m2t, m3t, a_chunk)


def _split_bf16(a):
    hi = a.astype(BF16)
    lo = (a - hi.astype(F32)).astype(BF16)
    return hi, lo


def _route(logits):
    lane = lax.broadcasted_iota(jnp.int32, logits.shape, 1)
    lane_f = lane.astype(F32)
    big = float(ROUTER_LANES)

    def first_argmax(vals, vmax):
        return jnp.min(jnp.where(vals == vmax, lane_f, big), axis=-1, keepdims=True)

    g_mask = (lane >= N_EXPERTS) & (lane < N_EXPERTS + N_GROUPS)
    g_log = jnp.where(g_mask, logits, NEG_INF)
    g_max = jnp.max(g_log, axis=-1, keepdims=True)
    g_sum = jnp.sum(jnp.where(g_mask, jnp.exp(g_log - g_max), 0.0), axis=-1, keepdims=True)
    g_prob = 1.0 / g_sum
    g_idx = first_argmax(g_log, g_max) - float(N_EXPERTS)
    e_lo = g_idx * float(EXPERTS_PER_GROUP)
    e_mask = (lane_f >= e_lo) & (lane_f < e_lo + float(EXPERTS_PER_GROUP))
    e_log = jnp.where(e_mask, logits, NEG_INF)
    e_max1 = jnp.max(e_log, axis=-1, keepdims=True)
    idx1 = first_argmax(e_log, e_max1)
    e_log2 = jnp.where(lane_f == idx1, NEG_INF, e_log)
    e_max2 = jnp.max(e_log2, axis=-1, keepdims=True)
    idx2 = first_argmax(e_log2, e_max2)
    r = jnp.exp(e_max2 - e_max1)
    w1 = g_prob / (1.0 + r)
    w2 = g_prob * r / (1.0 + r)
    return jnp.where(lane_f == idx1, w1, 0.0) + jnp.where(lane_f == idx2, w2, 0.0)


def _merge_kernel(x_ref, ya_ref, z_ref, gain1_ref, wgate_ref, bgate_ref, wglu_ref, bglu_ref,
                  wba_ref, wbs_ref, wout_ref, gain2_ref, rhi_ref, rlo_ref, rbias_ref,
                  x1_ref, h_ref, gates_ref):
    x = x_ref[...]
    hn = _rms_norm(x, gain1_ref[...]).astype(BF16)
    z = z_ref[...]
    glu = jnp.dot(z, wglu_ref[...], preferred_element_type=F32) + bglu_ref[...]
    y_ssm = (z.astype(F32) * jax.nn.sigmoid(glu)).astype(BF16)
    merged = None
    for s, (y_b, wb_ref) in enumerate(((ya_ref[...], wba_ref), (y_ssm, wbs_ref))):
        cols = slice(s * D_MODEL, (s + 1) * D_MODEL)
        gate = jax.nn.sigmoid(jnp.dot(hn, wgate_ref[:, cols], preferred_element_type=F32)
                              + bgate_ref[:, cols])
        term = gate * jnp.dot(y_b, wb_ref[...], preferred_element_type=F32)
        merged = term if merged is None else merged + term
    x1 = x + jnp.dot(merged.astype(BF16), wout_ref[...], preferred_element_type=F32)
    x1_ref[...] = x1

    h = _rms_norm(x1, gain2_ref[...])
    h_ref[...] = h.astype(BF16)
    h_hi, h_lo = _split_bf16(h)
    logits = (jnp.dot(h_hi, rhi_ref[...], preferred_element_type=F32)
              + jnp.dot(h_hi, rlo_ref[...], preferred_element_type=F32)
              + jnp.dot(h_lo, rhi_ref[...], preferred_element_type=F32)
              + rbias_ref[...])
    gates_ref[...] = _route(logits)


def _merge(x2, y_attn, z, gain1, w_gate, b_gate, w_glu, b_glu, w_branch, w_out, ffn_gain,
           w_group_router, group_bias, w_expert_router, expert_bias, tm):
    t = x2.shape[0]
    pad = ROUTER_LANES - N_EXPERTS - N_GROUPS
    w_r = jnp.concatenate([w_expert_router, w_group_router,
                           jnp.zeros((D_MODEL, pad), F32)], axis=1)
    r_hi, r_lo = _split_bf16(w_r)
    r_bias = jnp.concatenate([expert_bias, group_bias, jnp.zeros((pad,), F32)])[None, :]
    row = lambda n: pl.BlockSpec((tm, n), lambda i: (i, 0))
    resident = lambda shape: pl.BlockSpec(shape, lambda i: (0, 0), pipeline_mode=pl.Buffered(1))
    return pl.pallas_call(
        _merge_kernel,
        grid=(t // tm,),
        in_specs=[row(D_MODEL), row(D_ATTN), row(D_SSM),
                  resident((1, D_MODEL)), resident((D_MODEL, N_BRANCH * D_MODEL)),
                  resident((1, N_BRANCH * D_MODEL)),
                  resident((D_SSM, D_SSM)), resident((1, D_SSM)),
                  resident((D_ATTN, D_MODEL)), resident((D_SSM, D_MODEL)),
                  resident((D_MODEL, D_MODEL)), resident((1, D_MODEL)),
                  resident((D_MODEL, ROUTER_LANES)), resident((D_MODEL, ROUTER_LANES)),
                  resident((1, ROUTER_LANES))],
        out_specs=[row(D_MODEL), row(D_MODEL), row(ROUTER_LANES)],
        out_shape=[jax.ShapeDtypeStruct((t, D_MODEL), F32),
                   jax.ShapeDtypeStruct((t, D_MODEL), BF16),
                   jax.ShapeDtypeStruct((t, ROUTER_LANES), F32)],
        compiler_params=_compiler_params(1),
        name="merge",
    )(x2, y_attn, z, gain1[None, :], w_gate.astype(BF16), b_gate[None, :],
      w_glu.astype(BF16), b_glu[None, :],
      w_branch[:D_ATTN].astype(BF16), w_branch[D_ATTN:].astype(BF16),
      w_out.astype(BF16), ffn_gain[None, :], r_hi, r_lo, r_bias)


def _moe_kernel(x1_ref, h_ref, gates_ref, wg_ref, wu_ref, wd_ref, o_ref):
    h = h_ref[...]
    gates = gates_ref[...]
    acc = x1_ref[...]
    width = EXPERTS_PER_GROUP * D_EXPERT
    for g in range(N_GROUPS):
        cols = slice(g * width, (g + 1) * width)
        a = jnp.dot(h, wg_ref[:, cols], preferred_element_type=F32)
        u = jnp.dot(h, wu_ref[:, cols], preferred_element_type=F32)
        hid = jax.nn.silu(a) * u
        parts = []
        for j in range(EXPERTS_PER_GROUP):
            e = g * EXPERTS_PER_GROUP + j
            parts.append(hid[:, j * D_EXPERT:(j + 1) * D_EXPERT] * gates[:, e:e + 1])
        hid = jnp.concatenate(parts, axis=1).astype(BF16)
        acc = acc + jnp.dot(hid, wd_ref[cols, :], preferred_element_type=F32)
    o_ref[...] = acc


def _moe(x1, h, gates, w_e_gate, w_e_up, w_e_down, tm):
    t = x1.shape[0]
    hidden = N_EXPERTS * D_EXPERT
    wg = jnp.transpose(w_e_gate, (1, 0, 2)).reshape(D_MODEL, hidden).astype(BF16)
    wu = jnp.transpose(w_e_up, (1, 0, 2)).reshape(D_MODEL, hidden).astype(BF16)
    wd = w_e_down.reshape(hidden, D_MODEL).astype(BF16)
    row = lambda n: pl.BlockSpec((tm, n), lambda i: (i, 0))
    resident = lambda shape: pl.BlockSpec(shape, lambda i: (0, 0), pipeline_mode=pl.Buffered(1))
    return pl.pallas_call(
        _moe_kernel,
        grid=(t // tm,),
        in_specs=[row(D_MODEL), row(D_MODEL), row(ROUTER_LANES),
                  resident((D_MODEL, hidden)), resident((D_MODEL, hidden)),
                  resident((hidden, D_MODEL))],
        out_specs=row(D_MODEL),
        out_shape=jax.ShapeDtypeStruct((t, D_MODEL), F32),
        compiler_params=_compiler_params(1),
        name="moe",
    )(x1, h, gates, wg, wu, wd)


def kernel(x, mix_norm_gain, w_in, b_gate, q_gain, k_gain, rel_bias, ssm_lambda_re, ssm_lambda_im, ssm_log_step, ssm_b_re, ssm_b_im, ssm_c_re, ssm_c_im, ssm_d, w_glu, b_glu, w_branch, w_out, ffn_norm_gain, w_group_router, group_bias, w_expert_router, expert_bias, w_e_gate, w_e_up, w_e_down):
    bsz, seq, _ = x.shape
    assert seq % (SSM_KBLOCK * SSM_CHUNK) == 0 and seq % (2 * CHUNK) == 0
    depth = w_in.shape[0]
    tiles = _tiles(bsz * seq)
    x2 = x.reshape(bsz * seq, D_MODEL)
    for i in range(depth):
        w_qkv = w_in[i][:, :D_QKV]
        w_u = w_in[i][:, D_QKV:D_QKV + D_SSM]
        w_gate = w_in[i][:, D_QKV + D_SSM:]
        q, k, v = _inproj(x2, mix_norm_gain[i], w_qkv, q_gain[i], k_gain[i], tiles["inproj"])
        y_attn = _attention(q, k, v, _rel_bias_table(rel_bias[i]), bsz, seq)
        u_t = _ssm_inproj(x2, mix_norm_gain[i], w_u, bsz, seq)
        tables = _ssm_tables(ssm_lambda_re[i], ssm_lambda_im[i], ssm_log_step[i],
                             ssm_b_re[i], ssm_b_im[i], ssm_c_re[i], ssm_c_im[i], ssm_d[i])
        z = _ssm_out(_ssm(u_t, tables, bsz), bsz, seq)
        x1, h, gates = _merge(x2, y_attn, z, mix_norm_gain[i], w_gate, b_gate[i], w_glu[i],
                              b_glu[i], w_branch[i], w_out[i], ffn_norm_gain[i],
                              w_group_router[i], group_bias[i], w_expert_router[i],
                              expert_bias[i], tiles["merge"])
        x2 = _moe(x1, h, gates, w_e_gate[i], w_e_up[i], w_e_down[i], tiles["moe"])
    return x2.reshape(bsz, seq, D_MODEL)
```

```python
import functools

import jax
import jax.numpy as jnp
from jax import lax
from jax.experimental import pallas as pl
from jax.experimental.pallas import tpu as pltpu

F32 = jnp.float32
BF16 = jnp.bfloat16

D_MODEL = 1024
CHUNK = 64
LEFT_CHUNKS = 8
BAND = LEFT_CHUNKS + 1
BAND_KEYS = BAND * CHUNK
PAD_KEYS = LEFT_CHUNKS * CHUNK
ATTN_HEADS = 8
HEAD_DIM = 64
D_ATTN = ATTN_HEADS * HEAD_DIM
MAX_REL = 256
REL_FUTURE = CHUNK - 1
D_SSM = D_MODEL // 2
SSM_GROUP = 16
SSM_GROUPS = D_SSM // SSM_GROUP
SSM_STATE = 64
N_BRANCH = 2
D_QKV = 3 * D_ATTN
D_IN = D_QKV + D_SSM + N_BRANCH * D_MODEL
N_GROUPS = 4
EXPERTS_PER_GROUP = 4
N_EXPERTS = N_GROUPS * EXPERTS_PER_GROUP
D_EXPERT = D_MODEL // 4
EPS = 1e-6
NEG_INF = -1e30
LOG2_E = 1.4426950408889634

LANES = 128
HEAD_PAIRS = D_ATTN // LANES
SSM_CHUNK = 16
SSM_ROW = SSM_CHUNK * SSM_GROUP
SSM_KBLOCK = 16
SSM_BGROUP = LANES // SSM_KBLOCK
ROUTER_LANES = LANES
VMEM_LIMIT = 56 * 1024 * 1024


def _tiles(n_tokens):
    return dict(inproj=512, merge=512, moe=512)


def _compiler_params(n_axes):
    return pltpu.CompilerParams(
        dimension_semantics=("arbitrary",) * n_axes,
        vmem_limit_bytes=VMEM_LIMIT)


def _full(shape):
    nd = len(shape)
    return pl.BlockSpec(shape, lambda *_: (0,) * nd)


def _rms_norm(x, gain):
    ms = jnp.mean(x * x, axis=-1, keepdims=True)
    return x * lax.rsqrt(ms + EPS) * gain


NT_DIMS = (((1,), (1,)), ((), ()))


def _inproj_kernel(x_ref, gain_ref, w_ref, qg_ref, kg_ref, ones_ref, q_ref, k_ref, v_ref):
    h = _rms_norm(x_ref[...], gain_ref[...]).astype(BF16)

    def proj(lo, n):
        return jnp.dot(h, w_ref[:, lo:lo + n], preferred_element_type=F32)

    def head_norm(t, g_ref):
        ss = jnp.dot((t * t).astype(BF16), ones_ref[...], preferred_element_type=F32)
        return t * lax.rsqrt(ss * (1.0 / HEAD_DIM) + EPS) * g_ref[...]

    q_ref[...] = head_norm(proj(0, D_ATTN), qg_ref).astype(BF16)
    k_ref[...] = head_norm(proj(D_ATTN, D_ATTN), kg_ref).astype(BF16)
    v_ref[...] = proj(2 * D_ATTN, D_ATTN).astype(BF16)


def _inproj(x2, gain, w_qkv, q_gain, k_gain, tm):
    t = x2.shape[0]
    ones = jnp.kron(jnp.eye(ATTN_HEADS, dtype=F32), jnp.ones((HEAD_DIM, HEAD_DIM), F32)).astype(BF16)
    qg = jnp.tile(q_gain, ATTN_HEADS)[None, :] * (HEAD_DIM ** -0.5 * LOG2_E)
    kg = jnp.tile(k_gain, ATTN_HEADS)[None, :]
    row = lambda n: pl.BlockSpec((tm, n), lambda i: (i, 0))
    return pl.pallas_call(
        _inproj_kernel,
        grid=(t // tm,),
        in_specs=[row(D_MODEL), _full((1, D_MODEL)), _full((D_MODEL, D_QKV)),
                  _full((1, D_ATTN)), _full((1, D_ATTN)), _full((D_ATTN, D_ATTN))],
        out_specs=[row(D_ATTN)] * 3,
        out_shape=[jax.ShapeDtypeStruct((t, D_ATTN), BF16)] * 3,
        compiler_params=_compiler_params(1),
        name="inproj",
    )(x2, gain[None, :], w_qkv.astype(BF16), qg, kg, ones)


def _attn_kernel(q_ref, k_ref, v_ref, bias_ref, o_ref, kpad, v_even, v_odd, s_a, s_b):
    seq = q_ref.shape[0]
    n_chunks = seq // CHUNK
    lane_v = lax.broadcasted_iota(jnp.int32, (seq, D_ATTN), 1)
    even_v = (lane_v % LANES) < HEAD_DIM
    zeros = jnp.zeros((PAD_KEYS, D_ATTN), BF16)
    v = v_ref[...]
    kpad[0:PAD_KEYS, :] = zeros
    v_even[0:PAD_KEYS, :] = zeros
    v_odd[0:PAD_KEYS, :] = zeros
    kpad[PAD_KEYS:, :] = k_ref[...]
    v_even[PAD_KEYS:, :] = jnp.where(even_v, v, jnp.zeros_like(v))
    v_odd[PAD_KEYS:, :] = jnp.where(even_v, jnp.zeros_like(v), v)

    lane = lax.broadcasted_iota(jnp.int32, (CHUNK, LANES), 1)
    first_head = lane < HEAD_DIM
    key_col = lax.broadcasted_iota(jnp.int32, (2 * CHUNK, BAND_KEYS), 1)

    def scores(n, s_buf):
        r0 = pl.multiple_of(n * CHUNK, CHUNK)
        for j in range(HEAD_PAIRS):
            cols = slice(j * LANES, (j + 1) * LANES)
            qp = q_ref[pl.ds(r0, CHUNK), cols]
            zq = jnp.zeros_like(qp)
            q2 = jnp.concatenate([jnp.where(first_head, qp, zq), jnp.where(first_head, zq, qp)], axis=0)
            kb = kpad[pl.ds(r0, BAND_KEYS), cols]
            s_buf[j * 2 * CHUNK:(j + 1) * 2 * CHUNK, :] = lax.dot_general(
                q2, kb, NT_DIMS, preferred_element_type=F32)

    def softmax_pv(n, s_buf, masked):
        r0 = pl.multiple_of(n * CHUNK, CHUNK)
        for j in range(HEAD_PAIRS):
            cols = slice(j * LANES, (j + 1) * LANES)
            rows = slice(j * 2 * CHUNK, (j + 1) * 2 * CHUNK)
            s = s_buf[rows, :] + bias_ref[rows, :]
            if masked:
                s = jnp.where(key_col >= (LEFT_CHUNKS - n) * CHUNK, s, NEG_INF)
            m = jnp.max(s, axis=-1, keepdims=True)
            p = jnp.exp2(s - m)
            inv_l = 1.0 / jnp.sum(p, axis=-1, keepdims=True)
            pb = p.astype(BF16)
            o0 = jnp.dot(pb[:CHUNK], v_even[pl.ds(r0, BAND_KEYS), cols], preferred_element_type=F32)
            o1 = jnp.dot(pb[CHUNK:], v_odd[pl.ds(r0, BAND_KEYS), cols], preferred_element_type=F32)
            o_ref[pl.ds(r0, CHUNK), cols] = (o0 * inv_l[:CHUNK] + o1 * inv_l[CHUNK:]).astype(BF16)

    def pair_body(i, carry, *, masked):
        n0 = 2 * i
        scores(n0 + 1, s_b)
        softmax_pv(n0, s_a, masked)
        scores(jnp.minimum(n0 + 2, n_chunks - 1), s_a)
        softmax_pv(n0 + 1, s_b, masked)
        return carry

    assert n_chunks % 2 == 0 and LEFT_CHUNKS % 2 == 0
    masked_pairs = min(LEFT_CHUNKS, n_chunks) // 2
    scores(0, s_a)
    lax.fori_loop(0, masked_pairs, functools.partial(pair_body, masked=True), 0)
    lax.fori_loop(masked_pairs, n_chunks // 2, functools.partial(pair_body, masked=False), 0)


def _attention(q, k, v, bias, bsz, seq):
    q3, k3, v3 = (a.reshape(bsz, seq, D_ATTN) for a in (q, k, v))
    seq_spec = pl.BlockSpec((None, seq, D_ATTN), lambda b: (b, 0, 0))
    padded = pltpu.VMEM((PAD_KEYS + seq, D_ATTN), BF16)
    out = pl.pallas_call(
        _attn_kernel,
        grid=(bsz,),
        in_specs=[seq_spec, seq_spec, seq_spec, _full((ATTN_HEADS * CHUNK, BAND_KEYS))],
        out_specs=seq_spec,
        out_shape=jax.ShapeDtypeStruct((bsz, seq, D_ATTN), BF16),
        scratch_shapes=[padded, padded, padded,
                        pltpu.VMEM((ATTN_HEADS * CHUNK, BAND_KEYS), F32),
                        pltpu.VMEM((ATTN_HEADS * CHUNK, BAND_KEYS), F32)],
        compiler_params=_compiler_params(1),
        name="attention",
    )(q3, k3, v3, bias)
    return out.reshape(bsz * seq, D_ATTN)


def _rel_bias_table(rel_bias):
    n_rel = REL_FUTURE + MAX_REL + 1
    far = jnp.broadcast_to(rel_bias[:, n_rel - 1:], (ATTN_HEADS, PAD_KEYS - MAX_REL + CHUNK))
    ext = jnp.concatenate([far, rel_bias[:, n_rel - 2::-1]], axis=1) * LOG2_E
    rows = [ext[:, CHUNK - 1 - q:CHUNK - 1 - q + BAND_KEYS] for q in range(CHUNK)]
    return jnp.stack(rows, axis=1).reshape(ATTN_HEADS * CHUNK, BAND_KEYS)


SSM_BLOCK_LEN = SSM_KBLOCK * SSM_CHUNK
SSM_BLOCK_TOK = SSM_BGROUP * SSM_KBLOCK


def _token_block_spec(width):
    return pl.BlockSpec((SSM_BGROUP, SSM_BLOCK_LEN, width), lambda kb, bg: (bg, kb, 0))


def _transposed_spec(n_bgroups):
    return pl.BlockSpec((SSM_GROUPS, SSM_ROW, SSM_BLOCK_TOK),
                        lambda kb, bg: (0, 0, kb * n_bgroups + bg))


def _token_grid(bsz, seq):
    assert bsz % SSM_BGROUP == 0 and seq % SSM_BLOCK_LEN == 0
    return (seq // SSM_BLOCK_LEN, bsz // SSM_BGROUP)


def _ssm_inproj_kernel(x_ref, gain_ref, wu_ref, ut_ref, u_scr):
    n_rows = SSM_BGROUP * SSM_BLOCK_LEN
    x = x_ref[...].reshape(n_rows, D_MODEL)
    h = _rms_norm(x, gain_ref[...]).astype(BF16)
    u = jnp.dot(h, wu_ref[...], preferred_element_type=F32)
    for q in range(D_SSM // LANES):
        u_scr[q] = u[:, q * LANES:(q + 1) * LANES]
    groups_per_tile = LANES // SSM_GROUP
    for t in range(SSM_CHUNK):
        for q in range(D_SSM // LANES):
            ut = u_scr[q, pl.ds(t, SSM_BLOCK_TOK, stride=SSM_CHUNK), :].T.astype(BF16)
            for gl in range(groups_per_tile):
                ut_ref[q * groups_per_tile + gl, t * SSM_GROUP:(t + 1) * SSM_GROUP, :] = (
                    ut[gl * SSM_GROUP:(gl + 1) * SSM_GROUP, :])


def _ssm_inproj(x2, gain, w_u, bsz, seq):
    grid = _token_grid(bsz, seq)
    return pl.pallas_call(
        _ssm_inproj_kernel,
        grid=grid,
        in_specs=[_token_block_spec(D_MODEL), _full((1, D_MODEL)), _full((D_MODEL, D_SSM))],
        out_specs=_transposed_spec(grid[1]),
        out_shape=jax.ShapeDtypeStruct((SSM_GROUPS, SSM_ROW, bsz * seq // SSM_CHUNK), BF16),
        scratch_shapes=[pltpu.VMEM((D_SSM // LANES, SSM_BGROUP * SSM_BLOCK_LEN, LANES), F32)],
        compiler_params=_compiler_params(2),
        name="ssm_inproj",
    )(x2.reshape(bsz, seq, D_MODEL), gain[None, :], w_u.astype(BF16))


def _ssm_out_kernel(yt_ref, z_ref, y_scr):
    groups_per_tile = LANES // SSM_GROUP
    for t in range(SSM_CHUNK):
        for q in range(D_SSM // LANES):
            yt = jnp.concatenate(
                [yt_ref[q * groups_per_tile + gl, t * SSM_GROUP:(t + 1) * SSM_GROUP, :]
                 for gl in range(groups_per_tile)], axis=0)
            y_scr[q, pl.ds(t, SSM_BLOCK_TOK, stride=SSM_CHUNK), :] = yt.T
    for q in range(D_SSM // LANES):
        z = jax.nn.gelu(y_scr[q])
        z_ref[:, :, q * LANES:(q + 1) * LANES] = z.astype(BF16).reshape(
            SSM_BGROUP, SSM_BLOCK_LEN, LANES)


def _ssm_out(y_t, bsz, seq):
    grid = _token_grid(bsz, seq)
    z = pl.pallas_call(
        _ssm_out_kernel,
        grid=grid,
        in_specs=[_transposed_spec(grid[1])],
        out_specs=_token_block_spec(D_SSM),
        out_shape=jax.ShapeDtypeStruct((bsz, seq, D_SSM), BF16),
        scratch_shapes=[pltpu.VMEM((D_SSM // LANES, SSM_BGROUP * SSM_BLOCK_LEN, LANES), F32)],
        compiler_params=_compiler_params(2),
        name="ssm_out",
    )(y_t)
    return z.reshape(bsz * seq, D_SSM)


def _ssm_tables(lambda_re, lambda_im, log_step, b_re, b_im, c_re, c_im, d_skip):
    step = jnp.exp(log_step)[:, None]
    mag = jnp.exp(lambda_re * step)
    ang = lambda_im * step
    a_re = mag * jnp.cos(ang)
    a_im = mag * jnp.sin(ang)
    num_re = a_re - 1.0
    num_im = a_im
    den = lambda_re * lambda_re + lambda_im * lambda_im
    f_re = (num_re * lambda_re + num_im * lambda_im) / den
    f_im = (num_im * lambda_re - num_re * lambda_im) / den
    bb_re = f_re[..., None] * b_re - f_im[..., None] * b_im
    bb_im = f_re[..., None] * b_im + f_im[..., None] * b_re
    log_mag = lambda_re * step

    def powers(tau):
        tau = jnp.asarray(tau, F32)[:, None, None]
        pmag = jnp.exp(tau * log_mag[None])
        return pmag * jnp.cos(tau * ang[None]), pmag * jnp.sin(tau * ang[None])

    steps = jnp.arange(SSM_CHUNK)
    up_re, up_im = powers(steps + 1)
    cp_re = c_re[None] * up_re[:, :, None, :] - c_im[None] * up_im[:, :, None, :]
    cp_im = c_re[None] * up_im[:, :, None, :] + c_im[None] * up_re[:, :, None, :]
    readout = jnp.concatenate([cp_re, -cp_im], axis=-1)
    readout = jnp.transpose(readout, (1, 0, 2, 3)).reshape(SSM_GROUPS, SSM_ROW, 2 * SSM_STATE)

    def inject(tau):
        p_re, p_im = powers(tau)
        bt_re = jnp.transpose(bb_re, (0, 2, 1))[None]
        bt_im = jnp.transpose(bb_im, (0, 2, 1))[None]
        i_re = p_re[:, :, None, :] * bt_re - p_im[:, :, None, :] * bt_im
        i_im = p_re[:, :, None, :] * bt_im + p_im[:, :, None, :] * bt_re
        both = jnp.concatenate([i_re, i_im], axis=-1)
        return jnp.transpose(both, (1, 3, 0, 2)).reshape(SSM_GROUPS, 2 * SSM_STATE, SSM_ROW)

    inject_back = inject(-(steps + 1.0))
    inject_end = inject(SSM_CHUNK - 1.0 - steps).astype(BF16)
    d_row = jnp.tile(d_skip.reshape(SSM_GROUPS, 1, SSM_GROUP), (1, 1, SSM_CHUNK))
    (ar,), (ai,) = powers([float(SSM_CHUNK)])
    a_chunk = jnp.stack([jnp.concatenate([ar, ar], -1), jnp.concatenate([-ai, ai], -1),
                         jnp.concatenate([ai, -ai], -1)], axis=1)
    return readout, inject_back, inject_end, d_row, a_chunk


def _ssm_kernel(ut_ref, ro_ref, ib_ref, ie_ref, d_ref, a_ref, yt_ref, p_scr, q_scr, s_scr, *, bsz):
    ut = ut_ref[...]
    n_tok = ut.shape[1]
    r_hi, r_lo = _split_bf16(ro_ref[...])
    i_hi, i_lo = _split_bf16(ib_ref[...])
    kern = (jnp.dot(r_hi, i_hi, preferred_element_type=F32)
            + jnp.dot(r_hi, i_lo, preferred_element_type=F32)
            + jnp.dot(r_lo, i_hi, preferred_element_type=F32))
    row = lax.broadcasted_iota(jnp.int32, kern.shape, 0)
    col = lax.broadcasted_iota(jnp.int32, kern.shape, 1)
    group_shift = SSM_GROUP.bit_length() - 1
    causal = jnp.right_shift(row, group_shift) >= jnp.right_shift(col, group_shift)
    kern = jnp.where(causal, kern, 0.0) + jnp.where(row == col, d_ref[...], 0.0)
    wt = jnp.dot(ie_ref[...], ut, preferred_element_type=F32)
    w = wt.T
    p_scr[...] = w
    q_scr[...] = pltpu.roll(w, SSM_STATE, 1)
    a1 = a_ref[0:1, :]
    a2 = a_ref[1:2, :]
    a3 = a_ref[2:3, :]

    def block(kb, carry):
        cp, cq = carry
        base = pl.multiple_of(kb * (bsz * SSM_KBLOCK), bsz * SSM_KBLOCK)
        for k in range(SSM_KBLOCK):
            rows = pl.ds(base + k, bsz, stride=SSM_KBLOCK)
            s_scr[rows, :] = cp
            cp, cq = (a1 * cp + a2 * cq + p_scr[rows, :],
                      a1 * cq + a3 * cp + q_scr[rows, :])
        return cp, cq

    zero = jnp.zeros((bsz, 2 * SSM_STATE), F32)
    lax.fori_loop(0, n_tok // (bsz * SSM_KBLOCK), block, (zero, zero))
    yt = jnp.dot(kern.astype(BF16), ut, preferred_element_type=F32)
    yt = yt + lax.dot_general(r_hi, s_scr[...].astype(BF16), NT_DIMS, preferred_element_type=F32)
    yt_ref[...] = yt


def _ssm(ut, tables, bsz):
    readout, inject_back, inject_end, d_row, a_chunk = tables
    n_tok = ut.shape[2]
    per_group = lambda a, b: pl.BlockSpec((None, a, b), lambda g: (g, 0, 0))
    state = pltpu.VMEM((n_tok, 2 * SSM_STATE), F32)
    return pl.pallas_call(
        functools.partial(_ssm_kernel, bsz=bsz),
        grid=(SSM_GROUPS,),
        in_specs=[per_group(SSM_ROW, n_tok), per_group(SSM_ROW, 2 * SSM_STATE),
                  per_group(2 * SSM_STATE, SSM_ROW), per_group(2 * SSM_STATE, SSM_ROW),
                  per_group(1, SSM_ROW), per_group(3, 2 * SSM_STATE)],
        out_specs=per_group(SSM_ROW, n_tok),
        out_shape=jax.ShapeDtypeStruct((SSM_GROUPS, SSM_ROW, n_tok), F32),
        scratch_shapes=[state, state, state],
        compiler_params=_compiler_params(1),
        name="ssm",
    )(ut, readout, inject_back, inject_end, d_row, a_chunk)


def _split_bf16(a):
    hi = a.astype(BF16)
    lo = (a - hi.astype(F32)).astype(BF16)
    return hi, lo


def _route(logits):
    lane = lax.broadcasted_iota(jnp.int32, logits.shape, 1)
    lane_f = lane.astype(F32)
    big = float(ROUTER_LANES)

    def first_argmax(vals, vmax):
        return jnp.min(jnp.where(vals == vmax, lane_f, big), axis=-1, keepdims=True)

    g_mask = (lane >= N_EXPERTS) & (lane < N_EXPERTS + N_GROUPS)
    g_log = jnp.where(g_mask, logits, NEG_INF)
    g_max = jnp.max(g_log, axis=-1, keepdims=True)
    g_sum = jnp.sum(jnp.where(g_mask, jnp.exp(g_log - g_max), 0.0), axis=-1, keepdims=True)
    g_prob = 1.0 / g_sum
    g_idx = first_argmax(g_log, g_max) - float(N_EXPERTS)
    e_lo = g_idx * float(EXPERTS_PER_GROUP)
    e_mask = (lane_f >= e_lo) & (lane_f < e_lo + float(EXPERTS_PER_GROUP))
    e_log = jnp.where(e_mask, logits, NEG_INF)
    e_max1 = jnp.max(e_log, axis=-1, keepdims=True)
    idx1 = first_argmax(e_log, e_max1)
    e_log2 = jnp.where(lane_f == idx1, NEG_INF, e_log)
    e_max2 = jnp.max(e_log2, axis=-1, keepdims=True)
    idx2 = first_argmax(e_log2, e_max2)
    r = jnp.exp(e_max2 - e_max1)
    w1 = g_prob / (1.0 + r)
    w2 = g_prob * r / (1.0 + r)
    return jnp.where(lane_f == idx1, w1, 0.0) + jnp.where(lane_f == idx2, w2, 0.0)


def _merge_kernel(x_ref, ya_ref, z_ref, gain1_ref, wgate_ref, bgate_ref, wglu_ref, bglu_ref,
                  wba_ref, wbs_ref, wout_ref, gain2_ref, rhi_ref, rlo_ref, rbias_ref,
                  x1_ref, h_ref, gates_ref):
    x = x_ref[...]
    hn = _rms_norm(x, gain1_ref[...]).astype(BF16)
    z = z_ref[...]
    glu = jnp.dot(z, wglu_ref[...], preferred_element_type=F32) + bglu_ref[...]
    y_ssm = (z.astype(F32) * jax.nn.sigmoid(glu)).astype(BF16)
    merged = None
    for s, (y_b, wb_ref) in enumerate(((ya_ref[...], wba_ref), (y_ssm, wbs_ref))):
        cols = slice(s * D_MODEL, (s + 1) * D_MODEL)
        gate = jax.nn.sigmoid(jnp.dot(hn, wgate_ref[:, cols], preferred_element_type=F32)
                              + bgate_ref[:, cols])
        term = gate * jnp.dot(y_b, wb_ref[...], preferred_element_type=F32)
        merged = term if merged is None else merged + term
    x1 = x + jnp.dot(merged.astype(BF16), wout_ref[...], preferred_element_type=F32)
    x1_ref[...] = x1

    h = _rms_norm(x1, gain2_ref[...])
    h_ref[...] = h.astype(BF16)
    h_hi, h_lo = _split_bf16(h)
    logits = (jnp.dot(h_hi, rhi_ref[...], preferred_element_type=F32)
              + jnp.dot(h_hi, rlo_ref[...], preferred_element_type=F32)
              + jnp.dot(h_lo, rhi_ref[...], preferred_element_type=F32)
              + rbias_ref[...])
    gates_ref[...] = _route(logits)


def _merge(x2, y_attn, z, gain1, w_gate, b_gate, w_glu, b_glu, w_branch, w_out, ffn_gain,
           w_group_router, group_bias, w_expert_router, expert_bias, tm):
    t = x2.shape[0]
    pad = ROUTER_LANES - N_EXPERTS - N_GROUPS
    w_r = jnp.concatenate([w_expert_router, w_group_router,
                           jnp.zeros((D_MODEL, pad), F32)], axis=1)
    r_hi, r_lo = _split_bf16(w_r)
    r_bias = jnp.concatenate([expert_bias, group_bias, jnp.zeros((pad,), F32)])[None, :]
    row = lambda n: pl.BlockSpec((tm, n), lambda i: (i, 0))
    resident = lambda shape: pl.BlockSpec(shape, lambda i: (0, 0), pipeline_mode=pl.Buffered(1))
    return pl.pallas_call(
        _merge_kernel,
        grid=(t // tm,),
        in_specs=[row(D_MODEL), row(D_ATTN), row(D_SSM),
                  resident((1, D_MODEL)), resident((D_MODEL, N_BRANCH * D_MODEL)),
                  resident((1, N_BRANCH * D_MODEL)),
                  resident((D_SSM, D_SSM)), resident((1, D_SSM)),
                  resident((D_ATTN, D_MODEL)), resident((D_SSM, D_MODEL)),
                  resident((D_MODEL, D_MODEL)), resident((1, D_MODEL)),
                  resident((D_MODEL, ROUTER_LANES)), resident((D_MODEL, ROUTER_LANES)),
                  resident((1, ROUTER_LANES))],
        out_specs=[row(D_MODEL), row(D_MODEL), row(ROUTER_LANES)],
        out_shape=[jax.ShapeDtypeStruct((t, D_MODEL), F32),
                   jax.ShapeDtypeStruct((t, D_MODEL), BF16),
                   jax.ShapeDtypeStruct((t, ROUTER_LANES), F32)],
        compiler_params=_compiler_params(1),
        name="merge",
    )(x2, y_attn, z, gain1[None, :], w_gate.astype(BF16), b_gate[None, :],
      w_glu.astype(BF16), b_glu[None, :],
      w_branch[:D_ATTN].astype(BF16), w_branch[D_ATTN:].astype(BF16),
      w_out.astype(BF16), ffn_gain[None, :], r_hi, r_lo, r_bias)


def _moe_kernel(x1_ref, h_ref, gates_ref, wg_ref, wu_ref, wd_ref, o_ref):
    h = h_ref[...]
    gates = gates_ref[...]
    acc = x1_ref[...]
    width = EXPERTS_PER_GROUP * D_EXPERT
    for g in range(N_GROUPS):
        cols = slice(g * width, (g + 1) * width)
        a = jnp.dot(h, wg_ref[:, cols], preferred_element_type=F32)
        u = jnp.dot(h, wu_ref[:, cols], preferred_element_type=F32)
        hid = jax.nn.silu(a) * u
        parts = []
        for j in range(EXPERTS_PER_GROUP):
            e = g * EXPERTS_PER_GROUP + j
            parts.append(hid[:, j * D_EXPERT:(j + 1) * D_EXPERT] * gates[:, e:e + 1])
        hid = jnp.concatenate(parts, axis=1).astype(BF16)
        acc = acc + jnp.dot(hid, wd_ref[cols, :], preferred_element_type=F32)
    o_ref[...] = acc


def _moe(x1, h, gates, w_e_gate, w_e_up, w_e_down, tm):
    t = x1.shape[0]
    hidden = N_EXPERTS * D_EXPERT
    wg = jnp.transpose(w_e_gate, (1, 0, 2)).reshape(D_MODEL, hidden).astype(BF16)
    wu = jnp.transpose(w_e_up, (1, 0, 2)).reshape(D_MODEL, hidden).astype(BF16)
    wd = w_e_down.reshape(hidden, D_MODEL).astype(BF16)
    row = lambda n: pl.BlockSpec((tm, n), lambda i: (i, 0))
    resident = lambda shape: pl.BlockSpec(shape, lambda i: (0, 0), pipeline_mode=pl.Buffered(1))
    return pl.pallas_call(
        _moe_kernel,
        grid=(t // tm,),
        in_specs=[row(D_MODEL), row(D_MODEL), row(ROUTER_LANES),
                  resident((D_MODEL, hidden)), resident((D_MODEL, hidden)),
                  resident((hidden, D_MODEL))],
        out_specs=row(D_MODEL),
        out_shape=jax.ShapeDtypeStruct((t, D_MODEL), F32),
        compiler_params=_compiler_params(1),
        name="moe",
    )(x1, h, gates, wg, wu, wd)


def kernel(x, mix_norm_gain, w_in, b_gate, q_gain, k_gain, rel_bias, ssm_lambda_re, ssm_lambda_im, ssm_log_step, ssm_b_re, ssm_b_im, ssm_c_re, ssm_c_im, ssm_d, w_glu, b_glu, w_branch, w_out, ffn_norm_gain, w_group_router, group_bias, w_expert_router, expert_bias, w_e_gate, w_e_up, w_e_down):
    bsz, seq, _ = x.shape
    assert seq % (SSM_KBLOCK * SSM_CHUNK) == 0 and seq % (2 * CHUNK) == 0
    depth = w_in.shape[0]
    tiles = _tiles(bsz * seq)
    x2 = x.reshape(bsz * seq, D_MODEL)
    for i in range(depth):
        w_qkv = w_in[i][:, :D_QKV]
        w_u = w_in[i][:, D_QKV:D_QKV + D_SSM]
        w_gate = w_in[i][:, D_QKV + D_SSM:]
        q, k, v = _inproj(x2, mix_norm_gain[i], w_qkv, q_gain[i], k_gain[i], tiles["inproj"])
        y_attn = _attention(q, k, v, _rel_bias_table(rel_bias[i]), bsz, seq)
        u_t = _ssm_inproj(x2, mix_norm_gain[i], w_u, bsz, seq)
        tables = _ssm_tables(ssm_lambda_re[i], ssm_lambda_im[i], ssm_log_step[i],
                             ssm_b_re[i], ssm_b_im[i], ssm_c_re[i], ssm_c_im[i], ssm_d[i])
        z = _ssm_out(_ssm(u_t, tables, bsz), bsz, seq)
        x1, h, gates = _merge(x2, y_attn, z, mix_norm_gain[i], w_gate, b_gate[i], w_glu[i],
                              b_glu[i], w_branch[i], w_out[i], ffn_norm_gain[i],
                              w_group_router[i], group_bias[i], w_expert_router[i],
                              expert_bias[i], tiles["merge"])
        x2 = _moe(x1, h, gates, w_e_gate[i], w_e_up[i], w_e_down[i], tiles["moe"])
    return x2.reshape(bsz, seq, D_MODEL)
```

```python
import functools

import jax
import jax.numpy as jnp
from jax import lax
from jax.experimental import pallas as pl
from jax.experimental.pallas import tpu as pltpu

F32 = jnp.float32
BF16 = jnp.bfloat16

D_MODEL = 1024
CHUNK = 64
LEFT_CHUNKS = 8
BAND = LEFT_CHUNKS + 1
BAND_KEYS = BAND * CHUNK
PAD_KEYS = LEFT_CHUNKS * CHUNK
ATTN_HEADS = 8
HEAD_DIM = 64
D_ATTN = ATTN_HEADS * HEAD_DIM
MAX_REL = 256
REL_FUTURE = CHUNK - 1
D_SSM = D_MODEL // 2
SSM_GROUP = 16
SSM_GROUPS = D_SSM // SSM_GROUP
SSM_STATE = 64
N_BRANCH = 2
D_QKV = 3 * D_ATTN
D_IN = D_QKV + D_SSM + N_BRANCH * D_MODEL
N_GROUPS = 4
EXPERTS_PER_GROUP = 4
N_EXPERTS = N_GROUPS * EXPERTS_PER_GROUP
D_EXPERT = D_MODEL // 4
EPS = 1e-6
NEG_INF = -1e30
LOG2_E = 1.4426950408889634

LANES = 128
HEAD_PAIRS = D_ATTN // LANES
SSM_CHUNK = 16
SSM_ROW = SSM_CHUNK * SSM_GROUP
SSM_KBLOCK = 16
SSM_BGROUP = LANES // SSM_KBLOCK
ROUTER_LANES = LANES
D_ROUTED = D_MODEL + ROUTER_LANES
CLASS_LANE = 32
RANK_LANE = 33
MOE_TILE = 256
_PAIRS = [(a, b) for a in range(EXPERTS_PER_GROUP) for b in range(a + 1, EXPERTS_PER_GROUP)]
PAIRS_PER_GROUP = len(_PAIRS)
N_CLASSES = N_GROUPS * PAIRS_PER_GROUP
_CLASS_EXPERTS = [(g * EXPERTS_PER_GROUP + a, g * EXPERTS_PER_GROUP + b)
                  for g in range(N_GROUPS) for a, b in _PAIRS]
VMEM_LIMIT = 56 * 1024 * 1024


def _tiles(n_tokens):
    return dict(inproj=512, merge=512, moe_rows=512)


def _compiler_params(n_axes):
    return pltpu.CompilerParams(
        dimension_semantics=("arbitrary",) * n_axes,
        vmem_limit_bytes=VMEM_LIMIT)


def _full(shape):
    nd = len(shape)
    return pl.BlockSpec(shape, lambda *_: (0,) * nd)


def _rms_norm(x, gain):
    ms = jnp.mean(x * x, axis=-1, keepdims=True)
    return x * lax.rsqrt(ms + EPS) * gain


NT_DIMS = (((1,), (1,)), ((), ()))


def _inproj_kernel(x_ref, gain_ref, w_ref, qg_ref, kg_ref, ones_ref, q_ref, k_ref, v_ref):
    h = _rms_norm(x_ref[...], gain_ref[...]).astype(BF16)

    def proj(lo, n):
        return jnp.dot(h, w_ref[:, lo:lo + n], preferred_element_type=F32)

    def head_norm(t, g_ref):
        ss = jnp.dot((t * t).astype(BF16), ones_ref[...], preferred_element_type=F32)
        return t * lax.rsqrt(ss * (1.0 / HEAD_DIM) + EPS) * g_ref[...]

    q_ref[...] = head_norm(proj(0, D_ATTN), qg_ref).astype(BF16)
    k_ref[...] = head_norm(proj(D_ATTN, D_ATTN), kg_ref).astype(BF16)
    v_ref[...] = proj(2 * D_ATTN, D_ATTN).astype(BF16)


def _inproj(x2, gain, w_qkv, q_gain, k_gain, tm):
    t = x2.shape[0]
    ones = jnp.kron(jnp.eye(ATTN_HEADS, dtype=F32), jnp.ones((HEAD_DIM, HEAD_DIM), F32)).astype(BF16)
    qg = jnp.tile(q_gain, ATTN_HEADS)[None, :] * (HEAD_DIM ** -0.5 * LOG2_E)
    kg = jnp.tile(k_gain, ATTN_HEADS)[None, :]
    row = lambda n: pl.BlockSpec((tm, n), lambda i: (i, 0))
    return pl.pallas_call(
        _inproj_kernel,
        grid=(t // tm,),
        in_specs=[row(D_MODEL), _full((1, D_MODEL)), _full((D_MODEL, D_QKV)),
                  _full((1, D_ATTN)), _full((1, D_ATTN)), _full((D_ATTN, D_ATTN))],
        out_specs=[row(D_ATTN)] * 3,
        out_shape=[jax.ShapeDtypeStruct((t, D_ATTN), BF16)] * 3,
        compiler_params=_compiler_params(1),
        name="inproj",
    )(x2, gain[None, :], w_qkv.astype(BF16), qg, kg, ones)


def _attn_kernel(q_ref, k_ref, v_ref, bias_ref, o_ref, kpad, v_even, v_odd, s_a, s_b):
    seq = q_ref.shape[0]
    n_chunks = seq // CHUNK
    lane_v = lax.broadcasted_iota(jnp.int32, (seq, D_ATTN), 1)
    even_v = (lane_v % LANES) < HEAD_DIM
    zeros = jnp.zeros((PAD_KEYS, D_ATTN), BF16)
    v = v_ref[...]
    kpad[0:PAD_KEYS, :] = zeros
    v_even[0:PAD_KEYS, :] = zeros
    v_odd[0:PAD_KEYS, :] = zeros
    kpad[PAD_KEYS:, :] = k_ref[...]
    v_even[PAD_KEYS:, :] = jnp.where(even_v, v, jnp.zeros_like(v))
    v_odd[PAD_KEYS:, :] = jnp.where(even_v, jnp.zeros_like(v), v)

    lane = lax.broadcasted_iota(jnp.int32, (CHUNK, LANES), 1)
    first_head = lane < HEAD_DIM
    key_col = lax.broadcasted_iota(jnp.int32, (2 * CHUNK, BAND_KEYS), 1)

    def scores(n, s_buf):
        r0 = pl.multiple_of(n * CHUNK, CHUNK)
        for j in range(HEAD_PAIRS):
            cols = slice(j * LANES, (j + 1) * LANES)
            qp = q_ref[pl.ds(r0, CHUNK), cols]
            zq = jnp.zeros_like(qp)
            q2 = jnp.concatenate([jnp.where(first_head, qp, zq), jnp.where(first_head, zq, qp)], axis=0)
            kb = kpad[pl.ds(r0, BAND_KEYS), cols]
            s_buf[j * 2 * CHUNK:(j + 1) * 2 * CHUNK, :] = lax.dot_general(
                q2, kb, NT_DIMS, preferred_element_type=F32)

    def softmax_pv(n, s_buf, masked):
        r0 = pl.multiple_of(n * CHUNK, CHUNK)
        for j in range(HEAD_PAIRS):
            cols = slice(j * LANES, (j + 1) * LANES)
            rows = slice(j * 2 * CHUNK, (j + 1) * 2 * CHUNK)
            s = s_buf[rows, :] + bias_ref[rows, :]
            if masked:
                s = jnp.where(key_col >= (LEFT_CHUNKS - n) * CHUNK, s, NEG_INF)
            m = jnp.max(s, axis=-1, keepdims=True)
            p = jnp.exp2(s - m)
            inv_l = 1.0 / jnp.sum(p, axis=-1, keepdims=True)
            pb = p.astype(BF16)
            o0 = jnp.dot(pb[:CHUNK], v_even[pl.ds(r0, BAND_KEYS), cols], preferred_element_type=F32)
            o1 = jnp.dot(pb[CHUNK:], v_odd[pl.ds(r0, BAND_KEYS), cols], preferred_element_type=F32)
            o_ref[pl.ds(r0, CHUNK), cols] = (o0 * inv_l[:CHUNK] + o1 * inv_l[CHUNK:]).astype(BF16)

    def pair_body(i, carry, *, masked):
        n0 = 2 * i
        scores(n0 + 1, s_b)
        softmax_pv(n0, s_a, masked)
        scores(jnp.minimum(n0 + 2, n_chunks - 1), s_a)
        softmax_pv(n0 + 1, s_b, masked)
        return carry

    assert n_chunks % 2 == 0 and LEFT_CHUNKS % 2 == 0
    masked_pairs = min(LEFT_CHUNKS, n_chunks) // 2
    scores(0, s_a)
    lax.fori_loop(0, masked_pairs, functools.partial(pair_body, masked=True), 0)
    lax.fori_loop(masked_pairs, n_chunks // 2, functools.partial(pair_body, masked=False), 0)


def _attention(q, k, v, bias, bsz, seq):
    q3, k3, v3 = (a.reshape(bsz, seq, D_ATTN) for a in (q, k, v))
    seq_spec = pl.BlockSpec((None, seq, D_ATTN), lambda b: (b, 0, 0))
    padded = pltpu.VMEM((PAD_KEYS + seq, D_ATTN), BF16)
    out = pl.pallas_call(
        _attn_kernel,
        grid=(bsz,),
        in_specs=[seq_spec, seq_spec, seq_spec, _full((ATTN_HEADS * CHUNK, BAND_KEYS))],
        out_specs=seq_spec,
        out_shape=jax.ShapeDtypeStruct((bsz, seq, D_ATTN), BF16),
        scratch_shapes=[padded, padded, padded,
                        pltpu.VMEM((ATTN_HEADS * CHUNK, BAND_KEYS), F32),
                        pltpu.VMEM((ATTN_HEADS * CHUNK, BAND_KEYS), F32)],
        compiler_params=_compiler_params(1),
        name="attention",
    )(q3, k3, v3, bias)
    return out.reshape(bsz * seq, D_ATTN)


def _rel_bias_table(rel_bias):
    n_rel = REL_FUTURE + MAX_REL + 1
    far = jnp.broadcast_to(rel_bias[:, n_rel - 1:], (ATTN_HEADS, PAD_KEYS - MAX_REL + CHUNK))
    ext = jnp.concatenate([far, rel_bias[:, n_rel - 2::-1]], axis=1) * LOG2_E
    rows = [ext[:, CHUNK - 1 - q:CHUNK - 1 - q + BAND_KEYS] for q in range(CHUNK)]
    return jnp.stack(rows, axis=1).reshape(ATTN_HEADS * CHUNK, BAND_KEYS)


SSM_BLOCK_LEN = SSM_KBLOCK * SSM_CHUNK
SSM_BLOCK_TOK = SSM_BGROUP * SSM_KBLOCK


def _token_block_spec(width):
    return pl.BlockSpec((SSM_BGROUP, SSM_BLOCK_LEN, width), lambda kb, bg: (bg, kb, 0))


def _transposed_spec(n_bgroups):
    return pl.BlockSpec((SSM_GROUPS, SSM_ROW, SSM_BLOCK_TOK),
                        lambda kb, bg: (0, 0, kb * n_bgroups + bg))


def _token_grid(bsz, seq):
    assert bsz % SSM_BGROUP == 0 and seq % SSM_BLOCK_LEN == 0
    return (seq // SSM_BLOCK_LEN, bsz // SSM_BGROUP)


def _ssm_inproj_kernel(x_ref, gain_ref, wu_ref, ut_ref, u_scr):
    n_rows = SSM_BGROUP * SSM_BLOCK_LEN
    x = x_ref[...].reshape(n_rows, D_MODEL)
    h = _rms_norm(x, gain_ref[...]).astype(BF16)
    u = jnp.dot(h, wu_ref[...], preferred_element_type=F32)
    for q in range(D_SSM // LANES):
        u_scr[q] = u[:, q * LANES:(q + 1) * LANES]
    groups_per_tile = LANES // SSM_GROUP
    for t in range(SSM_CHUNK):
        for q in range(D_SSM // LANES):
            ut = u_scr[q, pl.ds(t, SSM_BLOCK_TOK, stride=SSM_CHUNK), :].T.astype(BF16)
            for gl in range(groups_per_tile):
                ut_ref[q * groups_per_tile + gl, t * SSM_GROUP:(t + 1) * SSM_GROUP, :] = (
                    ut[gl * SSM_GROUP:(gl + 1) * SSM_GROUP, :])


def _ssm_inproj(x2, gain, w_u, bsz, seq):
    grid = _token_grid(bsz, seq)
    return pl.pallas_call(
        _ssm_inproj_kernel,
        grid=grid,
        in_specs=[_token_block_spec(D_MODEL), _full((1, D_MODEL)), _full((D_MODEL, D_SSM))],
        out_specs=_transposed_spec(grid[1]),
        out_shape=jax.ShapeDtypeStruct((SSM_GROUPS, SSM_ROW, bsz * seq // SSM_CHUNK), BF16),
        scratch_shapes=[pltpu.VMEM((D_SSM // LANES, SSM_BGROUP * SSM_BLOCK_LEN, LANES), F32)],
        compiler_params=_compiler_params(2),
        name="ssm_inproj",
    )(x2.reshape(bsz, seq, D_MODEL), gain[None, :], w_u.astype(BF16))


def _ssm_out_kernel(yt_ref, z_ref, y_scr):
    groups_per_tile = LANES // SSM_GROUP
    for t in range(SSM_CHUNK):
        for q in range(D_SSM // LANES):
            yt = jnp.concatenate(
                [yt_ref[q * groups_per_tile + gl, t * SSM_GROUP:(t + 1) * SSM_GROUP, :]
                 for gl in range(groups_per_tile)], axis=0)
            y_scr[q, pl.ds(t, SSM_BLOCK_TOK, stride=SSM_CHUNK), :] = yt.T
    for q in range(D_SSM // LANES):
        z = jax.nn.gelu(y_scr[q])
        z_ref[:, :, q * LANES:(q + 1) * LANES] = z.astype(BF16).reshape(
            SSM_BGROUP, SSM_BLOCK_LEN, LANES)


def _ssm_out(y_t, bsz, seq):
    grid = _token_grid(bsz, seq)
    z = pl.pallas_call(
        _ssm_out_kernel,
        grid=grid,
        in_specs=[_transposed_spec(grid[1])],
        out_specs=_token_block_spec(D_SSM),
        out_shape=jax.ShapeDtypeStruct((bsz, seq, D_SSM), BF16),
        scratch_shapes=[pltpu.VMEM((D_SSM // LANES, SSM_BGROUP * SSM_BLOCK_LEN, LANES), F32)],
        compiler_params=_compiler_params(2),
        name="ssm_out",
    )(y_t)
    return z.reshape(bsz * seq, D_SSM)


def _ssm_tables(lambda_re, lambda_im, log_step, b_re, b_im, c_re, c_im, d_skip):
    step = jnp.exp(log_step)[:, None]
    mag = jnp.exp(lambda_re * step)
    ang = lambda_im * step
    a_re = mag * jnp.cos(ang)
    a_im = mag * jnp.sin(ang)
    num_re = a_re - 1.0
    num_im = a_im
    den = lambda_re * lambda_re + lambda_im * lambda_im
    f_re = (num_re * lambda_re + num_im * lambda_im) / den
    f_im = (num_im * lambda_re - num_re * lambda_im) / den
    bb_re = f_re[..., None] * b_re - f_im[..., None] * b_im
    bb_im = f_re[..., None] * b_im + f_im[..., None] * b_re
    log_mag = lambda_re * step

    def powers(tau):
        tau = jnp.asarray(tau, F32)[:, None, None]
        pmag = jnp.exp(tau * log_mag[None])
        return pmag * jnp.cos(tau * ang[None]), pmag * jnp.sin(tau * ang[None])

    steps = jnp.arange(SSM_CHUNK)
    up_re, up_im = powers(steps + 1)
    cp_re = c_re[None] * up_re[:, :, None, :] - c_im[None] * up_im[:, :, None, :]
    cp_im = c_re[None] * up_im[:, :, None, :] + c_im[None] * up_re[:, :, None, :]
    readout = jnp.concatenate([cp_re, -cp_im], axis=-1)
    readout = jnp.transpose(readout, (1, 0, 2, 3)).reshape(SSM_GROUPS, SSM_ROW, 2 * SSM_STATE)

    def inject(tau):
        p_re, p_im = powers(tau)
        bt_re = jnp.transpose(bb_re, (0, 2, 1))[None]
        bt_im = jnp.transpose(bb_im, (0, 2, 1))[None]
        i_re = p_re[:, :, None, :] * bt_re - p_im[:, :, None, :] * bt_im
        i_im = p_re[:, :, None, :] * bt_im + p_im[:, :, None, :] * bt_re
        both = jnp.concatenate([i_re, i_im], axis=-1)
        return jnp.transpose(both, (1, 3, 0, 2)).reshape(SSM_GROUPS, 2 * SSM_STATE, SSM_ROW)

    inject_back = inject(-(steps + 1.0))
    inject_end = inject(SSM_CHUNK - 1.0 - steps).astype(BF16)
    d_row = jnp.tile(d_skip.reshape(SSM_GROUPS, 1, SSM_GROUP), (1, 1, SSM_CHUNK))
    (ar,), (ai,) = powers([float(SSM_CHUNK)])
    a_chunk = jnp.stack([jnp.concatenate([ar, ar], -1), jnp.concatenate([-ai, ai], -1),
                         jnp.concatenate([ai, -ai], -1)], axis=1)
    return readout, inject_back, inject_end, d_row, a_chunk


def _ssm_kernel(ut_ref, ro_ref, ib_ref, ie_ref, d_ref, a_ref, yt_ref, p_scr, q_scr, s_scr, *, bsz):
    ut = ut_ref[...]
    n_tok = ut.shape[1]
    r_hi, r_lo = _split_bf16(ro_ref[...])
    i_hi, i_lo = _split_bf16(ib_ref[...])
    kern = (jnp.dot(r_hi, i_hi, preferred_element_type=F32)
            + jnp.dot(r_hi, i_lo, preferred_element_type=F32)
            + jnp.dot(r_lo, i_hi, preferred_element_type=F32))
    row = lax.broadcasted_iota(jnp.int32, kern.shape, 0)
    col = lax.broadcasted_iota(jnp.int32, kern.shape, 1)
    group_shift = SSM_GROUP.bit_length() - 1
    causal = jnp.right_shift(row, group_shift) >= jnp.right_shift(col, group_shift)
    kern = jnp.where(causal, kern, 0.0) + jnp.where(row == col, d_ref[...], 0.0)
    wt = jnp.dot(ie_ref[...], ut, preferred_element_type=F32)
    w = wt.T
    p_scr[...] = w
    q_scr[...] = pltpu.roll(w, SSM_STATE, 1)
    a1 = a_ref[0:1, :]
    a2 = a_ref[1:2, :]
    a3 = a_ref[2:3, :]

    def block(kb, carry):
        cp, cq = carry
        base = pl.multiple_of(kb * (bsz * SSM_KBLOCK), bsz * SSM_KBLOCK)
        for k in range(SSM_KBLOCK):
            rows = pl.ds(base + k, bsz, stride=SSM_KBLOCK)
            s_scr[rows, :] = cp
            cp, cq = (a1 * cp + a2 * cq + p_scr[rows, :],
                      a1 * cq + a3 * cp + q_scr[rows, :])
        return cp, cq

    zero = jnp.zeros((bsz, 2 * SSM_STATE), F32)
    lax.fori_loop(0, n_tok // (bsz * SSM_KBLOCK), block, (zero, zero))
    yt = jnp.dot(kern.astype(BF16), ut, preferred_element_type=F32)
    yt = yt + lax.dot_general(r_hi, s_scr[...].astype(BF16), NT_DIMS, preferred_element_type=F32)
    yt_ref[...] = yt


def _ssm(ut, tables, bsz):
    readout, inject_back, inject_end, d_row, a_chunk = tables
    n_tok = ut.shape[2]
    per_group = lambda a, b: pl.BlockSpec((None, a, b), lambda g: (g, 0, 0))
    state = pltpu.VMEM((n_tok, 2 * SSM_STATE), F32)
    return pl.pallas_call(
        functools.partial(_ssm_kernel, bsz=bsz),
        grid=(SSM_GROUPS,),
        in_specs=[per_group(SSM_ROW, n_tok), per_group(SSM_ROW, 2 * SSM_STATE),
                  per_group(2 * SSM_STATE, SSM_ROW), per_group(2 * SSM_STATE, SSM_ROW),
                  per_group(1, SSM_ROW), per_group(3, 2 * SSM_STATE)],
        out_specs=per_group(SSM_ROW, n_tok),
        out_shape=jax.ShapeDtypeStruct((SSM_GROUPS, SSM_ROW, n_tok), F32),
        scratch_shapes=[state, state, state],
        compiler_params=_compiler_params(1),
        name="ssm",
    )(ut, readout, inject_back, inject_end, d_row, a_chunk)


def _split_bf16(a):
    hi = a.astype(BF16)
    lo = (a - hi.astype(F32)).astype(BF16)
    return hi, lo


def _route(logits):
    lane = lax.broadcasted_iota(jnp.int32, logits.shape, 1)
    lane_f = lane.astype(F32)
    big = float(ROUTER_LANES)

    def first_argmax(vals, vmax):
        return jnp.min(jnp.where(vals == vmax, lane_f, big), axis=-1, keepdims=True)

    g_mask = (lane >= N_EXPERTS) & (lane < N_EXPERTS + N_GROUPS)
    g_log = jnp.where(g_mask, logits, NEG_INF)
    g_max = jnp.max(g_log, axis=-1, keepdims=True)
    g_sum = jnp.sum(jnp.where(g_mask, jnp.exp(g_log - g_max), 0.0), axis=-1, keepdims=True)
    g_prob = 1.0 / g_sum
    g_idx = first_argmax(g_log, g_max) - float(N_EXPERTS)
    e_lo = g_idx * float(EXPERTS_PER_GROUP)
    e_mask = (lane_f >= e_lo) & (lane_f < e_lo + float(EXPERTS_PER_GROUP))
    e_log = jnp.where(e_mask, logits, NEG_INF)
    e_max1 = jnp.max(e_log, axis=-1, keepdims=True)
    idx1 = first_argmax(e_log, e_max1)
    e_log2 = jnp.where(lane_f == idx1, NEG_INF, e_log)
    e_max2 = jnp.max(e_log2, axis=-1, keepdims=True)
    idx2 = first_argmax(e_log2, e_max2)
    r = jnp.exp(e_max2 - e_max1)
    w1 = g_prob / (1.0 + r)
    w2 = g_prob * r / (1.0 + r)
    gates = jnp.where(lane_f == idx1, w1, 0.0) + jnp.where(lane_f == idx2, w2, 0.0)
    lo = jnp.minimum(idx1, idx2) - e_lo
    hi = jnp.maximum(idx1, idx2) - e_lo
    pair = lo * (2.0 * EXPERTS_PER_GROUP - 1.0 - lo) * 0.5 + (hi - lo - 1.0)
    return gates, g_idx * float(PAIRS_PER_GROUP) + pair


def _merge_kernel(x_ref, ya_ref, z_ref, gain1_ref, wgate_ref, bgate_ref, wglu_ref, bglu_ref,
                  wba_ref, wbs_ref, wout_ref, gain2_ref, rhi_ref, rlo_ref, rbias_ref, ltri_ref,
                  xg_ref, counts_ref, count_scr):
    x = x_ref[...]
    hn = _rms_norm(x, gain1_ref[...]).astype(BF16)
    z = z_ref[...]
    glu = jnp.dot(z, wglu_ref[...], preferred_element_type=F32) + bglu_ref[...]
    y_ssm = (z.astype(F32) * jax.nn.sigmoid(glu)).astype(BF16)
    merged = None
    for s, (y_b, wb_ref) in enumerate(((ya_ref[...], wba_ref), (y_ssm, wbs_ref))):
        cols = slice(s * D_MODEL, (s + 1) * D_MODEL)
        gate = jax.nn.sigmoid(jnp.dot(hn, wgate_ref[:, cols], preferred_element_type=F32)
                              + bgate_ref[:, cols])
        term = gate * jnp.dot(y_b, wb_ref[...], preferred_element_type=F32)
        merged = term if merged is None else merged + term
    x1 = x + jnp.dot(merged.astype(BF16), wout_ref[...], preferred_element_type=F32)
    xg_ref[:, :D_MODEL] = x1

    h = _rms_norm(x1, gain2_ref[...])
    h_hi, h_lo = _split_bf16(h)
    logits = (jnp.dot(h_hi, rhi_ref[...], preferred_element_type=F32)
              + jnp.dot(h_hi, rlo_ref[...], preferred_element_type=F32)
              + jnp.dot(h_lo, rhi_ref[...], preferred_element_type=F32)
              + rbias_ref[...])
    gates, cls = _route(logits)

    @pl.when(pl.program_id(0) == 0)
    def _():
        count_scr[...] = jnp.zeros_like(count_scr)

    lane_f = lax.broadcasted_iota(jnp.int32, gates.shape, 1).astype(F32)
    onehot = jnp.where(lane_f == cls, 1.0, 0.0)
    before = jnp.dot(ltri_ref[...], onehot.astype(BF16), preferred_element_type=F32)
    seen = count_scr[...]
    rank = jnp.sum(onehot * (before + seen), axis=-1, keepdims=True)
    count_scr[...] = seen + jnp.sum(onehot, axis=0, keepdims=True)
    counts_ref[...] = count_scr[...]
    xg_ref[:, D_MODEL:] = (gates + jnp.where(lane_f == float(CLASS_LANE), cls, 0.0)
                           + jnp.where(lane_f == float(RANK_LANE), rank, 0.0))


def _merge(x2, y_attn, z, gain1, w_gate, b_gate, w_glu, b_glu, w_branch, w_out, ffn_gain,
           w_group_router, group_bias, w_expert_router, expert_bias, tm):
    t = x2.shape[0]
    pad = ROUTER_LANES - N_EXPERTS - N_GROUPS
    w_r = jnp.concatenate([w_expert_router, w_group_router,
                           jnp.zeros((D_MODEL, pad), F32)], axis=1)
    r_hi, r_lo = _split_bf16(w_r)
    r_bias = jnp.concatenate([expert_bias, group_bias, jnp.zeros((pad,), F32)])[None, :]
    ltri = jnp.tril(jnp.ones((tm, tm), F32), -1).astype(BF16)
    row = lambda n: pl.BlockSpec((tm, n), lambda i: (i, 0))
    resident = lambda shape: pl.BlockSpec(shape, lambda i: (0, 0), pipeline_mode=pl.Buffered(1))
    return pl.pallas_call(
        _merge_kernel,
        grid=(t // tm,),
        in_specs=[row(D_MODEL), row(D_ATTN), row(D_SSM),
                  resident((1, D_MODEL)), resident((D_MODEL, N_BRANCH * D_MODEL)),
                  resident((1, N_BRANCH * D_MODEL)),
                  resident((D_SSM, D_SSM)), resident((1, D_SSM)),
                  resident((D_ATTN, D_MODEL)), resident((D_SSM, D_MODEL)),
                  resident((D_MODEL, D_MODEL)), resident((1, D_MODEL)),
                  resident((D_MODEL, ROUTER_LANES)), resident((D_MODEL, ROUTER_LANES)),
                  resident((1, ROUTER_LANES)), resident((tm, tm))],
        out_specs=[row(D_ROUTED), pl.BlockSpec((1, ROUTER_LANES), lambda i: (0, 0))],
        out_shape=[jax.ShapeDtypeStruct((t, D_ROUTED), F32),
                   jax.ShapeDtypeStruct((1, ROUTER_LANES), F32)],
        scratch_shapes=[pltpu.VMEM((1, ROUTER_LANES), F32)],
        compiler_params=_compiler_params(1),
        name="merge",
    )(x2, y_attn, z, gain1[None, :], w_gate.astype(BF16), b_gate[None, :],
      w_glu.astype(BF16), b_glu[None, :],
      w_branch[:D_ATTN].astype(BF16), w_branch[D_ATTN:].astype(BF16),
      w_out.astype(BF16), ffn_gain[None, :], r_hi, r_lo, r_bias, ltri)


def _moe_plan(routed, counts, n_tiles):
    cnt = counts[0, :N_CLASSES].astype(jnp.int32)
    padded = (cnt + (MOE_TILE - 1)) // MOE_TILE * MOE_TILE
    ends = jnp.cumsum(padded)
    starts = ends - padded
    classes = jnp.arange(N_CLASSES, dtype=jnp.int32)
    cls = routed[:, D_MODEL + CLASS_LANE].astype(jnp.int32)
    rank = routed[:, D_MODEL + RANK_LANE].astype(jnp.int32)
    dest = rank + jnp.sum(jnp.where(cls[:, None] == classes[None, :], starts[None, :], 0), axis=1)
    n_used = ends[-1] // MOE_TILE
    tile_end = ends // MOE_TILE
    tiles = jnp.arange(n_tiles, dtype=jnp.int32)
    tile_cls = jnp.minimum(jnp.sum(tiles[:, None] >= tile_end[None, :], axis=1), N_CLASSES - 1)
    pick = lambda table: jnp.sum(
        jnp.where(tile_cls[:, None] == classes[None, :], jnp.asarray(table, jnp.int32)[None, :], 0),
        axis=1).astype(jnp.int32)
    tile_ea = pick([e for e, _ in _CLASS_EXPERTS])
    tile_eb = pick([e for _, e in _CLASS_EXPERTS])
    clear_a = jnp.where(padded > 0, tile_end - 1, -1)
    tail = n_used + classes
    clear_b = jnp.where(tail < n_tiles, tail, -1)
    clear = jnp.concatenate([clear_a, clear_b]).astype(jnp.int32)
    return dest.astype(jnp.int32), tile_ea, tile_eb, jnp.reshape(n_used, (1,)).astype(jnp.int32), clear


def _row_copy(src_hbm, src_row, dst_hbm, dst_row, sem):
    return pltpu.make_async_copy(src_hbm.at[pl.ds(src_row, 1), :], dst_hbm.at[pl.ds(dst_row, 1), :], sem)


def _dispatch_kernel(clear_ref, dest_ref, src_hbm, dst_hbm, zero_scr, sem, *, rows):
    step = pl.program_id(0)

    def clear_copy(j):
        start = pl.multiple_of(clear_ref[j] * MOE_TILE, MOE_TILE)
        return pltpu.make_async_copy(zero_scr, dst_hbm.at[pl.ds(start, MOE_TILE), :], sem)

    @pl.when(step == 0)
    def _():
        zero_scr[...] = jnp.zeros_like(zero_scr)
        for wait in (False, True):
            for j in range(2 * N_CLASSES):
                @pl.when(clear_ref[j] >= 0)
                def _():
                    clear_copy(j).wait() if wait else clear_copy(j).start()

    base = step * rows

    def issue(r, carry):
        _row_copy(src_hbm, base + r, dst_hbm, dest_ref[r], sem).start()
        return carry

    def drain(r, carry):
        _row_copy(src_hbm, 0, dst_hbm, 0, sem).wait()
        return carry

    lax.fori_loop(0, rows, issue, 0, unroll=8)
    lax.fori_loop(0, rows, drain, 0, unroll=8)


def _dispatch(routed, dest, clear, n_tiles, rows):
    t, width = routed.shape
    return pl.pallas_call(
        functools.partial(_dispatch_kernel, rows=rows),
        grid_spec=pltpu.PrefetchScalarGridSpec(
            num_scalar_prefetch=1,
            grid=(t // rows,),
            in_specs=[pl.BlockSpec((rows,), lambda i, clear: (i,), memory_space=pltpu.SMEM),
                      pl.BlockSpec(memory_space=pl.ANY)],
            out_specs=pl.BlockSpec(memory_space=pl.ANY),
            scratch_shapes=[pltpu.VMEM((MOE_TILE, width), F32), pltpu.SemaphoreType.DMA(())]),
        out_shape=jax.ShapeDtypeStruct((n_tiles * MOE_TILE, width), F32),
        compiler_params=_compiler_params(1),
        name="moe_dispatch",
    )(clear, dest, routed)


def _expert_kernel(ea_ref, eb_ref, used_ref, xs_ref, gain_ref, wga_ref, wua_ref, wda_ref,
                   wgb_ref, wub_ref, wdb_ref, o_ref):
    i = pl.program_id(0)

    @pl.when(i < used_ref[0])
    def _():
        x1 = xs_ref[:, :D_MODEL]
        gates = xs_ref[:, D_MODEL:]
        h = _rms_norm(x1, gain_ref[...]).astype(BF16)
        lane = lax.broadcasted_iota(jnp.int32, gates.shape, 1)
        acc = x1
        for e_ref, wg_ref, wu_ref, wd_ref in ((ea_ref, wga_ref, wua_ref, wda_ref),
                                              (eb_ref, wgb_ref, wub_ref, wdb_ref)):
            gate = jnp.sum(jnp.where(lane == e_ref[i], gates, 0.0), axis=-1, keepdims=True)
            a = jnp.dot(h, wg_ref[...], preferred_element_type=F32)
            u = jnp.dot(h, wu_ref[...], preferred_element_type=F32)
            hid = (jax.nn.silu(a) * u * gate).astype(BF16)
            acc = acc + jnp.dot(hid, wd_ref[...], preferred_element_type=F32)
        o_ref[...] = acc

    @pl.when(i >= used_ref[0])
    def _():
        o_ref[...] = jnp.zeros_like(o_ref)


def _experts(xs, tile_ea, tile_eb, n_used, gain, w_e_gate, w_e_up, w_e_down):
    n_tiles = xs.shape[0] // MOE_TILE
    by_a = lambda r, c: pl.BlockSpec((None, r, c), lambda i, ea, eb, used: (ea[i], 0, 0))
    by_b = lambda r, c: pl.BlockSpec((None, r, c), lambda i, ea, eb, used: (eb[i], 0, 0))
    wg, wu, wd = (w.astype(BF16) for w in (w_e_gate, w_e_up, w_e_down))
    return pl.pallas_call(
        _expert_kernel,
        grid_spec=pltpu.PrefetchScalarGridSpec(
            num_scalar_prefetch=3,
            grid=(n_tiles,),
            in_specs=[pl.BlockSpec((MOE_TILE, D_ROUTED),
                                   lambda i, ea, eb, used: (jnp.minimum(i, used[0] - 1), 0)),
                      pl.BlockSpec((1, D_MODEL), lambda i, ea, eb, used: (0, 0)),
                      by_a(D_MODEL, D_EXPERT), by_a(D_MODEL, D_EXPERT), by_a(D_EXPERT, D_MODEL),
                      by_b(D_MODEL, D_EXPERT), by_b(D_MODEL, D_EXPERT), by_b(D_EXPERT, D_MODEL)],
            out_specs=pl.BlockSpec((MOE_TILE, D_MODEL), lambda i, ea, eb, used: (i, 0))),
        out_shape=jax.ShapeDtypeStruct((n_tiles * MOE_TILE, D_MODEL), F32),
        compiler_params=_compiler_params(1),
        name="moe_experts",
    )(tile_ea, tile_eb, n_used, xs, gain[None, :], wg, wu, wd, wg, wu, wd)


def _unsort_kernel(dest_ref, src_hbm, dst_hbm, sem, *, rows):
    base = pl.program_id(0) * rows

    def issue(r, carry):
        _row_copy(src_hbm, dest_ref[r], dst_hbm, base + r, sem).start()
        return carry

    def drain(r, carry):
        _row_copy(src_hbm, 0, dst_hbm, 0, sem).wait()
        return carry

    lax.fori_loop(0, rows, issue, 0, unroll=8)
    lax.fori_loop(0, rows, drain, 0, unroll=8)


def _unsort(xo, dest, rows):
    t = dest.shape[0]
    return pl.pallas_call(
        functools.partial(_unsort_kernel, rows=rows),
        grid=(t // rows,),
        in_specs=[pl.BlockSpec((rows,), lambda i: (i,), memory_space=pltpu.SMEM),
                  pl.BlockSpec(memory_space=pl.ANY)],
        out_specs=pl.BlockSpec(memory_space=pl.ANY),
        out_shape=jax.ShapeDtypeStruct((t, xo.shape[1]), F32),
        scratch_shapes=[pltpu.SemaphoreType.DMA(())],
        compiler_params=_compiler_params(1),
        name="moe_unsort",
    )(dest, xo)


def _moe(routed, counts, ffn_gain, w_e_gate, w_e_up, w_e_down, rows):
    t = routed.shape[0]
    assert t % MOE_TILE == 0 and t % rows == 0
    n_tiles = t // MOE_TILE + N_CLASSES
    dest, tile_ea, tile_eb, n_used, clear = _moe_plan(routed, counts, n_tiles)
    xs = _dispatch(routed, dest, clear, n_tiles, rows)
    xo = _experts(xs, tile_ea, tile_eb, n_used, ffn_gain, w_e_gate, w_e_up, w_e_down)
    return _unsort(xo, dest, rows)


def kernel(x, mix_norm_gain, w_in, b_gate, q_gain, k_gain, rel_bias, ssm_lambda_re, ssm_lambda_im, ssm_log_step, ssm_b_re, ssm_b_im, ssm_c_re, ssm_c_im, ssm_d, w_glu, b_glu, w_branch, w_out, ffn_norm_gain, w_group_router, group_bias, w_expert_router, expert_bias, w_e_gate, w_e_up, w_e_down):
    bsz, seq, _ = x.shape
    assert seq % (SSM_KBLOCK * SSM_CHUNK) == 0 and seq % (2 * CHUNK) == 0
    depth = w_in.shape[0]
    tiles = _tiles(bsz * seq)
    x2 = x.reshape(bsz * seq, D_MODEL)
    for i in range(depth):
        w_qkv = w_in[i][:, :D_QKV]
        w_u = w_in[i][:, D_QKV:D_QKV + D_SSM]
        w_gate = w_in[i][:, D_QKV + D_SSM:]
        q, k, v = _inproj(x2, mix_norm_gain[i], w_qkv, q_gain[i], k_gain[i], tiles["inproj"])
        y_attn = _attention(q, k, v, _rel_bias_table(rel_bias[i]), bsz, seq)
        u_t = _ssm_inproj(x2, mix_norm_gain[i], w_u, bsz, seq)
        tables = _ssm_tables(ssm_lambda_re[i], ssm_lambda_im[i], ssm_log_step[i],
                             ssm_b_re[i], ssm_b_im[i], ssm_c_re[i], ssm_c_im[i], ssm_d[i])
        z = _ssm_out(_ssm(u_t, tables, bsz), bsz, seq)
        routed, counts = _merge(x2, y_attn, z, mix_norm_gain[i], w_gate, b_gate[i], w_glu[i],
                                b_glu[i], w_branch[i], w_out[i], ffn_norm_gain[i],
                                w_group_router[i], group_bias[i], w_expert_router[i],
                                expert_bias[i], tiles["merge"])
        x2 = _moe(routed, counts, ffn_norm_gain[i], w_e_gate[i], w_e_up[i], w_e_down[i],
                  tiles["moe_rows"])
    return x2.reshape(bsz, seq, D_MODEL)
```

```python
import functools

import jax
import jax.numpy as jnp
from jax import lax
from jax.experimental import pallas as pl
from jax.experimental.pallas import tpu as pltpu

F32 = jnp.float32
BF16 = jnp.bfloat16

D_MODEL = 1024
CHUNK = 64
LEFT_CHUNKS = 8
BAND = LEFT_CHUNKS + 1
BAND_KEYS = BAND * CHUNK
PAD_KEYS = LEFT_CHUNKS * CHUNK
ATTN_HEADS = 8
HEAD_DIM = 64
D_ATTN = ATTN_HEADS * HEAD_DIM
MAX_REL = 256
REL_FUTURE = CHUNK - 1
D_SSM = D_MODEL // 2
SSM_GROUP = 16
SSM_GROUPS = D_SSM // SSM_GROUP
SSM_STATE = 64
N_BRANCH = 2
D_QKV = 3 * D_ATTN
D_IN = D_QKV + D_SSM + N_BRANCH * D_MODEL
N_GROUPS = 4
EXPERTS_PER_GROUP = 4
N_EXPERTS = N_GROUPS * EXPERTS_PER_GROUP
D_EXPERT = D_MODEL // 4
EPS = 1e-6
NEG_INF = -1e30
LOG2_E = 1.4426950408889634

LANES = 128
HEAD_PAIRS = D_ATTN // LANES
SSM_CHUNK = 16
SSM_ROW = SSM_CHUNK * SSM_GROUP
SSM_KBLOCK = 16
SSM_BGROUP = LANES // SSM_KBLOCK
ROUTER_LANES = LANES
D_ROUTED = D_MODEL + ROUTER_LANES
CLASS_LANE = 32
RANK_LANE = 33
MOE_TILE = 256
_PAIRS = [(a, b) for a in range(EXPERTS_PER_GROUP) for b in range(a + 1, EXPERTS_PER_GROUP)]
PAIRS_PER_GROUP = len(_PAIRS)
N_CLASSES = N_GROUPS * PAIRS_PER_GROUP
_CLASS_EXPERTS = [(g * EXPERTS_PER_GROUP + a, g * EXPERTS_PER_GROUP + b)
                  for g in range(N_GROUPS) for a, b in _PAIRS]
VMEM_LIMIT = 56 * 1024 * 1024


def _tiles(n_tokens):
    return dict(inproj=512, merge=512, moe_rows=512)


def _compiler_params(n_axes):
    return pltpu.CompilerParams(
        dimension_semantics=("arbitrary",) * n_axes,
        vmem_limit_bytes=VMEM_LIMIT)


def _full(shape):
    nd = len(shape)
    return pl.BlockSpec(shape, lambda *_: (0,) * nd)


def _rms_norm(x, gain):
    ms = jnp.mean(x * x, axis=-1, keepdims=True)
    return x * lax.rsqrt(ms + EPS) * gain


NT_DIMS = (((1,), (1,)), ((), ()))


def _inproj_kernel(x_ref, gain_ref, w_ref, qg_ref, kg_ref, ones_ref, q_ref, k_ref, v_ref):
    h = _rms_norm(x_ref[...], gain_ref[...]).astype(BF16)

    def proj(lo, n):
        return jnp.dot(h, w_ref[:, lo:lo + n], preferred_element_type=F32)

    def head_norm(t, g_ref):
        ss = jnp.dot((t * t).astype(BF16), ones_ref[...], preferred_element_type=F32)
        return t * lax.rsqrt(ss * (1.0 / HEAD_DIM) + EPS) * g_ref[...]

    q_ref[...] = head_norm(proj(0, D_ATTN), qg_ref).astype(BF16)
    k_ref[...] = head_norm(proj(D_ATTN, D_ATTN), kg_ref).astype(BF16)
    v_ref[...] = proj(2 * D_ATTN, D_ATTN).astype(BF16)


def _inproj(x2, gain, w_qkv, q_gain, k_gain, tm):
    t = x2.shape[0]
    ones = jnp.kron(jnp.eye(ATTN_HEADS, dtype=F32), jnp.ones((HEAD_DIM, HEAD_DIM), F32)).astype(BF16)
    qg = jnp.tile(q_gain, ATTN_HEADS)[None, :] * (HEAD_DIM ** -0.5 * LOG2_E)
    kg = jnp.tile(k_gain, ATTN_HEADS)[None, :]
    row = lambda n: pl.BlockSpec((tm, n), lambda i: (i, 0))
    return pl.pallas_call(
        _inproj_kernel,
        grid=(t // tm,),
        in_specs=[row(D_MODEL), _full((1, D_MODEL)), _full((D_MODEL, D_QKV)),
                  _full((1, D_ATTN)), _full((1, D_ATTN)), _full((D_ATTN, D_ATTN))],
        out_specs=[row(D_ATTN)] * 3,
        out_shape=[jax.ShapeDtypeStruct((t, D_ATTN), BF16)] * 3,
        compiler_params=_compiler_params(1),
        name="inproj",
    )(x2, gain[None, :], w_qkv.astype(BF16), qg, kg, ones)


def _attn_kernel(q_ref, k_ref, v_ref, bias_ref, o_ref, kpad, v_even, v_odd, s_a, s_b):
    seq = q_ref.shape[0]
    n_chunks = seq // CHUNK
    lane_v = lax.broadcasted_iota(jnp.int32, (seq, D_ATTN), 1)
    even_v = (lane_v % LANES) < HEAD_DIM
    zeros = jnp.zeros((PAD_KEYS, D_ATTN), BF16)
    v = v_ref[...]
    kpad[0:PAD_KEYS, :] = zeros
    v_even[0:PAD_KEYS, :] = zeros
    v_odd[0:PAD_KEYS, :] = zeros
    kpad[PAD_KEYS:, :] = k_ref[...]
    v_even[PAD_KEYS:, :] = jnp.where(even_v, v, jnp.zeros_like(v))
    v_odd[PAD_KEYS:, :] = jnp.where(even_v, jnp.zeros_like(v), v)

    lane = lax.broadcasted_iota(jnp.int32, (CHUNK, LANES), 1)
    first_head = lane < HEAD_DIM
    key_col = lax.broadcasted_iota(jnp.int32, (2 * CHUNK, BAND_KEYS), 1)

    def scores(n, s_buf):
        r0 = pl.multiple_of(n * CHUNK, CHUNK)
        for j in range(HEAD_PAIRS):
            cols = slice(j * LANES, (j + 1) * LANES)
            qp = q_ref[pl.ds(r0, CHUNK), cols]
            zq = jnp.zeros_like(qp)
            q2 = jnp.concatenate([jnp.where(first_head, qp, zq), jnp.where(first_head, zq, qp)], axis=0)
            kb = kpad[pl.ds(r0, BAND_KEYS), cols]
            s_buf[j * 2 * CHUNK:(j + 1) * 2 * CHUNK, :] = lax.dot_general(
                q2, kb, NT_DIMS, preferred_element_type=F32)

    def softmax_pv(n, s_buf, masked):
        r0 = pl.multiple_of(n * CHUNK, CHUNK)
        for j in range(HEAD_PAIRS):
            cols = slice(j * LANES, (j + 1) * LANES)
            rows = slice(j * 2 * CHUNK, (j + 1) * 2 * CHUNK)
            s = s_buf[rows, :] + bias_ref[rows, :]
            if masked:
                s = jnp.where(key_col >= (LEFT_CHUNKS - n) * CHUNK, s, NEG_INF)
            m = jnp.max(s, axis=-1, keepdims=True)
            p = jnp.exp2(s - m)
            inv_l = 1.0 / jnp.sum(p, axis=-1, keepdims=True)
            pb = p.astype(BF16)
            o0 = jnp.dot(pb[:CHUNK], v_even[pl.ds(r0, BAND_KEYS), cols], preferred_element_type=F32)
            o1 = jnp.dot(pb[CHUNK:], v_odd[pl.ds(r0, BAND_KEYS), cols], preferred_element_type=F32)
            o_ref[pl.ds(r0, CHUNK), cols] = (o0 * inv_l[:CHUNK] + o1 * inv_l[CHUNK:]).astype(BF16)

    def pair_body(i, carry, *, masked):
        n0 = 2 * i
        scores(n0 + 1, s_b)
        softmax_pv(n0, s_a, masked)
        scores(jnp.minimum(n0 + 2, n_chunks - 1), s_a)
        softmax_pv(n0 + 1, s_b, masked)
        return carry

    assert n_chunks % 2 == 0 and LEFT_CHUNKS % 2 == 0
    masked_pairs = min(LEFT_CHUNKS, n_chunks) // 2
    scores(0, s_a)
    lax.fori_loop(0, masked_pairs, functools.partial(pair_body, masked=True), 0)
    lax.fori_loop(masked_pairs, n_chunks // 2, functools.partial(pair_body, masked=False), 0)


def _attention(q, k, v, bias, bsz, seq):
    q3, k3, v3 = (a.reshape(bsz, seq, D_ATTN) for a in (q, k, v))
    seq_spec = pl.BlockSpec((None, seq, D_ATTN), lambda b: (b, 0, 0))
    padded = pltpu.VMEM((PAD_KEYS + seq, D_ATTN), BF16)
    out = pl.pallas_call(
        _attn_kernel,
        grid=(bsz,),
        in_specs=[seq_spec, seq_spec, seq_spec, _full((ATTN_HEADS * CHUNK, BAND_KEYS))],
        out_specs=seq_spec,
        out_shape=jax.ShapeDtypeStruct((bsz, seq, D_ATTN), BF16),
        scratch_shapes=[padded, padded, padded,
                        pltpu.VMEM((ATTN_HEADS * CHUNK, BAND_KEYS), F32),
                        pltpu.VMEM((ATTN_HEADS * CHUNK, BAND_KEYS), F32)],
        compiler_params=_compiler_params(1),
        name="attention",
    )(q3, k3, v3, bias)
    return out.reshape(bsz * seq, D_ATTN)


def _rel_bias_table(rel_bias):
    n_rel = REL_FUTURE + MAX_REL + 1
    far = jnp.broadcast_to(rel_bias[:, n_rel - 1:], (ATTN_HEADS, PAD_KEYS - MAX_REL + CHUNK))
    ext = jnp.concatenate([far, rel_bias[:, n_rel - 2::-1]], axis=1) * LOG2_E
    rows = [ext[:, CHUNK - 1 - q:CHUNK - 1 - q + BAND_KEYS] for q in range(CHUNK)]
    return jnp.stack(rows, axis=1).reshape(ATTN_HEADS * CHUNK, BAND_KEYS)


SSM_BLOCK_LEN = SSM_KBLOCK * SSM_CHUNK
SSM_BLOCK_TOK = SSM_BGROUP * SSM_KBLOCK


def _token_block_spec(width):
    return pl.BlockSpec((SSM_BGROUP, SSM_BLOCK_LEN, width), lambda kb, bg: (bg, kb, 0))


def _transposed_spec(n_bgroups):
    return pl.BlockSpec((SSM_GROUPS, SSM_ROW, SSM_BLOCK_TOK),
                        lambda kb, bg: (0, 0, kb * n_bgroups + bg))


def _token_grid(bsz, seq):
    assert bsz % SSM_BGROUP == 0 and seq % SSM_BLOCK_LEN == 0
    return (seq // SSM_BLOCK_LEN, bsz // SSM_BGROUP)


def _ssm_inproj_kernel(x_ref, gain_ref, wu_ref, ut_ref, u_scr):
    n_rows = SSM_BGROUP * SSM_BLOCK_LEN
    x = x_ref[...].reshape(n_rows, D_MODEL)
    h = _rms_norm(x, gain_ref[...]).astype(BF16)
    u = jnp.dot(h, wu_ref[...], preferred_element_type=F32)
    for q in range(D_SSM // LANES):
        u_scr[q] = u[:, q * LANES:(q + 1) * LANES]
    groups_per_tile = LANES // SSM_GROUP
    for t in range(SSM_CHUNK):
        for q in range(D_SSM // LANES):
            ut = u_scr[q, pl.ds(t, SSM_BLOCK_TOK, stride=SSM_CHUNK), :].T.astype(BF16)
            for gl in range(groups_per_tile):
                ut_ref[q * groups_per_tile + gl, t * SSM_GROUP:(t + 1) * SSM_GROUP, :] = (
                    ut[gl * SSM_GROUP:(gl + 1) * SSM_GROUP, :])


def _ssm_inproj(x2, gain, w_u, bsz, seq):
    grid = _token_grid(bsz, seq)
    return pl.pallas_call(
        _ssm_inproj_kernel,
        grid=grid,
        in_specs=[_token_block_spec(D_MODEL), _full((1, D_MODEL)), _full((D_MODEL, D_SSM))],
        out_specs=_transposed_spec(grid[1]),
        out_shape=jax.ShapeDtypeStruct((SSM_GROUPS, SSM_ROW, bsz * seq // SSM_CHUNK), BF16),
        scratch_shapes=[pltpu.VMEM((D_SSM // LANES, SSM_BGROUP * SSM_BLOCK_LEN, LANES), F32)],
        compiler_params=_compiler_params(2),
        name="ssm_inproj",
    )(x2.reshape(bsz, seq, D_MODEL), gain[None, :], w_u.astype(BF16))


def _ssm_out_kernel(yt_ref, z_ref, y_scr):
    groups_per_tile = LANES // SSM_GROUP
    for t in range(SSM_CHUNK):
        for q in range(D_SSM // LANES):
            yt = jnp.concatenate(
                [yt_ref[q * groups_per_tile + gl, t * SSM_GROUP:(t + 1) * SSM_GROUP, :]
                 for gl in range(groups_per_tile)], axis=0)
            y_scr[q, pl.ds(t, SSM_BLOCK_TOK, stride=SSM_CHUNK), :] = yt.T
    for q in range(D_SSM // LANES):
        z = jax.nn.gelu(y_scr[q])
        z_ref[:, :, q * LANES:(q + 1) * LANES] = z.astype(BF16).reshape(
            SSM_BGROUP, SSM_BLOCK_LEN, LANES)


def _ssm_out(y_t, bsz, seq):
    grid = _token_grid(bsz, seq)
    z = pl.pallas_call(
        _ssm_out_kernel,
        grid=grid,
        in_specs=[_transposed_spec(grid[1])],
        out_specs=_token_block_spec(D_SSM),
        out_shape=jax.ShapeDtypeStruct((bsz, seq, D_SSM), BF16),
        scratch_shapes=[pltpu.VMEM((D_SSM // LANES, SSM_BGROUP * SSM_BLOCK_LEN, LANES), F32)],
        compiler_params=_compiler_params(2),
        name="ssm_out",
    )(y_t)
    return z.reshape(bsz * seq, D_SSM)


def _ssm_tables(lambda_re, lambda_im, log_step, b_re, b_im, c_re, c_im, d_skip):
    step = jnp.exp(log_step)[:, None]
    mag = jnp.exp(lambda_re * step)
    ang = lambda_im * step
    a_re = mag * jnp.cos(ang)
    a_im = mag * jnp.sin(ang)
    num_re = a_re - 1.0
    num_im = a_im
    den = lambda_re * lambda_re + lambda_im * lambda_im
    f_re = (num_re * lambda_re + num_im * lambda_im) / den
    f_im = (num_im * lambda_re - num_re * lambda_im) / den
    bb_re = f_re[..., None] * b_re - f_im[..., None] * b_im
    bb_im = f_re[..., None] * b_im + f_im[..., None] * b_re
    log_mag = lambda_re * step

    def powers(tau):
        tau = jnp.asarray(tau, F32)[:, None, None]
        pmag = jnp.exp(tau * log_mag[None])
        return pmag * jnp.cos(tau * ang[None]), pmag * jnp.sin(tau * ang[None])

    steps = jnp.arange(SSM_CHUNK)
    up_re, up_im = powers(steps + 1)
    cp_re = c_re[None] * up_re[:, :, None, :] - c_im[None] * up_im[:, :, None, :]
    cp_im = c_re[None] * up_im[:, :, None, :] + c_im[None] * up_re[:, :, None, :]
    readout = jnp.concatenate([cp_re, -cp_im], axis=-1)
    readout = jnp.transpose(readout, (1, 0, 2, 3)).reshape(SSM_GROUPS, SSM_ROW, 2 * SSM_STATE)

    def inject(tau):
        p_re, p_im = powers(tau)
        bt_re = jnp.transpose(bb_re, (0, 2, 1))[None]
        bt_im = jnp.transpose(bb_im, (0, 2, 1))[None]
        i_re = p_re[:, :, None, :] * bt_re - p_im[:, :, None, :] * bt_im
        i_im = p_re[:, :, None, :] * bt_im + p_im[:, :, None, :] * bt_re
        both = jnp.concatenate([i_re, i_im], axis=-1)
        return jnp.transpose(both, (1, 3, 0, 2)).reshape(SSM_GROUPS, 2 * SSM_STATE, SSM_ROW)

    inject_back = inject(-(steps + 1.0))
    inject_end = inject(SSM_CHUNK - 1.0 - steps).astype(BF16)
    d_row = jnp.tile(d_skip.reshape(SSM_GROUPS, 1, SSM_GROUP), (1, 1, SSM_CHUNK))
    (ar,), (ai,) = powers([float(SSM_CHUNK)])
    a_chunk = jnp.stack([jnp.concatenate([ar, ar], -1), jnp.concatenate([-ai, ai], -1),
                         jnp.concatenate([ai, -ai], -1)], axis=1)
    return readout, inject_back, inject_end, d_row, a_chunk


def _ssm_kernel(ut_ref, ro_ref, ib_ref, ie_ref, d_ref, a_ref, yt_ref, p_scr, q_scr, s_scr, *, bsz):
    ut = ut_ref[...]
    n_tok = ut.shape[1]
    r_hi, r_lo = _split_bf16(ro_ref[...])
    i_hi, i_lo = _split_bf16(ib_ref[...])
    kern = (jnp.dot(r_hi, i_hi, preferred_element_type=F32)
            + jnp.dot(r_hi, i_lo, preferred_element_type=F32)
            + jnp.dot(r_lo, i_hi, preferred_element_type=F32))
    row = lax.broadcasted_iota(jnp.int32, kern.shape, 0)
    col = lax.broadcasted_iota(jnp.int32, kern.shape, 1)
    group_shift = SSM_GROUP.bit_length() - 1
    causal = jnp.right_shift(row, group_shift) >= jnp.right_shift(col, group_shift)
    kern = jnp.where(causal, kern, 0.0) + jnp.where(row == col, d_ref[...], 0.0)
    wt = jnp.dot(ie_ref[...], ut, preferred_element_type=F32)
    w = wt.T
    p_scr[...] = w
    q_scr[...] = pltpu.roll(w, SSM_STATE, 1)
    a1 = a_ref[0:1, :]
    a2 = a_ref[1:2, :]
    a3 = a_ref[2:3, :]

    def block(kb, carry):
        cp, cq = carry
        base = pl.multiple_of(kb * (bsz * SSM_KBLOCK), bsz * SSM_KBLOCK)
        for k in range(SSM_KBLOCK):
            rows = pl.ds(base + k, bsz, stride=SSM_KBLOCK)
            s_scr[rows, :] = cp
            cp, cq = (a1 * cp + a2 * cq + p_scr[rows, :],
                      a1 * cq + a3 * cp + q_scr[rows, :])
        return cp, cq

    zero = jnp.zeros((bsz, 2 * SSM_STATE), F32)
    lax.fori_loop(0, n_tok // (bsz * SSM_KBLOCK), block, (zero, zero))
    yt = jnp.dot(kern.astype(BF16), ut, preferred_element_type=F32)
    yt = yt + lax.dot_general(r_hi, s_scr[...].astype(BF16), NT_DIMS, preferred_element_type=F32)
    yt_ref[...] = yt


def _ssm(ut, tables, bsz):
    readout, inject_back, inject_end, d_row, a_chunk = tables
    n_tok = ut.shape[2]
    per_group = lambda a, b: pl.BlockSpec((None, a, b), lambda g: (g, 0, 0))
    state = pltpu.VMEM((n_tok, 2 * SSM_STATE), F32)
    return pl.pallas_call(
        functools.partial(_ssm_kernel, bsz=bsz),
        grid=(SSM_GROUPS,),
        in_specs=[per_group(SSM_ROW, n_tok), per_group(SSM_ROW, 2 * SSM_STATE),
                  per_group(2 * SSM_STATE, SSM_ROW), per_group(2 * SSM_STATE, SSM_ROW),
                  per_group(1, SSM_ROW), per_group(3, 2 * SSM_STATE)],
        out_specs=per_group(SSM_ROW, n_tok),
        out_shape=jax.ShapeDtypeStruct((SSM_GROUPS, SSM_ROW, n_tok), F32),
        scratch_shapes=[state, state, state],
        compiler_params=_compiler_params(1),
        name="ssm",
    )(ut, readout, inject_back, inject_end, d_row, a_chunk)


def _split_bf16(a):
    hi = a.astype(BF16)
    lo = (a - hi.astype(F32)).astype(BF16)
    return hi, lo


def _route(logits):
    lane = lax.broadcasted_iota(jnp.int32, logits.shape, 1)
    lane_f = lane.astype(F32)
    big = float(ROUTER_LANES)

    def first_argmax(vals, vmax):
        return jnp.min(jnp.where(vals == vmax, lane_f, big), axis=-1, keepdims=True)

    g_mask = (lane >= N_EXPERTS) & (lane < N_EXPERTS + N_GROUPS)
    g_log = jnp.where(g_mask, logits, NEG_INF)
    g_max = jnp.max(g_log, axis=-1, keepdims=True)
    g_sum = jnp.sum(jnp.where(g_mask, jnp.exp(g_log - g_max), 0.0), axis=-1, keepdims=True)
    g_prob = 1.0 / g_sum
    g_idx = first_argmax(g_log, g_max) - float(N_EXPERTS)
    e_lo = g_idx * float(EXPERTS_PER_GROUP)
    e_mask = (lane_f >= e_lo) & (lane_f < e_lo + float(EXPERTS_PER_GROUP))
    e_log = jnp.where(e_mask, logits, NEG_INF)
    e_max1 = jnp.max(e_log, axis=-1, keepdims=True)
    idx1 = first_argmax(e_log, e_max1)
    e_log2 = jnp.where(lane_f == idx1, NEG_INF, e_log)
    e_max2 = jnp.max(e_log2, axis=-1, keepdims=True)
    idx2 = first_argmax(e_log2, e_max2)
    r = jnp.exp(e_max2 - e_max1)
    w1 = g_prob / (1.0 + r)
    w2 = g_prob * r / (1.0 + r)
    gates = jnp.where(lane_f == idx1, w1, 0.0) + jnp.where(lane_f == idx2, w2, 0.0)
    lo = jnp.minimum(idx1, idx2) - e_lo
    hi = jnp.maximum(idx1, idx2) - e_lo
    pair = lo * (2.0 * EXPERTS_PER_GROUP - 1.0 - lo) * 0.5 + (hi - lo - 1.0)
    return gates, g_idx * float(PAIRS_PER_GROUP) + pair


def _merge_kernel(x_ref, ya_ref, z_ref, gain1_ref, wgate_ref, bgate_ref, wglu_ref, bglu_ref,
                  wba_ref, wbs_ref, wout_ref, gain2_ref, rhi_ref, rlo_ref, rbias_ref, ltri_ref,
                  xg_ref, counts_ref, count_scr):
    x = x_ref[...]
    hn = _rms_norm(x, gain1_ref[...]).astype(BF16)
    z = z_ref[...]
    glu = jnp.dot(z, wglu_ref[...], preferred_element_type=F32) + bglu_ref[...]
    y_ssm = (z.astype(F32) * jax.nn.sigmoid(glu)).astype(BF16)
    merged = None
    for s, (y_b, wb_ref) in enumerate(((ya_ref[...], wba_ref), (y_ssm, wbs_ref))):
        cols = slice(s * D_MODEL, (s + 1) * D_MODEL)
        gate = jax.nn.sigmoid(jnp.dot(hn, wgate_ref[:, cols], preferred_element_type=F32)
                              + bgate_ref[:, cols])
        term = gate * jnp.dot(y_b, wb_ref[...], preferred_element_type=F32)
        merged = term if merged is None else merged + term
    x1 = x + jnp.dot(merged.astype(BF16), wout_ref[...], preferred_element_type=F32)
    xg_ref[:, :D_MODEL] = x1

    h = _rms_norm(x1, gain2_ref[...])
    h_hi, h_lo = _split_bf16(h)
    logits = (jnp.dot(h_hi, rhi_ref[...], preferred_element_type=F32)
              + jnp.dot(h_hi, rlo_ref[...], preferred_element_type=F32)
              + jnp.dot(h_lo, rhi_ref[...], preferred_element_type=F32)
              + rbias_ref[...])
    gates, cls = _route(logits)

    @pl.when(pl.program_id(0) == 0)
    def _():
        count_scr[...] = jnp.zeros_like(count_scr)

    lane_f = lax.broadcasted_iota(jnp.int32, gates.shape, 1).astype(F32)
    onehot = jnp.where(lane_f == cls, 1.0, 0.0)
    before = jnp.dot(ltri_ref[...], onehot.astype(BF16), preferred_element_type=F32)
    seen = count_scr[...]
    rank = jnp.sum(onehot * (before + seen), axis=-1, keepdims=True)
    count_scr[...] = seen + jnp.sum(onehot, axis=0, keepdims=True)
    counts_ref[...] = count_scr[...]
    xg_ref[:, D_MODEL:] = (gates + jnp.where(lane_f == float(CLASS_LANE), cls, 0.0)
                           + jnp.where(lane_f == float(RANK_LANE), rank, 0.0))


def _merge(x2, y_attn, z, gain1, w_gate, b_gate, w_glu, b_glu, w_branch, w_out, ffn_gain,
           w_group_router, group_bias, w_expert_router, expert_bias, tm):
    t = x2.shape[0]
    pad = ROUTER_LANES - N_EXPERTS - N_GROUPS
    w_r = jnp.concatenate([w_expert_router, w_group_router,
                           jnp.zeros((D_MODEL, pad), F32)], axis=1)
    r_hi, r_lo = _split_bf16(w_r)
    r_bias = jnp.concatenate([expert_bias, group_bias, jnp.zeros((pad,), F32)])[None, :]
    ltri = jnp.tril(jnp.ones((tm, tm), F32), -1).astype(BF16)
    row = lambda n: pl.BlockSpec((tm, n), lambda i: (i, 0))
    resident = lambda shape: pl.BlockSpec(shape, lambda i: (0, 0), pipeline_mode=pl.Buffered(1))
    return pl.pallas_call(
        _merge_kernel,
        grid=(t // tm,),
        in_specs=[row(D_MODEL), row(D_ATTN), row(D_SSM),
                  resident((1, D_MODEL)), resident((D_MODEL, N_BRANCH * D_MODEL)),
                  resident((1, N_BRANCH * D_MODEL)),
                  resident((D_SSM, D_SSM)), resident((1, D_SSM)),
                  resident((D_ATTN, D_MODEL)), resident((D_SSM, D_MODEL)),
                  resident((D_MODEL, D_MODEL)), resident((1, D_MODEL)),
                  resident((D_MODEL, ROUTER_LANES)), resident((D_MODEL, ROUTER_LANES)),
                  resident((1, ROUTER_LANES)), resident((tm, tm))],
        out_specs=[row(D_ROUTED), pl.BlockSpec((1, ROUTER_LANES), lambda i: (0, 0))],
        out_shape=[jax.ShapeDtypeStruct((t, D_ROUTED), F32),
                   jax.ShapeDtypeStruct((1, ROUTER_LANES), F32)],
        scratch_shapes=[pltpu.VMEM((1, ROUTER_LANES), F32)],
        compiler_params=_compiler_params(1),
        name="merge",
    )(x2, y_attn, z, gain1[None, :], w_gate.astype(BF16), b_gate[None, :],
      w_glu.astype(BF16), b_glu[None, :],
      w_branch[:D_ATTN].astype(BF16), w_branch[D_ATTN:].astype(BF16),
      w_out.astype(BF16), ffn_gain[None, :], r_hi, r_lo, r_bias, ltri)


def _moe_plan(routed, counts, n_tiles):
    cnt = counts[0, :N_CLASSES].astype(jnp.int32)
    padded = (cnt + (MOE_TILE - 1)) // MOE_TILE * MOE_TILE
    ends = jnp.cumsum(padded)
    starts = ends - padded
    classes = jnp.arange(N_CLASSES, dtype=jnp.int32)
    cls = routed[:, D_MODEL + CLASS_LANE].astype(jnp.int32)
    rank = routed[:, D_MODEL + RANK_LANE].astype(jnp.int32)
    dest = rank + jnp.sum(jnp.where(cls[:, None] == classes[None, :], starts[None, :], 0), axis=1)
    n_used = ends[-1] // MOE_TILE
    tile_end = ends // MOE_TILE
    tiles = jnp.arange(n_tiles, dtype=jnp.int32)
    tile_cls = jnp.minimum(jnp.sum(tiles[:, None] >= tile_end[None, :], axis=1), N_CLASSES - 1)
    pick = lambda table: jnp.sum(
        jnp.where(tile_cls[:, None] == classes[None, :], jnp.asarray(table, jnp.int32)[None, :], 0),
        axis=1).astype(jnp.int32)
    tile_ea = pick([e for e, _ in _CLASS_EXPERTS])
    tile_eb = pick([e for _, e in _CLASS_EXPERTS])
    clear_a = jnp.where(padded > 0, tile_end - 1, -1)
    tail = n_used + classes
    clear_b = jnp.where(tail < n_tiles, tail, -1)
    clear = jnp.concatenate([clear_a, clear_b]).astype(jnp.int32)
    return dest.astype(jnp.int32), tile_ea, tile_eb, jnp.reshape(n_used, (1,)).astype(jnp.int32), clear


def _row_copy(src, src_row, dst, dst_row, sem):
    return pltpu.make_async_copy(src.at[pl.ds(src_row, 1), :], dst.at[pl.ds(dst_row, 1), :], sem)


def _dispatch_kernel(clear_ref, dest_ref, src_ref, dst_hbm, zero_scr, sem, *, rows):
    step = pl.program_id(0)

    def clear_copy(j):
        start = pl.multiple_of(clear_ref[j] * MOE_TILE, MOE_TILE)
        return pltpu.make_async_copy(zero_scr, dst_hbm.at[pl.ds(start, MOE_TILE), :], sem)

    @pl.when(step == 0)
    def _():
        zero_scr[...] = jnp.zeros_like(zero_scr)
        for wait in (False, True):
            for j in range(2 * N_CLASSES):
                @pl.when(clear_ref[j] >= 0)
                def _():
                    clear_copy(j).wait() if wait else clear_copy(j).start()

    def issue(r, carry):
        _row_copy(src_ref, r, dst_hbm, dest_ref[r], sem).start()
        return carry

    def drain(r, carry):
        _row_copy(src_ref, 0, dst_hbm, 0, sem).wait()
        return carry

    lax.fori_loop(0, rows, issue, 0, unroll=8)
    lax.fori_loop(0, rows, drain, 0, unroll=8)


def _dispatch(routed, dest, clear, n_tiles, rows):
    t, width = routed.shape
    return pl.pallas_call(
        functools.partial(_dispatch_kernel, rows=rows),
        grid_spec=pltpu.PrefetchScalarGridSpec(
            num_scalar_prefetch=1,
            grid=(t // rows,),
            in_specs=[pl.BlockSpec((rows,), lambda i, clear: (i,), memory_space=pltpu.SMEM),
                      pl.BlockSpec((rows, width), lambda i, clear: (i, 0))],
            out_specs=pl.BlockSpec(memory_space=pl.ANY),
            scratch_shapes=[pltpu.VMEM((MOE_TILE, width), F32), pltpu.SemaphoreType.DMA(())]),
        out_shape=jax.ShapeDtypeStruct((n_tiles * MOE_TILE, width), F32),
        compiler_params=_compiler_params(1),
        name="moe_dispatch",
    )(clear, dest, routed)


def _expert_kernel(ea_ref, eb_ref, used_ref, xs_ref, gain_ref, wga_ref, wua_ref, wda_ref,
                   wgb_ref, wub_ref, wdb_ref, o_ref):
    i = pl.program_id(0)

    @pl.when(i < used_ref[0])
    def _():
        x1 = xs_ref[:, :D_MODEL]
        gates = xs_ref[:, D_MODEL:]
        h = _rms_norm(x1, gain_ref[...]).astype(BF16)
        lane = lax.broadcasted_iota(jnp.int32, gates.shape, 1)
        acc = x1
        for e_ref, wg_ref, wu_ref, wd_ref in ((ea_ref, wga_ref, wua_ref, wda_ref),
                                              (eb_ref, wgb_ref, wub_ref, wdb_ref)):
            gate = jnp.sum(jnp.where(lane == e_ref[i], gates, 0.0), axis=-1, keepdims=True)
            a = jnp.dot(h, wg_ref[...], preferred_element_type=F32)
            u = jnp.dot(h, wu_ref[...], preferred_element_type=F32)
            hid = (jax.nn.silu(a) * u * gate).astype(BF16)
            acc = acc + jnp.dot(hid, wd_ref[...], preferred_element_type=F32)
        o_ref[...] = acc

    @pl.when(i >= used_ref[0])
    def _():
        o_ref[...] = jnp.zeros_like(o_ref)


def _experts(xs, tile_ea, tile_eb, n_used, gain, w_e_gate, w_e_up, w_e_down):
    n_tiles = xs.shape[0] // MOE_TILE
    by_a = lambda r, c: pl.BlockSpec((None, r, c), lambda i, ea, eb, used: (ea[i], 0, 0))
    by_b = lambda r, c: pl.BlockSpec((None, r, c), lambda i, ea, eb, used: (eb[i], 0, 0))
    wg, wu, wd = (w.astype(BF16) for w in (w_e_gate, w_e_up, w_e_down))
    return pl.pallas_call(
        _expert_kernel,
        grid_spec=pltpu.PrefetchScalarGridSpec(
            num_scalar_prefetch=3,
            grid=(n_tiles,),
            in_specs=[pl.BlockSpec((MOE_TILE, D_ROUTED),
                                   lambda i, ea, eb, used: (jnp.minimum(i, used[0] - 1), 0)),
                      pl.BlockSpec((1, D_MODEL), lambda i, ea, eb, used: (0, 0)),
                      by_a(D_MODEL, D_EXPERT), by_a(D_MODEL, D_EXPERT), by_a(D_EXPERT, D_MODEL),
                      by_b(D_MODEL, D_EXPERT), by_b(D_MODEL, D_EXPERT), by_b(D_EXPERT, D_MODEL)],
            out_specs=pl.BlockSpec((MOE_TILE, D_MODEL), lambda i, ea, eb, used: (i, 0))),
        out_shape=jax.ShapeDtypeStruct((n_tiles * MOE_TILE, D_MODEL), F32),
        compiler_params=_compiler_params(1),
        name="moe_experts",
    )(tile_ea, tile_eb, n_used, xs, gain[None, :], wg, wu, wd, wg, wu, wd)


def _unsort_kernel(dest_ref, src_hbm, dst_ref, sem, *, rows):
    def issue(r, carry):
        _row_copy(src_hbm, dest_ref[r], dst_ref, r, sem).start()
        return carry

    def drain(r, carry):
        _row_copy(src_hbm, 0, dst_ref, 0, sem).wait()
        return carry

    lax.fori_loop(0, rows, issue, 0, unroll=8)
    lax.fori_loop(0, rows, drain, 0, unroll=8)


def _unsort(xo, dest, rows):
    t = dest.shape[0]
    return pl.pallas_call(
        functools.partial(_unsort_kernel, rows=rows),
        grid=(t // rows,),
        in_specs=[pl.BlockSpec((rows,), lambda i: (i,), memory_space=pltpu.SMEM),
                  pl.BlockSpec(memory_space=pl.ANY)],
        out_specs=pl.BlockSpec((rows, xo.shape[1]), lambda i: (i, 0)),
        out_shape=jax.ShapeDtypeStruct((t, xo.shape[1]), F32),
        scratch_shapes=[pltpu.SemaphoreType.DMA(())],
        compiler_params=_compiler_params(1),
        name="moe_unsort",
    )(dest, xo)


def _moe(routed, counts, ffn_gain, w_e_gate, w_e_up, w_e_down, rows):
    t = routed.shape[0]
    assert t % MOE_TILE == 0 and t % rows == 0
    n_tiles = t // MOE_TILE + N_CLASSES
    dest, tile_ea, tile_eb, n_used, clear = _moe_plan(routed, counts, n_tiles)
    xs = _dispatch(routed, dest, clear, n_tiles, rows)
    xo = _experts(xs, tile_ea, tile_eb, n_used, ffn_gain, w_e_gate, w_e_up, w_e_down)
    return _unsort(xo, dest, rows)


def kernel(x, mix_norm_gain, w_in, b_gate, q_gain, k_gain, rel_bias, ssm_lambda_re, ssm_lambda_im, ssm_log_step, ssm_b_re, ssm_b_im, ssm_c_re, ssm_c_im, ssm_d, w_glu, b_glu, w_branch, w_out, ffn_norm_gain, w_group_router, group_bias, w_expert_router, expert_bias, w_e_gate, w_e_up, w_e_down):
    bsz, seq, _ = x.shape
    assert seq % (SSM_KBLOCK * SSM_CHUNK) == 0 and seq % (2 * CHUNK) == 0
    depth = w_in.shape[0]
    tiles = _tiles(bsz * seq)
    x2 = x.reshape(bsz * seq, D_MODEL)
    for i in range(depth):
        w_qkv = w_in[i][:, :D_QKV]
        w_u = w_in[i][:, D_QKV:D_QKV + D_SSM]
        w_gate = w_in[i][:, D_QKV + D_SSM:]
        q, k, v = _inproj(x2, mix_norm_gain[i], w_qkv, q_gain[i], k_gain[i], tiles["inproj"])
        y_attn = _attention(q, k, v, _rel_bias_table(rel_bias[i]), bsz, seq)
        u_t = _ssm_inproj(x2, mix_norm_gain[i], w_u, bsz, seq)
        tables = _ssm_tables(ssm_lambda_re[i], ssm_lambda_im[i], ssm_log_step[i],
                             ssm_b_re[i], ssm_b_im[i], ssm_c_re[i], ssm_c_im[i], ssm_d[i])
        z = _ssm_out(_ssm(u_t, tables, bsz), bsz, seq)
        routed, counts = _merge(x2, y_attn, z, mix_norm_gain[i], w_gate, b_gate[i], w_glu[i],
                                b_glu[i], w_branch[i], w_out[i], ffn_norm_gain[i],
                                w_group_router[i], group_bias[i], w_expert_router[i],
                                expert_bias[i], tiles["merge"])
        x2 = _moe(routed, counts, ffn_norm_gain[i], w_e_gate[i], w_e_up[i], w_e_down[i],
                  tiles["moe_rows"])
    return x2.reshape(bsz, seq, D_MODEL)
```

```python
import functools

import jax
import jax.numpy as jnp
from jax import lax
from jax.experimental import pallas as pl
from jax.experimental.pallas import tpu as pltpu

F32 = jnp.float32
BF16 = jnp.bfloat16

D_MODEL = 1024
CHUNK = 64
LEFT_CHUNKS = 8
BAND = LEFT_CHUNKS + 1
BAND_KEYS = BAND * CHUNK
PAD_KEYS = LEFT_CHUNKS * CHUNK
ATTN_HEADS = 8
HEAD_DIM = 64
D_ATTN = ATTN_HEADS * HEAD_DIM
MAX_REL = 256
REL_FUTURE = CHUNK - 1
D_SSM = D_MODEL // 2
SSM_GROUP = 16
SSM_GROUPS = D_SSM // SSM_GROUP
SSM_STATE = 64
N_BRANCH = 2
D_QKV = 3 * D_ATTN
D_IN = D_QKV + D_SSM + N_BRANCH * D_MODEL
N_GROUPS = 4
EXPERTS_PER_GROUP = 4
N_EXPERTS = N_GROUPS * EXPERTS_PER_GROUP
D_EXPERT = D_MODEL // 4
EPS = 1e-6
NEG_INF = -1e30
LOG2_E = 1.4426950408889634

LANES = 128
HEAD_PAIRS = D_ATTN // LANES
SSM_CHUNK = 16
SSM_ROW = SSM_CHUNK * SSM_GROUP
SSM_KBLOCK = 16
SSM_BGROUP = LANES // SSM_KBLOCK
ROUTER_LANES = LANES
D_ROUTED = D_MODEL + ROUTER_LANES
CLASS_LANE = 32
RANK_LANE = 33
MOE_TILE = 256
_PAIRS = [(a, b) for a in range(EXPERTS_PER_GROUP) for b in range(a + 1, EXPERTS_PER_GROUP)]
PAIRS_PER_GROUP = len(_PAIRS)
N_CLASSES = N_GROUPS * PAIRS_PER_GROUP
_CLASS_EXPERTS = [(g * EXPERTS_PER_GROUP + a, g * EXPERTS_PER_GROUP + b)
                  for g in range(N_GROUPS) for a, b in _PAIRS]
VMEM_LIMIT = 56 * 1024 * 1024


def _tiles(n_tokens):
    return dict(inproj=512, merge=512, moe_rows=512)


def _compiler_params(n_axes):
    return pltpu.CompilerParams(
        dimension_semantics=("arbitrary",) * n_axes,
        vmem_limit_bytes=VMEM_LIMIT)


def _full(shape):
    nd = len(shape)
    return pl.BlockSpec(shape, lambda *_: (0,) * nd)


def _rms_norm(x, gain):
    ms = jnp.mean(x * x, axis=-1, keepdims=True)
    return x * lax.rsqrt(ms + EPS) * gain


NT_DIMS = (((1,), (1,)), ((), ()))


def _inproj_kernel(x_ref, gain_ref, w_ref, qg_ref, kg_ref, ones_ref, q_ref, k_ref, v_ref):
    h = _rms_norm(x_ref[...], gain_ref[...]).astype(BF16)

    def proj(lo, n):
        return jnp.dot(h, w_ref[:, lo:lo + n], preferred_element_type=F32)

    def head_norm(t, g_ref):
        ss = jnp.dot((t * t).astype(BF16), ones_ref[...], preferred_element_type=F32)
        return t * lax.rsqrt(ss * (1.0 / HEAD_DIM) + EPS) * g_ref[...]

    q_ref[...] = head_norm(proj(0, D_ATTN), qg_ref).astype(BF16)
    k_ref[...] = head_norm(proj(D_ATTN, D_ATTN), kg_ref).astype(BF16)
    v_ref[...] = proj(2 * D_ATTN, D_ATTN).astype(BF16)


def _inproj(x2, gain, w_qkv, q_gain, k_gain, tm):
    t = x2.shape[0]
    ones = jnp.kron(jnp.eye(ATTN_HEADS, dtype=F32), jnp.ones((HEAD_DIM, HEAD_DIM), F32)).astype(BF16)
    qg = jnp.tile(q_gain, ATTN_HEADS)[None, :] * (HEAD_DIM ** -0.5 * LOG2_E)
    kg = jnp.tile(k_gain, ATTN_HEADS)[None, :]
    row = lambda n: pl.BlockSpec((tm, n), lambda i: (i, 0))
    return pl.pallas_call(
        _inproj_kernel,
        grid=(t // tm,),
        in_specs=[row(D_MODEL), _full((1, D_MODEL)), _full((D_MODEL, D_QKV)),
                  _full((1, D_ATTN)), _full((1, D_ATTN)), _full((D_ATTN, D_ATTN))],
        out_specs=[row(D_ATTN)] * 3,
        out_shape=[jax.ShapeDtypeStruct((t, D_ATTN), BF16)] * 3,
        compiler_params=_compiler_params(1),
        name="inproj",
    )(x2, gain[None, :], w_qkv.astype(BF16), qg, kg, ones)


def _attn_kernel(q_ref, k_ref, v_ref, bias_ref, o_ref, kpad, v_even, v_odd, s_a, s_b):
    seq = q_ref.shape[0]
    n_chunks = seq // CHUNK
    lane_v = lax.broadcasted_iota(jnp.int32, (seq, D_ATTN), 1)
    even_v = (lane_v % LANES) < HEAD_DIM
    zeros = jnp.zeros((PAD_KEYS, D_ATTN), BF16)
    v = v_ref[...]
    kpad[0:PAD_KEYS, :] = zeros
    v_even[0:PAD_KEYS, :] = zeros
    v_odd[0:PAD_KEYS, :] = zeros
    kpad[PAD_KEYS:, :] = k_ref[...]
    v_even[PAD_KEYS:, :] = jnp.where(even_v, v, jnp.zeros_like(v))
    v_odd[PAD_KEYS:, :] = jnp.where(even_v, jnp.zeros_like(v), v)

    lane = lax.broadcasted_iota(jnp.int32, (CHUNK, LANES), 1)
    first_head = lane < HEAD_DIM
    key_col = lax.broadcasted_iota(jnp.int32, (2 * CHUNK, BAND_KEYS), 1)

    def scores(n, s_buf):
        r0 = pl.multiple_of(n * CHUNK, CHUNK)
        for j in range(HEAD_PAIRS):
            cols = slice(j * LANES, (j + 1) * LANES)
            qp = q_ref[pl.ds(r0, CHUNK), cols]
            zq = jnp.zeros_like(qp)
            q2 = jnp.concatenate([jnp.where(first_head, qp, zq), jnp.where(first_head, zq, qp)], axis=0)
            kb = kpad[pl.ds(r0, BAND_KEYS), cols]
            s_buf[j * 2 * CHUNK:(j + 1) * 2 * CHUNK, :] = lax.dot_general(
                q2, kb, NT_DIMS, preferred_element_type=F32)

    def softmax_pv(n, s_buf, masked):
        r0 = pl.multiple_of(n * CHUNK, CHUNK)
        for j in range(HEAD_PAIRS):
            cols = slice(j * LANES, (j + 1) * LANES)
            rows = slice(j * 2 * CHUNK, (j + 1) * 2 * CHUNK)
            s = s_buf[rows, :] + bias_ref[rows, :]
            if masked:
                s = jnp.where(key_col >= (LEFT_CHUNKS - n) * CHUNK, s, NEG_INF)
            m = jnp.max(s, axis=-1, keepdims=True)
            p = jnp.exp2(s - m)
            inv_l = 1.0 / jnp.sum(p, axis=-1, keepdims=True)
            pb = p.astype(BF16)
            o0 = jnp.dot(pb[:CHUNK], v_even[pl.ds(r0, BAND_KEYS), cols], preferred_element_type=F32)
            o1 = jnp.dot(pb[CHUNK:], v_odd[pl.ds(r0, BAND_KEYS), cols], preferred_element_type=F32)
            o_ref[pl.ds(r0, CHUNK), cols] = (o0 * inv_l[:CHUNK] + o1 * inv_l[CHUNK:]).astype(BF16)

    def pair_body(i, carry, *, masked):
        n0 = 2 * i
        scores(n0 + 1, s_b)
        softmax_pv(n0, s_a, masked)
        scores(jnp.minimum(n0 + 2, n_chunks - 1), s_a)
        softmax_pv(n0 + 1, s_b, masked)
        return carry

    assert n_chunks % 2 == 0 and LEFT_CHUNKS % 2 == 0
    masked_pairs = min(LEFT_CHUNKS, n_chunks) // 2
    scores(0, s_a)
    lax.fori_loop(0, masked_pairs, functools.partial(pair_body, masked=True), 0)
    lax.fori_loop(masked_pairs, n_chunks // 2, functools.partial(pair_body, masked=False), 0)


def _attention(q, k, v, bias, bsz, seq):
    q3, k3, v3 = (a.reshape(bsz, seq, D_ATTN) for a in (q, k, v))
    seq_spec = pl.BlockSpec((None, seq, D_ATTN), lambda b: (b, 0, 0))
    padded = pltpu.VMEM((PAD_KEYS + seq, D_ATTN), BF16)
    out = pl.pallas_call(
        _attn_kernel,
        grid=(bsz,),
        in_specs=[seq_spec, seq_spec, seq_spec, _full((ATTN_HEADS * CHUNK, BAND_KEYS))],
        out_specs=seq_spec,
        out_shape=jax.ShapeDtypeStruct((bsz, seq, D_ATTN), BF16),
        scratch_shapes=[padded, padded, padded,
                        pltpu.VMEM((ATTN_HEADS * CHUNK, BAND_KEYS), F32),
                        pltpu.VMEM((ATTN_HEADS * CHUNK, BAND_KEYS), F32)],
        compiler_params=_compiler_params(1),
        name="attention",
    )(q3, k3, v3, bias)
    return out.reshape(bsz * seq, D_ATTN)


def _rel_bias_table(rel_bias):
    n_rel = REL_FUTURE + MAX_REL + 1
    far = jnp.broadcast_to(rel_bias[:, n_rel - 1:], (ATTN_HEADS, PAD_KEYS - MAX_REL + CHUNK))
    ext = jnp.concatenate([far, rel_bias[:, n_rel - 2::-1]], axis=1) * LOG2_E
    rows = [ext[:, CHUNK - 1 - q:CHUNK - 1 - q + BAND_KEYS] for q in range(CHUNK)]
    return jnp.stack(rows, axis=1).reshape(ATTN_HEADS * CHUNK, BAND_KEYS)


SSM_BLOCK_LEN = SSM_KBLOCK * SSM_CHUNK
SSM_BLOCK_TOK = SSM_BGROUP * SSM_KBLOCK


def _token_block_spec(width):
    return pl.BlockSpec((SSM_BGROUP, SSM_BLOCK_LEN, width), lambda kb, bg: (bg, kb, 0))


def _transposed_spec(n_bgroups):
    return pl.BlockSpec((SSM_GROUPS, SSM_ROW, SSM_BLOCK_TOK),
                        lambda kb, bg: (0, 0, kb * n_bgroups + bg))


def _token_grid(bsz, seq):
    assert bsz % SSM_BGROUP == 0 and seq % SSM_BLOCK_LEN == 0
    return (seq // SSM_BLOCK_LEN, bsz // SSM_BGROUP)


def _ssm_inproj_kernel(x_ref, gain_ref, wu_ref, ut_ref, u_scr):
    n_rows = SSM_BGROUP * SSM_BLOCK_LEN
    x = x_ref[...].reshape(n_rows, D_MODEL)
    h = _rms_norm(x, gain_ref[...]).astype(BF16)
    u = jnp.dot(h, wu_ref[...], preferred_element_type=F32)
    for q in range(D_SSM // LANES):
        u_scr[q] = u[:, q * LANES:(q + 1) * LANES]
    groups_per_tile = LANES // SSM_GROUP
    for t in range(SSM_CHUNK):
        for q in range(D_SSM // LANES):
            ut = u_scr[q, pl.ds(t, SSM_BLOCK_TOK, stride=SSM_CHUNK), :].T.astype(BF16)
            for gl in range(groups_per_tile):
                ut_ref[q * groups_per_tile + gl, t * SSM_GROUP:(t + 1) * SSM_GROUP, :] = (
                    ut[gl * SSM_GROUP:(gl + 1) * SSM_GROUP, :])


def _ssm_inproj(x2, gain, w_u, bsz, seq):
    grid = _token_grid(bsz, seq)
    return pl.pallas_call(
        _ssm_inproj_kernel,
        grid=grid,
        in_specs=[_token_block_spec(D_MODEL), _full((1, D_MODEL)), _full((D_MODEL, D_SSM))],
        out_specs=_transposed_spec(grid[1]),
        out_shape=jax.ShapeDtypeStruct((SSM_GROUPS, SSM_ROW, bsz * seq // SSM_CHUNK), BF16),
        scratch_shapes=[pltpu.VMEM((D_SSM // LANES, SSM_BGROUP * SSM_BLOCK_LEN, LANES), F32)],
        compiler_params=_compiler_params(2),
        name="ssm_inproj",
    )(x2.reshape(bsz, seq, D_MODEL), gain[None, :], w_u.astype(BF16))


def _ssm_out_kernel(yt_ref, z_ref, y_scr):
    groups_per_tile = LANES // SSM_GROUP
    for t in range(SSM_CHUNK):
        for q in range(D_SSM // LANES):
            yt = jnp.concatenate(
                [yt_ref[q * groups_per_tile + gl, t * SSM_GROUP:(t + 1) * SSM_GROUP, :]
                 for gl in range(groups_per_tile)], axis=0)
            y_scr[q, pl.ds(t, SSM_BLOCK_TOK, stride=SSM_CHUNK), :] = yt.T
    for q in range(D_SSM // LANES):
        z = jax.nn.gelu(y_scr[q])
        z_ref[:, :, q * LANES:(q + 1) * LANES] = z.astype(BF16).reshape(
            SSM_BGROUP, SSM_BLOCK_LEN, LANES)


def _ssm_out(y_t, bsz, seq):
    grid = _token_grid(bsz, seq)
    z = pl.pallas_call(
        _ssm_out_kernel,
        grid=grid,
        in_specs=[_transposed_spec(grid[1])],
        out_specs=_token_block_spec(D_SSM),
        out_shape=jax.ShapeDtypeStruct((bsz, seq, D_SSM), BF16),
        scratch_shapes=[pltpu.VMEM((D_SSM // LANES, SSM_BGROUP * SSM_BLOCK_LEN, LANES), F32)],
        compiler_params=_compiler_params(2),
        name="ssm_out",
    )(y_t)
    return z.reshape(bsz * seq, D_SSM)


def _ssm_tables(lambda_re, lambda_im, log_step, b_re, b_im, c_re, c_im, d_skip):
    step = jnp.exp(log_step)[:, None]
    mag = jnp.exp(lambda_re * step)
    ang = lambda_im * step
    a_re = mag * jnp.cos(ang)
    a_im = mag * jnp.sin(ang)
    num_re = a_re - 1.0
    num_im = a_im
    den = lambda_re * lambda_re + lambda_im * lambda_im
    f_re = (num_re * lambda_re + num_im * lambda_im) / den
    f_im = (num_im * lambda_re - num_re * lambda_im) / den
    bb_re = f_re[..., None] * b_re - f_im[..., None] * b_im
    bb_im = f_re[..., None] * b_im + f_im[..., None] * b_re
    log_mag = lambda_re * step

    def powers(tau):
        tau = jnp.asarray(tau, F32)[:, None, None]
        pmag = jnp.exp(tau * log_mag[None])
        return pmag * jnp.cos(tau * ang[None]), pmag * jnp.sin(tau * ang[None])

    steps = jnp.arange(SSM_CHUNK)
    up_re, up_im = powers(steps + 1)
    cp_re = c_re[None] * up_re[:, :, None, :] - c_im[None] * up_im[:, :, None, :]
    cp_im = c_re[None] * up_im[:, :, None, :] + c_im[None] * up_re[:, :, None, :]
    readout = jnp.concatenate([cp_re, -cp_im], axis=-1)
    readout = jnp.transpose(readout, (1, 0, 2, 3)).reshape(SSM_GROUPS, SSM_ROW, 2 * SSM_STATE)

    def inject(tau):
        p_re, p_im = powers(tau)
        bt_re = jnp.transpose(bb_re, (0, 2, 1))[None]
        bt_im = jnp.transpose(bb_im, (0, 2, 1))[None]
        i_re = p_re[:, :, None, :] * bt_re - p_im[:, :, None, :] * bt_im
        i_im = p_re[:, :, None, :] * bt_im + p_im[:, :, None, :] * bt_re
        both = jnp.concatenate([i_re, i_im], axis=-1)
        return jnp.transpose(both, (1, 3, 0, 2)).reshape(SSM_GROUPS, 2 * SSM_STATE, SSM_ROW)

    inject_back = inject(-(steps + 1.0))
    inject_end = inject(SSM_CHUNK - 1.0 - steps).astype(BF16)
    d_row = jnp.tile(d_skip.reshape(SSM_GROUPS, 1, SSM_GROUP), (1, 1, SSM_CHUNK))
    (ar,), (ai,) = powers([float(SSM_CHUNK)])
    a_chunk = jnp.stack([jnp.concatenate([ar, ar], -1), jnp.concatenate([-ai, ai], -1),
                         jnp.concatenate([ai, -ai], -1)], axis=1)
    return readout, inject_back, inject_end, d_row, a_chunk


def _ssm_kernel(ut_ref, ro_ref, ib_ref, ie_ref, d_ref, a_ref, yt_ref, p_scr, q_scr, s_scr, *, bsz):
    ut = ut_ref[...]
    n_tok = ut.shape[1]
    r_hi, r_lo = _split_bf16(ro_ref[...])
    i_hi, i_lo = _split_bf16(ib_ref[...])
    kern = (jnp.dot(r_hi, i_hi, preferred_element_type=F32)
            + jnp.dot(r_hi, i_lo, preferred_element_type=F32)
            + jnp.dot(r_lo, i_hi, preferred_element_type=F32))
    row = lax.broadcasted_iota(jnp.int32, kern.shape, 0)
    col = lax.broadcasted_iota(jnp.int32, kern.shape, 1)
    group_shift = SSM_GROUP.bit_length() - 1
    causal = jnp.right_shift(row, group_shift) >= jnp.right_shift(col, group_shift)
    kern = jnp.where(causal, kern, 0.0) + jnp.where(row == col, d_ref[...], 0.0)
    wt = jnp.dot(ie_ref[...], ut, preferred_element_type=F32)
    w = wt.T
    p_scr[...] = w
    q_scr[...] = pltpu.roll(w, SSM_STATE, 1)
    a1 = a_ref[0:1, :]
    a2 = a_ref[1:2, :]
    a3 = a_ref[2:3, :]

    def block(kb, carry):
        cp, cq = carry
        base = pl.multiple_of(kb * (bsz * SSM_KBLOCK), bsz * SSM_KBLOCK)
        for k in range(SSM_KBLOCK):
            rows = pl.ds(base + k, bsz, stride=SSM_KBLOCK)
            s_scr[rows, :] = cp
            cp, cq = (a1 * cp + a2 * cq + p_scr[rows, :],
                      a1 * cq + a3 * cp + q_scr[rows, :])
        return cp, cq

    zero = jnp.zeros((bsz, 2 * SSM_STATE), F32)
    lax.fori_loop(0, n_tok // (bsz * SSM_KBLOCK), block, (zero, zero))
    yt = jnp.dot(kern.astype(BF16), ut, preferred_element_type=F32)
    yt = yt + lax.dot_general(r_hi, s_scr[...].astype(BF16), NT_DIMS, preferred_element_type=F32)
    yt_ref[...] = yt


def _ssm(ut, tables, bsz):
    readout, inject_back, inject_end, d_row, a_chunk = tables
    n_tok = ut.shape[2]
    per_group = lambda a, b: pl.BlockSpec((None, a, b), lambda g: (g, 0, 0))
    state = pltpu.VMEM((n_tok, 2 * SSM_STATE), F32)
    return pl.pallas_call(
        functools.partial(_ssm_kernel, bsz=bsz),
        grid=(SSM_GROUPS,),
        in_specs=[per_group(SSM_ROW, n_tok), per_group(SSM_ROW, 2 * SSM_STATE),
                  per_group(2 * SSM_STATE, SSM_ROW), per_group(2 * SSM_STATE, SSM_ROW),
                  per_group(1, SSM_ROW), per_group(3, 2 * SSM_STATE)],
        out_specs=per_group(SSM_ROW, n_tok),
        out_shape=jax.ShapeDtypeStruct((SSM_GROUPS, SSM_ROW, n_tok), F32),
        scratch_shapes=[state, state, state],
        compiler_params=_compiler_params(1),
        name="ssm",
    )(ut, readout, inject_back, inject_end, d_row, a_chunk)


def _split_bf16(a):
    hi = a.astype(BF16)
    lo = (a - hi.astype(F32)).astype(BF16)
    return hi, lo


def _route(logits):
    lane = lax.broadcasted_iota(jnp.int32, logits.shape, 1)
    lane_f = lane.astype(F32)
    big = float(ROUTER_LANES)

    def first_argmax(vals, vmax):
        return jnp.min(jnp.where(vals == vmax, lane_f, big), axis=-1, keepdims=True)

    g_mask = (lane >= N_EXPERTS) & (lane < N_EXPERTS + N_GROUPS)
    g_log = jnp.where(g_mask, logits, NEG_INF)
    g_max = jnp.max(g_log, axis=-1, keepdims=True)
    g_sum = jnp.sum(jnp.where(g_mask, jnp.exp(g_log - g_max), 0.0), axis=-1, keepdims=True)
    g_prob = 1.0 / g_sum
    g_idx = first_argmax(g_log, g_max) - float(N_EXPERTS)
    e_lo = g_idx * float(EXPERTS_PER_GROUP)
    e_mask = (lane_f >= e_lo) & (lane_f < e_lo + float(EXPERTS_PER_GROUP))
    e_log = jnp.where(e_mask, logits, NEG_INF)
    e_max1 = jnp.max(e_log, axis=-1, keepdims=True)
    idx1 = first_argmax(e_log, e_max1)
    e_log2 = jnp.where(lane_f == idx1, NEG_INF, e_log)
    e_max2 = jnp.max(e_log2, axis=-1, keepdims=True)
    idx2 = first_argmax(e_log2, e_max2)
    r = jnp.exp(e_max2 - e_max1)
    w1 = g_prob / (1.0 + r)
    w2 = g_prob * r / (1.0 + r)
    gates = jnp.where(lane_f == idx1, w1, 0.0) + jnp.where(lane_f == idx2, w2, 0.0)
    lo = jnp.minimum(idx1, idx2) - e_lo
    hi = jnp.maximum(idx1, idx2) - e_lo
    pair = lo * (2.0 * EXPERTS_PER_GROUP - 1.0 - lo) * 0.5 + (hi - lo - 1.0)
    return gates, g_idx * float(PAIRS_PER_GROUP) + pair


def _merge_kernel(x_ref, ya_ref, z_ref, gain1_ref, wgate_ref, bgate_ref, wglu_ref, bglu_ref,
                  wba_ref, wbs_ref, wout_ref, gain2_ref, rhi_ref, rlo_ref, rbias_ref, ltri_ref,
                  xg_ref, counts_ref, count_scr):
    x = x_ref[...]
    hn = _rms_norm(x, gain1_ref[...]).astype(BF16)
    z = z_ref[...]
    glu = jnp.dot(z, wglu_ref[...], preferred_element_type=F32) + bglu_ref[...]
    y_ssm = (z.astype(F32) * jax.nn.sigmoid(glu)).astype(BF16)
    merged = None
    for s, (y_b, wb_ref) in enumerate(((ya_ref[...], wba_ref), (y_ssm, wbs_ref))):
        cols = slice(s * D_MODEL, (s + 1) * D_MODEL)
        gate = jax.nn.sigmoid(jnp.dot(hn, wgate_ref[:, cols], preferred_element_type=F32)
                              + bgate_ref[:, cols])
        term = gate * jnp.dot(y_b, wb_ref[...], preferred_element_type=F32)
        merged = term if merged is None else merged + term
    x1 = x + jnp.dot(merged.astype(BF16), wout_ref[...], preferred_element_type=F32)
    xg_ref[:, :D_MODEL] = x1

    h = _rms_norm(x1, gain2_ref[...])
    h_hi, h_lo = _split_bf16(h)
    logits = (jnp.dot(h_hi, rhi_ref[...], preferred_element_type=F32)
              + jnp.dot(h_hi, rlo_ref[...], preferred_element_type=F32)
              + jnp.dot(h_lo, rhi_ref[...], preferred_element_type=F32)
              + rbias_ref[...])
    gates, cls = _route(logits)

    @pl.when(pl.program_id(0) == 0)
    def _():
        count_scr[...] = jnp.zeros_like(count_scr)

    lane_f = lax.broadcasted_iota(jnp.int32, gates.shape, 1).astype(F32)
    onehot = jnp.where(lane_f == cls, 1.0, 0.0)
    before = jnp.dot(ltri_ref[...], onehot.astype(BF16), preferred_element_type=F32)
    seen = count_scr[...]
    rank = jnp.sum(onehot * (before + seen), axis=-1, keepdims=True)
    count_scr[...] = seen + jnp.sum(onehot, axis=0, keepdims=True)
    counts_ref[...] = count_scr[...]
    xg_ref[:, D_MODEL:] = (gates + jnp.where(lane_f == float(CLASS_LANE), cls, 0.0)
                           + jnp.where(lane_f == float(RANK_LANE), rank, 0.0))


def _merge(x2, y_attn, z, gain1, w_gate, b_gate, w_glu, b_glu, w_branch, w_out, ffn_gain,
           w_group_router, group_bias, w_expert_router, expert_bias, tm):
    t = x2.shape[0]
    pad = ROUTER_LANES - N_EXPERTS - N_GROUPS
    w_r = jnp.concatenate([w_expert_router, w_group_router,
                           jnp.zeros((D_MODEL, pad), F32)], axis=1)
    r_hi, r_lo = _split_bf16(w_r)
    r_bias = jnp.concatenate([expert_bias, group_bias, jnp.zeros((pad,), F32)])[None, :]
    ltri = jnp.tril(jnp.ones((tm, tm), F32), -1).astype(BF16)
    row = lambda n: pl.BlockSpec((tm, n), lambda i: (i, 0))
    resident = lambda shape: pl.BlockSpec(shape, lambda i: (0, 0), pipeline_mode=pl.Buffered(1))
    return pl.pallas_call(
        _merge_kernel,
        grid=(t // tm,),
        in_specs=[row(D_MODEL), row(D_ATTN), row(D_SSM),
                  resident((1, D_MODEL)), resident((D_MODEL, N_BRANCH * D_MODEL)),
                  resident((1, N_BRANCH * D_MODEL)),
                  resident((D_SSM, D_SSM)), resident((1, D_SSM)),
                  resident((D_ATTN, D_MODEL)), resident((D_SSM, D_MODEL)),
                  resident((D_MODEL, D_MODEL)), resident((1, D_MODEL)),
                  resident((D_MODEL, ROUTER_LANES)), resident((D_MODEL, ROUTER_LANES)),
                  resident((1, ROUTER_LANES)), resident((tm, tm))],
        out_specs=[row(D_ROUTED), pl.BlockSpec((1, ROUTER_LANES), lambda i: (0, 0))],
        out_shape=[jax.ShapeDtypeStruct((t, D_ROUTED), F32),
                   jax.ShapeDtypeStruct((1, ROUTER_LANES), F32)],
        scratch_shapes=[pltpu.VMEM((1, ROUTER_LANES), F32)],
        compiler_params=_compiler_params(1),
        name="merge",
    )(x2, y_attn, z, gain1[None, :], w_gate.astype(BF16), b_gate[None, :],
      w_glu.astype(BF16), b_glu[None, :],
      w_branch[:D_ATTN].astype(BF16), w_branch[D_ATTN:].astype(BF16),
      w_out.astype(BF16), ffn_gain[None, :], r_hi, r_lo, r_bias, ltri)


def _moe_plan(routed, counts, n_tiles):
    cnt = counts[0, :N_CLASSES].astype(jnp.int32)
    padded = (cnt + (MOE_TILE - 1)) // MOE_TILE * MOE_TILE
    ends = jnp.cumsum(padded)
    starts = ends - padded
    classes = jnp.arange(N_CLASSES, dtype=jnp.int32)
    cls = routed[:, D_MODEL + CLASS_LANE].astype(jnp.int32)
    rank = routed[:, D_MODEL + RANK_LANE].astype(jnp.int32)
    dest = rank + jnp.sum(jnp.where(cls[:, None] == classes[None, :], starts[None, :], 0), axis=1)
    n_used = ends[-1] // MOE_TILE
    tile_end = ends // MOE_TILE
    tiles = jnp.arange(n_tiles, dtype=jnp.int32)
    tile_cls = jnp.minimum(jnp.sum(tiles[:, None] >= tile_end[None, :], axis=1), N_CLASSES - 1)
    pick = lambda table: jnp.sum(
        jnp.where(tile_cls[:, None] == classes[None, :], jnp.asarray(table, jnp.int32)[None, :], 0),
        axis=1).astype(jnp.int32)
    tile_ea = pick([e for e, _ in _CLASS_EXPERTS])
    tile_eb = pick([e for _, e in _CLASS_EXPERTS])
    clear_a = jnp.where(padded > 0, tile_end - 1, -1)
    tail = n_used + classes
    clear_b = jnp.where(tail < n_tiles, tail, -1)
    clear = jnp.concatenate([clear_a, clear_b]).astype(jnp.int32)
    return dest.astype(jnp.int32), tile_ea, tile_eb, jnp.reshape(n_used, (1,)).astype(jnp.int32), clear


def _row_copy(src, src_row, dst, dst_row, sem):
    return pltpu.make_async_copy(src.at[pl.ds(src_row, 1), :], dst.at[pl.ds(dst_row, 1), :], sem)


def _dispatch_kernel(clear_ref, dest_ref, src_ref, dst_hbm, zero_scr, sem, *, rows):
    step = pl.program_id(0)

    def clear_copy(j):
        start = pl.multiple_of(clear_ref[j] * MOE_TILE, MOE_TILE)
        return pltpu.make_async_copy(zero_scr, dst_hbm.at[pl.ds(start, MOE_TILE), :], sem)

    @pl.when(step == 0)
    def _():
        zero_scr[...] = jnp.zeros_like(zero_scr)
        for wait in (False, True):
            for j in range(2 * N_CLASSES):
                @pl.when(clear_ref[j] >= 0)
                def _():
                    clear_copy(j).wait() if wait else clear_copy(j).start()

    for r in range(rows):
        _row_copy(src_ref, r, dst_hbm, dest_ref[r], sem).start()
    for r in range(rows):
        _row_copy(src_ref, r, dst_hbm, 0, sem).wait()


def _dispatch(routed, dest, clear, n_tiles, rows):
    t, width = routed.shape
    return pl.pallas_call(
        functools.partial(_dispatch_kernel, rows=rows),
        grid_spec=pltpu.PrefetchScalarGridSpec(
            num_scalar_prefetch=1,
            grid=(t // rows,),
            in_specs=[pl.BlockSpec((rows,), lambda i, clear: (i,), memory_space=pltpu.SMEM),
                      pl.BlockSpec((rows, width), lambda i, clear: (i, 0))],
            out_specs=pl.BlockSpec(memory_space=pl.ANY),
            scratch_shapes=[pltpu.VMEM((MOE_TILE, width), F32), pltpu.SemaphoreType.DMA(())]),
        out_shape=jax.ShapeDtypeStruct((n_tiles * MOE_TILE, width), F32),
        compiler_params=_compiler_params(1),
        name="moe_dispatch",
    )(clear, dest, routed)


def _expert_kernel(ea_ref, eb_ref, used_ref, xs_ref, gain_ref, wga_ref, wua_ref, wda_ref,
                   wgb_ref, wub_ref, wdb_ref, o_ref):
    i = pl.program_id(0)

    @pl.when(i < used_ref[0])
    def _():
        x1 = xs_ref[:, :D_MODEL]
        gates = xs_ref[:, D_MODEL:]
        h = _rms_norm(x1, gain_ref[...]).astype(BF16)
        lane = lax.broadcasted_iota(jnp.int32, gates.shape, 1)
        acc = x1
        for e_ref, wg_ref, wu_ref, wd_ref in ((ea_ref, wga_ref, wua_ref, wda_ref),
                                              (eb_ref, wgb_ref, wub_ref, wdb_ref)):
            gate = jnp.sum(jnp.where(lane == e_ref[i], gates, 0.0), axis=-1, keepdims=True)
            a = jnp.dot(h, wg_ref[...], preferred_element_type=F32)
            u = jnp.dot(h, wu_ref[...], preferred_element_type=F32)
            hid = (jax.nn.silu(a) * u * gate).astype(BF16)
            acc = acc + jnp.dot(hid, wd_ref[...], preferred_element_type=F32)
        o_ref[...] = acc

    @pl.when(i >= used_ref[0])
    def _():
        o_ref[...] = jnp.zeros_like(o_ref)


def _experts(xs, tile_ea, tile_eb, n_used, gain, w_e_gate, w_e_up, w_e_down):
    n_tiles = xs.shape[0] // MOE_TILE
    by_a = lambda r, c: pl.BlockSpec((None, r, c), lambda i, ea, eb, used: (ea[i], 0, 0))
    by_b = lambda r, c: pl.BlockSpec((None, r, c), lambda i, ea, eb, used: (eb[i], 0, 0))
    wg, wu, wd = (w.astype(BF16) for w in (w_e_gate, w_e_up, w_e_down))
    return pl.pallas_call(
        _expert_kernel,
        grid_spec=pltpu.PrefetchScalarGridSpec(
            num_scalar_prefetch=3,
            grid=(n_tiles,),
            in_specs=[pl.BlockSpec((MOE_TILE, D_ROUTED),
                                   lambda i, ea, eb, used: (jnp.minimum(i, used[0] - 1), 0)),
                      pl.BlockSpec((1, D_MODEL), lambda i, ea, eb, used: (0, 0)),
                      by_a(D_MODEL, D_EXPERT), by_a(D_MODEL, D_EXPERT), by_a(D_EXPERT, D_MODEL),
                      by_b(D_MODEL, D_EXPERT), by_b(D_MODEL, D_EXPERT), by_b(D_EXPERT, D_MODEL)],
            out_specs=pl.BlockSpec((MOE_TILE, D_MODEL), lambda i, ea, eb, used: (i, 0))),
        out_shape=jax.ShapeDtypeStruct((n_tiles * MOE_TILE, D_MODEL), F32),
        compiler_params=_compiler_params(1),
        name="moe_experts",
    )(tile_ea, tile_eb, n_used, xs, gain[None, :], wg, wu, wd, wg, wu, wd)


def _unsort_kernel(dest_ref, src_hbm, dst_ref, sem, *, rows):
    for r in range(rows):
        _row_copy(src_hbm, dest_ref[r], dst_ref, r, sem).start()
    for r in range(rows):
        _row_copy(src_hbm, 0, dst_ref, r, sem).wait()


def _unsort(xo, dest, rows):
    t = dest.shape[0]
    return pl.pallas_call(
        functools.partial(_unsort_kernel, rows=rows),
        grid=(t // rows,),
        in_specs=[pl.BlockSpec((rows,), lambda i: (i,), memory_space=pltpu.SMEM),
                  pl.BlockSpec(memory_space=pl.ANY)],
        out_specs=pl.BlockSpec((rows, xo.shape[1]), lambda i: (i, 0)),
        out_shape=jax.ShapeDtypeStruct((t, xo.shape[1]), F32),
        scratch_shapes=[pltpu.SemaphoreType.DMA(())],
        compiler_params=_compiler_params(1),
        name="moe_unsort",
    )(dest, xo)


def _moe(routed, counts, ffn_gain, w_e_gate, w_e_up, w_e_down, rows):
    t = routed.shape[0]
    assert t % MOE_TILE == 0 and t % rows == 0
    n_tiles = t // MOE_TILE + N_CLASSES
    dest, tile_ea, tile_eb, n_used, clear = _moe_plan(routed, counts, n_tiles)
    xs = _dispatch(routed, dest, clear, n_tiles, rows)
    xo = _experts(xs, tile_ea, tile_eb, n_used, ffn_gain, w_e_gate, w_e_up, w_e_down)
    return _unsort(xo, dest, rows)


def kernel(x, mix_norm_gain, w_in, b_gate, q_gain, k_gain, rel_bias, ssm_lambda_re, ssm_lambda_im, ssm_log_step, ssm_b_re, ssm_b_im, ssm_c_re, ssm_c_im, ssm_d, w_glu, b_glu, w_branch, w_out, ffn_norm_gain, w_group_router, group_bias, w_expert_router, expert_bias, w_e_gate, w_e_up, w_e_down):
    bsz, seq, _ = x.shape
    assert seq % (SSM_KBLOCK * SSM_CHUNK) == 0 and seq % (2 * CHUNK) == 0
    depth = w_in.shape[0]
    tiles = _tiles(bsz * seq)
    x2 = x.reshape(bsz * seq, D_MODEL)
    for i in range(depth):
        w_qkv = w_in[i][:, :D_QKV]
        w_u = w_in[i][:, D_QKV:D_QKV + D_SSM]
        w_gate = w_in[i][:, D_QKV + D_SSM:]
        q, k, v = _inproj(x2, mix_norm_gain[i], w_qkv, q_gain[i], k_gain[i], tiles["inproj"])
        y_attn = _attention(q, k, v, _rel_bias_table(rel_bias[i]), bsz, seq)
        u_t = _ssm_inproj(x2, mix_norm_gain[i], w_u, bsz, seq)
        tables = _ssm_tables(ssm_lambda_re[i], ssm_lambda_im[i], ssm_log_step[i],
                             ssm_b_re[i], ssm_b_im[i], ssm_c_re[i], ssm_c_im[i], ssm_d[i])
        z = _ssm_out(_ssm(u_t, tables, bsz), bsz, seq)
        routed, counts = _merge(x2, y_attn, z, mix_norm_gain[i], w_gate, b_gate[i], w_glu[i],
                                b_glu[i], w_branch[i], w_out[i], ffn_norm_gain[i],
                                w_group_router[i], group_bias[i], w_expert_router[i],
                                expert_bias[i], tiles["merge"])
        x2 = _moe(routed, counts, ffn_norm_gain[i], w_e_gate[i], w_e_up[i], w_e_down[i],
                  tiles["moe_rows"])
    return x2.reshape(bsz, seq, D_MODEL)
```

```python
import functools

import jax
import jax.numpy as jnp
from jax import lax
from jax.experimental import pallas as pl
from jax.experimental.pallas import tpu as pltpu

F32 = jnp.float32
BF16 = jnp.bfloat16

D_MODEL = 1024
CHUNK = 64
LEFT_CHUNKS = 8
BAND = LEFT_CHUNKS + 1
BAND_KEYS = BAND * CHUNK
PAD_KEYS = LEFT_CHUNKS * CHUNK
ATTN_HEADS = 8
HEAD_DIM = 64
D_ATTN = ATTN_HEADS * HEAD_DIM
MAX_REL = 256
REL_FUTURE = CHUNK - 1
D_SSM = D_MODEL // 2
SSM_GROUP = 16
SSM_GROUPS = D_SSM // SSM_GROUP
SSM_STATE = 64
N_BRANCH = 2
D_QKV = 3 * D_ATTN
D_IN = D_QKV + D_SSM + N_BRANCH * D_MODEL
N_GROUPS = 4
EXPERTS_PER_GROUP = 4
N_EXPERTS = N_GROUPS * EXPERTS_PER_GROUP
D_EXPERT = D_MODEL // 4
EPS = 1e-6
NEG_INF = -1e30
LOG2_E = 1.4426950408889634

LANES = 128
HEAD_PAIRS = D_ATTN // LANES
SSM_CHUNK = 16
SSM_ROW = SSM_CHUNK * SSM_GROUP
SSM_KBLOCK = 16
SSM_BGROUP = LANES // SSM_KBLOCK
ROUTER_LANES = LANES
D_ROUTED = D_MODEL + ROUTER_LANES
CLASS_LANE = 32
RANK_LANE = 33
MOE_TILE = 256
DMA_THREADS = 2
_PAIRS = [(a, b) for a in range(EXPERTS_PER_GROUP) for b in range(a + 1, EXPERTS_PER_GROUP)]
PAIRS_PER_GROUP = len(_PAIRS)
N_CLASSES = N_GROUPS * PAIRS_PER_GROUP
_CLASS_EXPERTS = [(g * EXPERTS_PER_GROUP + a, g * EXPERTS_PER_GROUP + b)
                  for g in range(N_GROUPS) for a, b in _PAIRS]
VMEM_LIMIT = 56 * 1024 * 1024


def _tiles(n_tokens):
    return dict(inproj=512, merge=512, moe_rows=512)


def _compiler_params(n_axes):
    return pltpu.CompilerParams(
        dimension_semantics=("arbitrary",) * n_axes,
        vmem_limit_bytes=VMEM_LIMIT)


def _full(shape):
    nd = len(shape)
    return pl.BlockSpec(shape, lambda *_: (0,) * nd)


def _rms_norm(x, gain):
    ms = jnp.mean(x * x, axis=-1, keepdims=True)
    return x * lax.rsqrt(ms + EPS) * gain


NT_DIMS = (((1,), (1,)), ((), ()))


def _inproj_kernel(x_ref, gain_ref, w_ref, qg_ref, kg_ref, ones_ref, q_ref, k_ref, v_ref):
    h = _rms_norm(x_ref[...], gain_ref[...]).astype(BF16)

    def proj(lo, n):
        return jnp.dot(h, w_ref[:, lo:lo + n], preferred_element_type=F32)

    def head_norm(t, g_ref):
        ss = jnp.dot((t * t).astype(BF16), ones_ref[...], preferred_element_type=F32)
        return t * lax.rsqrt(ss * (1.0 / HEAD_DIM) + EPS) * g_ref[...]

    q_ref[...] = head_norm(proj(0, D_ATTN), qg_ref).astype(BF16)
    k_ref[...] = head_norm(proj(D_ATTN, D_ATTN), kg_ref).astype(BF16)
    v_ref[...] = proj(2 * D_ATTN, D_ATTN).astype(BF16)


def _inproj(x2, gain, w_qkv, q_gain, k_gain, tm):
    t = x2.shape[0]
    ones = jnp.kron(jnp.eye(ATTN_HEADS, dtype=F32), jnp.ones((HEAD_DIM, HEAD_DIM), F32)).astype(BF16)
    qg = jnp.tile(q_gain, ATTN_HEADS)[None, :] * (HEAD_DIM ** -0.5 * LOG2_E)
    kg = jnp.tile(k_gain, ATTN_HEADS)[None, :]
    row = lambda n: pl.BlockSpec((tm, n), lambda i: (i, 0))
    return pl.pallas_call(
        _inproj_kernel,
        grid=(t // tm,),
        in_specs=[row(D_MODEL), _full((1, D_MODEL)), _full((D_MODEL, D_QKV)),
                  _full((1, D_ATTN)), _full((1, D_ATTN)), _full((D_ATTN, D_ATTN))],
        out_specs=[row(D_ATTN)] * 3,
        out_shape=[jax.ShapeDtypeStruct((t, D_ATTN), BF16)] * 3,
        compiler_params=_compiler_params(1),
        name="inproj",
    )(x2, gain[None, :], w_qkv.astype(BF16), qg, kg, ones)


def _attn_kernel(q_ref, k_ref, v_ref, bias_ref, o_ref, kpad, v_even, v_odd, s_a, s_b):
    seq = q_ref.shape[0]
    n_chunks = seq // CHUNK
    lane_v = lax.broadcasted_iota(jnp.int32, (seq, D_ATTN), 1)
    even_v = (lane_v % LANES) < HEAD_DIM
    zeros = jnp.zeros((PAD_KEYS, D_ATTN), BF16)
    v = v_ref[...]
    kpad[0:PAD_KEYS, :] = zeros
    v_even[0:PAD_KEYS, :] = zeros
    v_odd[0:PAD_KEYS, :] = zeros
    kpad[PAD_KEYS:, :] = k_ref[...]
    v_even[PAD_KEYS:, :] = jnp.where(even_v, v, jnp.zeros_like(v))
    v_odd[PAD_KEYS:, :] = jnp.where(even_v, jnp.zeros_like(v), v)

    lane = lax.broadcasted_iota(jnp.int32, (CHUNK, LANES), 1)
    first_head = lane < HEAD_DIM
    key_col = lax.broadcasted_iota(jnp.int32, (2 * CHUNK, BAND_KEYS), 1)

    def scores(n, s_buf):
        r0 = pl.multiple_of(n * CHUNK, CHUNK)
        for j in range(HEAD_PAIRS):
            cols = slice(j * LANES, (j + 1) * LANES)
            qp = q_ref[pl.ds(r0, CHUNK), cols]
            zq = jnp.zeros_like(qp)
            q2 = jnp.concatenate([jnp.where(first_head, qp, zq), jnp.where(first_head, zq, qp)], axis=0)
            kb = kpad[pl.ds(r0, BAND_KEYS), cols]
            s_buf[j * 2 * CHUNK:(j + 1) * 2 * CHUNK, :] = lax.dot_general(
                q2, kb, NT_DIMS, preferred_element_type=F32)

    def softmax_pv(n, s_buf, masked):
        r0 = pl.multiple_of(n * CHUNK, CHUNK)
        for j in range(HEAD_PAIRS):
            cols = slice(j * LANES, (j + 1) * LANES)
            rows = slice(j * 2 * CHUNK, (j + 1) * 2 * CHUNK)
            s = s_buf[rows, :] + bias_ref[rows, :]
            if masked:
                s = jnp.where(key_col >= (LEFT_CHUNKS - n) * CHUNK, s, NEG_INF)
            m = jnp.max(s, axis=-1, keepdims=True)
            p = jnp.exp2(s - m)
            inv_l = 1.0 / jnp.sum(p, axis=-1, keepdims=True)
            pb = p.astype(BF16)
            o0 = jnp.dot(pb[:CHUNK], v_even[pl.ds(r0, BAND_KEYS), cols], preferred_element_type=F32)
            o1 = jnp.dot(pb[CHUNK:], v_odd[pl.ds(r0, BAND_KEYS), cols], preferred_element_type=F32)
            o_ref[pl.ds(r0, CHUNK), cols] = (o0 * inv_l[:CHUNK] + o1 * inv_l[CHUNK:]).astype(BF16)

    def pair_body(i, carry, *, masked):
        n0 = 2 * i
        scores(n0 + 1, s_b)
        softmax_pv(n0, s_a, masked)
        scores(jnp.minimum(n0 + 2, n_chunks - 1), s_a)
        softmax_pv(n0 + 1, s_b, masked)
        return carry

    assert n_chunks % 2 == 0 and LEFT_CHUNKS % 2 == 0
    masked_pairs = min(LEFT_CHUNKS, n_chunks) // 2
    scores(0, s_a)
    lax.fori_loop(0, masked_pairs, functools.partial(pair_body, masked=True), 0)
    lax.fori_loop(masked_pairs, n_chunks // 2, functools.partial(pair_body, masked=False), 0)


def _attention(q, k, v, bias, bsz, seq):
    q3, k3, v3 = (a.reshape(bsz, seq, D_ATTN) for a in (q, k, v))
    seq_spec = pl.BlockSpec((None, seq, D_ATTN), lambda b: (b, 0, 0))
    padded = pltpu.VMEM((PAD_KEYS + seq, D_ATTN), BF16)
    out = pl.pallas_call(
        _attn_kernel,
        grid=(bsz,),
        in_specs=[seq_spec, seq_spec, seq_spec, _full((ATTN_HEADS * CHUNK, BAND_KEYS))],
        out_specs=seq_spec,
        out_shape=jax.ShapeDtypeStruct((bsz, seq, D_ATTN), BF16),
        scratch_shapes=[padded, padded, padded,
                        pltpu.VMEM((ATTN_HEADS * CHUNK, BAND_KEYS), F32),
                        pltpu.VMEM((ATTN_HEADS * CHUNK, BAND_KEYS), F32)],
        compiler_params=_compiler_params(1),
        name="attention",
    )(q3, k3, v3, bias)
    return out.reshape(bsz * seq, D_ATTN)


def _rel_bias_table(rel_bias):
    n_rel = REL_FUTURE + MAX_REL + 1
    far = jnp.broadcast_to(rel_bias[:, n_rel - 1:], (ATTN_HEADS, PAD_KEYS - MAX_REL + CHUNK))
    ext = jnp.concatenate([far, rel_bias[:, n_rel - 2::-1]], axis=1) * LOG2_E
    rows = [ext[:, CHUNK - 1 - q:CHUNK - 1 - q + BAND_KEYS] for q in range(CHUNK)]
    return jnp.stack(rows, axis=1).reshape(ATTN_HEADS * CHUNK, BAND_KEYS)


SSM_BLOCK_LEN = SSM_KBLOCK * SSM_CHUNK
SSM_BLOCK_TOK = SSM_BGROUP * SSM_KBLOCK


def _token_block_spec(width):
    return pl.BlockSpec((SSM_BGROUP, SSM_BLOCK_LEN, width), lambda kb, bg: (bg, kb, 0))


def _transposed_spec(n_bgroups):
    return pl.BlockSpec((SSM_GROUPS, SSM_ROW, SSM_BLOCK_TOK),
                        lambda kb, bg: (0, 0, kb * n_bgroups + bg))


def _token_grid(bsz, seq):
    assert bsz % SSM_BGROUP == 0 and seq % SSM_BLOCK_LEN == 0
    return (seq // SSM_BLOCK_LEN, bsz // SSM_BGROUP)


def _ssm_inproj_kernel(x_ref, gain_ref, wu_ref, ut_ref, u_scr):
    n_rows = SSM_BGROUP * SSM_BLOCK_LEN
    x = x_ref[...].reshape(n_rows, D_MODEL)
    h = _rms_norm(x, gain_ref[...]).astype(BF16)
    u = jnp.dot(h, wu_ref[...], preferred_element_type=F32)
    for q in range(D_SSM // LANES):
        u_scr[q] = u[:, q * LANES:(q + 1) * LANES]
    groups_per_tile = LANES // SSM_GROUP
    for t in range(SSM_CHUNK):
        for q in range(D_SSM // LANES):
            ut = u_scr[q, pl.ds(t, SSM_BLOCK_TOK, stride=SSM_CHUNK), :].T.astype(BF16)
            for gl in range(groups_per_tile):
                ut_ref[q * groups_per_tile + gl, t * SSM_GROUP:(t + 1) * SSM_GROUP, :] = (
                    ut[gl * SSM_GROUP:(gl + 1) * SSM_GROUP, :])


def _ssm_inproj(x2, gain, w_u, bsz, seq):
    grid = _token_grid(bsz, seq)
    return pl.pallas_call(
        _ssm_inproj_kernel,
        grid=grid,
        in_specs=[_token_block_spec(D_MODEL), _full((1, D_MODEL)), _full((D_MODEL, D_SSM))],
        out_specs=_transposed_spec(grid[1]),
        out_shape=jax.ShapeDtypeStruct((SSM_GROUPS, SSM_ROW, bsz * seq // SSM_CHUNK), BF16),
        scratch_shapes=[pltpu.VMEM((D_SSM // LANES, SSM_BGROUP * SSM_BLOCK_LEN, LANES), F32)],
        compiler_params=_compiler_params(2),
        name="ssm_inproj",
    )(x2.reshape(bsz, seq, D_MODEL), gain[None, :], w_u.astype(BF16))


def _ssm_out_kernel(yt_ref, z_ref, y_scr):
    groups_per_tile = LANES // SSM_GROUP
    for t in range(SSM_CHUNK):
        for q in range(D_SSM // LANES):
            yt = jnp.concatenate(
                [yt_ref[q * groups_per_tile + gl, t * SSM_GROUP:(t + 1) * SSM_GROUP, :]
                 for gl in range(groups_per_tile)], axis=0)
            y_scr[q, pl.ds(t, SSM_BLOCK_TOK, stride=SSM_CHUNK), :] = yt.T
    for q in range(D_SSM // LANES):
        z = jax.nn.gelu(y_scr[q])
        z_ref[:, :, q * LANES:(q + 1) * LANES] = z.astype(BF16).reshape(
            SSM_BGROUP, SSM_BLOCK_LEN, LANES)


def _ssm_out(y_t, bsz, seq):
    grid = _token_grid(bsz, seq)
    z = pl.pallas_call(
        _ssm_out_kernel,
        grid=grid,
        in_specs=[_transposed_spec(grid[1])],
        out_specs=_token_block_spec(D_SSM),
        out_shape=jax.ShapeDtypeStruct((bsz, seq, D_SSM), BF16),
        scratch_shapes=[pltpu.VMEM((D_SSM // LANES, SSM_BGROUP * SSM_BLOCK_LEN, LANES), F32)],
        compiler_params=_compiler_params(2),
        name="ssm_out",
    )(y_t)
    return z.reshape(bsz * seq, D_SSM)


def _ssm_tables(lambda_re, lambda_im, log_step, b_re, b_im, c_re, c_im, d_skip):
    step = jnp.exp(log_step)[:, None]
    mag = jnp.exp(lambda_re * step)
    ang = lambda_im * step
    a_re = mag * jnp.cos(ang)
    a_im = mag * jnp.sin(ang)
    num_re = a_re - 1.0
    num_im = a_im
    den = lambda_re * lambda_re + lambda_im * lambda_im
    f_re = (num_re * lambda_re + num_im * lambda_im) / den
    f_im = (num_im * lambda_re - num_re * lambda_im) / den
    bb_re = f_re[..., None] * b_re - f_im[..., None] * b_im
    bb_im = f_re[..., None] * b_im + f_im[..., None] * b_re
    log_mag = lambda_re * step

    def powers(tau):
        tau = jnp.asarray(tau, F32)[:, None, None]
        pmag = jnp.exp(tau * log_mag[None])
        return pmag * jnp.cos(tau * ang[None]), pmag * jnp.sin(tau * ang[None])

    steps = jnp.arange(SSM_CHUNK)
    up_re, up_im = powers(steps + 1)
    cp_re = c_re[None] * up_re[:, :, None, :] - c_im[None] * up_im[:, :, None, :]
    cp_im = c_re[None] * up_im[:, :, None, :] + c_im[None] * up_re[:, :, None, :]
    readout = jnp.concatenate([cp_re, -cp_im], axis=-1)
    readout = jnp.transpose(readout, (1, 0, 2, 3)).reshape(SSM_GROUPS, SSM_ROW, 2 * SSM_STATE)

    def inject(tau):
        p_re, p_im = powers(tau)
        bt_re = jnp.transpose(bb_re, (0, 2, 1))[None]
        bt_im = jnp.transpose(bb_im, (0, 2, 1))[None]
        i_re = p_re[:, :, None, :] * bt_re - p_im[:, :, None, :] * bt_im
        i_im = p_re[:, :, None, :] * bt_im + p_im[:, :, None, :] * bt_re
        both = jnp.concatenate([i_re, i_im], axis=-1)
        return jnp.transpose(both, (1, 3, 0, 2)).reshape(SSM_GROUPS, 2 * SSM_STATE, SSM_ROW)

    inject_back = inject(-(steps + 1.0))
    inject_end = inject(SSM_CHUNK - 1.0 - steps).astype(BF16)
    d_row = jnp.tile(d_skip.reshape(SSM_GROUPS, 1, SSM_GROUP), (1, 1, SSM_CHUNK))
    (ar,), (ai,) = powers([float(SSM_CHUNK)])
    a_chunk = jnp.stack([jnp.concatenate([ar, ar], -1), jnp.concatenate([-ai, ai], -1),
                         jnp.concatenate([ai, -ai], -1)], axis=1)
    return readout, inject_back, inject_end, d_row, a_chunk


def _ssm_kernel(ut_ref, ro_ref, ib_ref, ie_ref, d_ref, a_ref, yt_ref, p_scr, q_scr, s_scr, *, bsz):
    ut = ut_ref[...]
    n_tok = ut.shape[1]
    r_hi, r_lo = _split_bf16(ro_ref[...])
    i_hi, i_lo = _split_bf16(ib_ref[...])
    kern = (jnp.dot(r_hi, i_hi, preferred_element_type=F32)
            + jnp.dot(r_hi, i_lo, preferred_element_type=F32)
            + jnp.dot(r_lo, i_hi, preferred_element_type=F32))
    row = lax.broadcasted_iota(jnp.int32, kern.shape, 0)
    col = lax.broadcasted_iota(jnp.int32, kern.shape, 1)
    group_shift = SSM_GROUP.bit_length() - 1
    causal = jnp.right_shift(row, group_shift) >= jnp.right_shift(col, group_shift)
    kern = jnp.where(causal, kern, 0.0) + jnp.where(row == col, d_ref[...], 0.0)
    wt = jnp.dot(ie_ref[...], ut, preferred_element_type=F32)
    w = wt.T
    p_scr[...] = w
    q_scr[...] = pltpu.roll(w, SSM_STATE, 1)
    a1 = a_ref[0:1, :]
    a2 = a_ref[1:2, :]
    a3 = a_ref[2:3, :]

    def block(kb, carry):
        cp, cq = carry
        base = pl.multiple_of(kb * (bsz * SSM_KBLOCK), bsz * SSM_KBLOCK)
        for k in range(SSM_KBLOCK):
            rows = pl.ds(base + k, bsz, stride=SSM_KBLOCK)
            s_scr[rows, :] = cp
            cp, cq = (a1 * cp + a2 * cq + p_scr[rows, :],
                      a1 * cq + a3 * cp + q_scr[rows, :])
        return cp, cq

    zero = jnp.zeros((bsz, 2 * SSM_STATE), F32)
    lax.fori_loop(0, n_tok // (bsz * SSM_KBLOCK), block, (zero, zero))
    yt = jnp.dot(kern.astype(BF16), ut, preferred_element_type=F32)
    yt = yt + lax.dot_general(r_hi, s_scr[...].astype(BF16), NT_DIMS, preferred_element_type=F32)
    yt_ref[...] = yt


def _ssm(ut, tables, bsz):
    readout, inject_back, inject_end, d_row, a_chunk = tables
    n_tok = ut.shape[2]
    per_group = lambda a, b: pl.BlockSpec((None, a, b), lambda g: (g, 0, 0))
    state = pltpu.VMEM((n_tok, 2 * SSM_STATE), F32)
    return pl.pallas_call(
        functools.partial(_ssm_kernel, bsz=bsz),
        grid=(SSM_GROUPS,),
        in_specs=[per_group(SSM_ROW, n_tok), per_group(SSM_ROW, 2 * SSM_STATE),
                  per_group(2 * SSM_STATE, SSM_ROW), per_group(2 * SSM_STATE, SSM_ROW),
                  per_group(1, SSM_ROW), per_group(3, 2 * SSM_STATE)],
        out_specs=per_group(SSM_ROW, n_tok),
        out_shape=jax.ShapeDtypeStruct((SSM_GROUPS, SSM_ROW, n_tok), F32),
        scratch_shapes=[state, state, state],
        compiler_params=_compiler_params(1),
        name="ssm",
    )(ut, readout, inject_back, inject_end, d_row, a_chunk)


def _split_bf16(a):
    hi = a.astype(BF16)
    lo = (a - hi.astype(F32)).astype(BF16)
    return hi, lo


def _route(logits):
    lane = lax.broadcasted_iota(jnp.int32, logits.shape, 1)
    lane_f = lane.astype(F32)
    big = float(ROUTER_LANES)

    def first_argmax(vals, vmax):
        return jnp.min(jnp.where(vals == vmax, lane_f, big), axis=-1, keepdims=True)

    g_mask = (lane >= N_EXPERTS) & (lane < N_EXPERTS + N_GROUPS)
    g_log = jnp.where(g_mask, logits, NEG_INF)
    g_max = jnp.max(g_log, axis=-1, keepdims=True)
    g_sum = jnp.sum(jnp.where(g_mask, jnp.exp(g_log - g_max), 0.0), axis=-1, keepdims=True)
    g_prob = 1.0 / g_sum
    g_idx = first_argmax(g_log, g_max) - float(N_EXPERTS)
    e_lo = g_idx * float(EXPERTS_PER_GROUP)
    e_mask = (lane_f >= e_lo) & (lane_f < e_lo + float(EXPERTS_PER_GROUP))
    e_log = jnp.where(e_mask, logits, NEG_INF)
    e_max1 = jnp.max(e_log, axis=-1, keepdims=True)
    idx1 = first_argmax(e_log, e_max1)
    e_log2 = jnp.where(lane_f == idx1, NEG_INF, e_log)
    e_max2 = jnp.max(e_log2, axis=-1, keepdims=True)
    idx2 = first_argmax(e_log2, e_max2)
    r = jnp.exp(e_max2 - e_max1)
    w1 = g_prob / (1.0 + r)
    w2 = g_prob * r / (1.0 + r)
    gates = jnp.where(lane_f == idx1, w1, 0.0) + jnp.where(lane_f == idx2, w2, 0.0)
    lo = jnp.minimum(idx1, idx2) - e_lo
    hi = jnp.maximum(idx1, idx2) - e_lo
    pair = lo * (2.0 * EXPERTS_PER_GROUP - 1.0 - lo) * 0.5 + (hi - lo - 1.0)
    return gates, g_idx * float(PAIRS_PER_GROUP) + pair


def _merge_kernel(x_ref, ya_ref, z_ref, gain1_ref, wgate_ref, bgate_ref, wglu_ref, bglu_ref,
                  wba_ref, wbs_ref, wout_ref, gain2_ref, rhi_ref, rlo_ref, rbias_ref, ltri_ref,
                  xg_ref, counts_ref, count_scr):
    x = x_ref[...]
    hn = _rms_norm(x, gain1_ref[...]).astype(BF16)
    z = z_ref[...]
    glu = jnp.dot(z, wglu_ref[...], preferred_element_type=F32) + bglu_ref[...]
    y_ssm = (z.astype(F32) * jax.nn.sigmoid(glu)).astype(BF16)
    merged = None
    for s, (y_b, wb_ref) in enumerate(((ya_ref[...], wba_ref), (y_ssm, wbs_ref))):
        cols = slice(s * D_MODEL, (s + 1) * D_MODEL)
        gate = jax.nn.sigmoid(jnp.dot(hn, wgate_ref[:, cols], preferred_element_type=F32)
                              + bgate_ref[:, cols])
        term = gate * jnp.dot(y_b, wb_ref[...], preferred_element_type=F32)
        merged = term if merged is None else merged + term
    x1 = x + jnp.dot(merged.astype(BF16), wout_ref[...], preferred_element_type=F32)
    xg_ref[:, :D_MODEL] = x1

    h = _rms_norm(x1, gain2_ref[...])
    h_hi, h_lo = _split_bf16(h)
    logits = (jnp.dot(h_hi, rhi_ref[...], preferred_element_type=F32)
              + jnp.dot(h_hi, rlo_ref[...], preferred_element_type=F32)
              + jnp.dot(h_lo, rhi_ref[...], preferred_element_type=F32)
              + rbias_ref[...])
    gates, cls = _route(logits)

    @pl.when(pl.program_id(0) == 0)
    def _():
        count_scr[...] = jnp.zeros_like(count_scr)

    lane_f = lax.broadcasted_iota(jnp.int32, gates.shape, 1).astype(F32)
    onehot = jnp.where(lane_f == cls, 1.0, 0.0)
    before = jnp.dot(ltri_ref[...], onehot.astype(BF16), preferred_element_type=F32)
    seen = count_scr[...]
    rank = jnp.sum(onehot * (before + seen), axis=-1, keepdims=True)
    count_scr[...] = seen + jnp.sum(onehot, axis=0, keepdims=True)
    counts_ref[...] = count_scr[...]
    xg_ref[:, D_MODEL:] = (gates + jnp.where(lane_f == float(CLASS_LANE), cls, 0.0)
                           + jnp.where(lane_f == float(RANK_LANE), rank, 0.0))


def _merge(x2, y_attn, z, gain1, w_gate, b_gate, w_glu, b_glu, w_branch, w_out, ffn_gain,
           w_group_router, group_bias, w_expert_router, expert_bias, tm):
    t = x2.shape[0]
    pad = ROUTER_LANES - N_EXPERTS - N_GROUPS
    w_r = jnp.concatenate([w_expert_router, w_group_router,
                           jnp.zeros((D_MODEL, pad), F32)], axis=1)
    r_hi, r_lo = _split_bf16(w_r)
    r_bias = jnp.concatenate([expert_bias, group_bias, jnp.zeros((pad,), F32)])[None, :]
    ltri = jnp.tril(jnp.ones((tm, tm), F32), -1).astype(BF16)
    row = lambda n: pl.BlockSpec((tm, n), lambda i: (i, 0))
    resident = lambda shape: pl.BlockSpec(shape, lambda i: (0, 0), pipeline_mode=pl.Buffered(1))
    return pl.pallas_call(
        _merge_kernel,
        grid=(t // tm,),
        in_specs=[row(D_MODEL), row(D_ATTN), row(D_SSM),
                  resident((1, D_MODEL)), resident((D_MODEL, N_BRANCH * D_MODEL)),
                  resident((1, N_BRANCH * D_MODEL)),
                  resident((D_SSM, D_SSM)), resident((1, D_SSM)),
                  resident((D_ATTN, D_MODEL)), resident((D_SSM, D_MODEL)),
                  resident((D_MODEL, D_MODEL)), resident((1, D_MODEL)),
                  resident((D_MODEL, ROUTER_LANES)), resident((D_MODEL, ROUTER_LANES)),
                  resident((1, ROUTER_LANES)), resident((tm, tm))],
        out_specs=[row(D_ROUTED), pl.BlockSpec((1, ROUTER_LANES), lambda i: (0, 0))],
        out_shape=[jax.ShapeDtypeStruct((t, D_ROUTED), F32),
                   jax.ShapeDtypeStruct((1, ROUTER_LANES), F32)],
        scratch_shapes=[pltpu.VMEM((1, ROUTER_LANES), F32)],
        compiler_params=_compiler_params(1),
        name="merge",
    )(x2, y_attn, z, gain1[None, :], w_gate.astype(BF16), b_gate[None, :],
      w_glu.astype(BF16), b_glu[None, :],
      w_branch[:D_ATTN].astype(BF16), w_branch[D_ATTN:].astype(BF16),
      w_out.astype(BF16), ffn_gain[None, :], r_hi, r_lo, r_bias, ltri)


def _moe_plan(routed, counts, n_tiles):
    cnt = counts[0, :N_CLASSES].astype(jnp.int32)
    padded = (cnt + (MOE_TILE - 1)) // MOE_TILE * MOE_TILE
    ends = jnp.cumsum(padded)
    starts = ends - padded
    classes = jnp.arange(N_CLASSES, dtype=jnp.int32)
    cls = routed[:, D_MODEL + CLASS_LANE].astype(jnp.int32)
    rank = routed[:, D_MODEL + RANK_LANE].astype(jnp.int32)
    dest = rank + jnp.sum(jnp.where(cls[:, None] == classes[None, :], starts[None, :], 0), axis=1)
    n_used = ends[-1] // MOE_TILE
    tile_end = ends // MOE_TILE
    tiles = jnp.arange(n_tiles, dtype=jnp.int32)
    tile_cls = jnp.minimum(jnp.sum(tiles[:, None] >= tile_end[None, :], axis=1), N_CLASSES - 1)
    pick = lambda table: jnp.sum(
        jnp.where(tile_cls[:, None] == classes[None, :], jnp.asarray(table, jnp.int32)[None, :], 0),
        axis=1).astype(jnp.int32)
    tile_ea = pick([e for e, _ in _CLASS_EXPERTS])
    tile_eb = pick([e for _, e in _CLASS_EXPERTS])
    clear_a = jnp.where(padded > 0, tile_end - 1, -1)
    tail = n_used + classes
    clear_b = jnp.where(tail < n_tiles, tail, -1)
    clear = jnp.concatenate([clear_a, clear_b]).astype(jnp.int32)
    return dest.astype(jnp.int32), tile_ea, tile_eb, jnp.reshape(n_used, (1,)).astype(jnp.int32), clear


def _row_copy(src, src_row, dst, dst_row, sem):
    return pltpu.make_async_copy(src.at[pl.ds(src_row, 1), :], dst.at[pl.ds(dst_row, 1), :], sem)


def _dispatch_kernel(clear_ref, dest_ref, src_ref, dst_hbm, zero_scr, sem, *, rows):
    step = pl.program_id(0)

    def clear_copy(j):
        start = pl.multiple_of(clear_ref[j] * MOE_TILE, MOE_TILE)
        return pltpu.make_async_copy(zero_scr, dst_hbm.at[pl.ds(start, MOE_TILE), :], sem)

    @pl.when(step == 0)
    def _():
        zero_scr[...] = jnp.zeros_like(zero_scr)
        for wait in (False, True):
            for j in range(2 * N_CLASSES):
                @pl.when(clear_ref[j] >= 0)
                def _():
                    clear_copy(j).wait() if wait else clear_copy(j).start()

    for r in range(rows):
        _row_copy(src_ref, r, dst_hbm, dest_ref[r], sem).start(priority=r % DMA_THREADS)
    for r in range(rows):
        _row_copy(src_ref, r, dst_hbm, 0, sem).wait()


def _dispatch(routed, dest, clear, n_tiles, rows):
    t, width = routed.shape
    return pl.pallas_call(
        functools.partial(_dispatch_kernel, rows=rows),
        grid_spec=pltpu.PrefetchScalarGridSpec(
            num_scalar_prefetch=1,
            grid=(t // rows,),
            in_specs=[pl.BlockSpec((rows,), lambda i, clear: (i,), memory_space=pltpu.SMEM),
                      pl.BlockSpec((rows, width), lambda i, clear: (i, 0))],
            out_specs=pl.BlockSpec(memory_space=pl.ANY),
            scratch_shapes=[pltpu.VMEM((MOE_TILE, width), F32), pltpu.SemaphoreType.DMA(())]),
        out_shape=jax.ShapeDtypeStruct((n_tiles * MOE_TILE, width), F32),
        compiler_params=_compiler_params(1),
        name="moe_dispatch",
    )(clear, dest, routed)


def _expert_kernel(ea_ref, eb_ref, used_ref, xs_ref, gain_ref, wga_ref, wua_ref, wda_ref,
                   wgb_ref, wub_ref, wdb_ref, o_ref):
    i = pl.program_id(0)

    @pl.when(i < used_ref[0])
    def _():
        x1 = xs_ref[:, :D_MODEL]
        gates = xs_ref[:, D_MODEL:]
        h = _rms_norm(x1, gain_ref[...]).astype(BF16)
        lane = lax.broadcasted_iota(jnp.int32, gates.shape, 1)
        acc = x1
        for e_ref, wg_ref, wu_ref, wd_ref in ((ea_ref, wga_ref, wua_ref, wda_ref),
                                              (eb_ref, wgb_ref, wub_ref, wdb_ref)):
            gate = jnp.sum(jnp.where(lane == e_ref[i], gates, 0.0), axis=-1, keepdims=True)
            a = jnp.dot(h, wg_ref[...], preferred_element_type=F32)
            u = jnp.dot(h, wu_ref[...], preferred_element_type=F32)
            hid = (jax.nn.silu(a) * u * gate).astype(BF16)
            acc = acc + jnp.dot(hid, wd_ref[...], preferred_element_type=F32)
        o_ref[...] = acc

    @pl.when(i >= used_ref[0])
    def _():
        o_ref[...] = jnp.zeros_like(o_ref)


def _experts(xs, tile_ea, tile_eb, n_used, gain, w_e_gate, w_e_up, w_e_down):
    n_tiles = xs.shape[0] // MOE_TILE
    by_a = lambda r, c: pl.BlockSpec((None, r, c), lambda i, ea, eb, used: (ea[i], 0, 0))
    by_b = lambda r, c: pl.BlockSpec((None, r, c), lambda i, ea, eb, used: (eb[i], 0, 0))
    wg, wu, wd = (w.astype(BF16) for w in (w_e_gate, w_e_up, w_e_down))
    return pl.pallas_call(
        _expert_kernel,
        grid_spec=pltpu.PrefetchScalarGridSpec(
            num_scalar_prefetch=3,
            grid=(n_tiles,),
            in_specs=[pl.BlockSpec((MOE_TILE, D_ROUTED),
                                   lambda i, ea, eb, used: (jnp.minimum(i, used[0] - 1), 0)),
                      pl.BlockSpec((1, D_MODEL), lambda i, ea, eb, used: (0, 0)),
                      by_a(D_MODEL, D_EXPERT), by_a(D_MODEL, D_EXPERT), by_a(D_EXPERT, D_MODEL),
                      by_b(D_MODEL, D_EXPERT), by_b(D_MODEL, D_EXPERT), by_b(D_EXPERT, D_MODEL)],
            out_specs=pl.BlockSpec((MOE_TILE, D_MODEL), lambda i, ea, eb, used: (i, 0))),
        out_shape=jax.ShapeDtypeStruct((n_tiles * MOE_TILE, D_MODEL), F32),
        compiler_params=_compiler_params(1),
        name="moe_experts",
    )(tile_ea, tile_eb, n_used, xs, gain[None, :], wg, wu, wd, wg, wu, wd)


def _unsort_kernel(dest_ref, src_hbm, dst_ref, sem, *, rows):
    for r in range(rows):
        _row_copy(src_hbm, dest_ref[r], dst_ref, r, sem).start(priority=r % DMA_THREADS)
    for r in range(rows):
        _row_copy(src_hbm, 0, dst_ref, r, sem).wait()


def _unsort(xo, dest, rows):
    t = dest.shape[0]
    return pl.pallas_call(
        functools.partial(_unsort_kernel, rows=rows),
        grid=(t // rows,),
        in_specs=[pl.BlockSpec((rows,), lambda i: (i,), memory_space=pltpu.SMEM),
                  pl.BlockSpec(memory_space=pl.ANY)],
        out_specs=pl.BlockSpec((rows, xo.shape[1]), lambda i: (i, 0)),
        out_shape=jax.ShapeDtypeStruct((t, xo.shape[1]), F32),
        scratch_shapes=[pltpu.SemaphoreType.DMA(())],
        compiler_params=_compiler_params(1),
        name="moe_unsort",
    )(dest, xo)


def _moe(routed, counts, ffn_gain, w_e_gate, w_e_up, w_e_down, rows):
    t = routed.shape[0]
    assert t % MOE_TILE == 0 and t % rows == 0
    n_tiles = t // MOE_TILE + N_CLASSES
    dest, tile_ea, tile_eb, n_used, clear = _moe_plan(routed, counts, n_tiles)
    xs = _dispatch(routed, dest, clear, n_tiles, rows)
    xo = _experts(xs, tile_ea, tile_eb, n_used, ffn_gain, w_e_gate, w_e_up, w_e_down)
    return _unsort(xo, dest, rows)


def kernel(x, mix_norm_gain, w_in, b_gate, q_gain, k_gain, rel_bias, ssm_lambda_re, ssm_lambda_im, ssm_log_step, ssm_b_re, ssm_b_im, ssm_c_re, ssm_c_im, ssm_d, w_glu, b_glu, w_branch, w_out, ffn_norm_gain, w_group_router, group_bias, w_expert_router, expert_bias, w_e_gate, w_e_up, w_e_down):
    bsz, seq, _ = x.shape
    assert seq % (SSM_KBLOCK * SSM_CHUNK) == 0 and seq % (2 * CHUNK) == 0
    depth = w_in.shape[0]
    tiles = _tiles(bsz * seq)
    x2 = x.reshape(bsz * seq, D_MODEL)
    for i in range(depth):
        w_qkv = w_in[i][:, :D_QKV]
        w_u = w_in[i][:, D_QKV:D_QKV + D_SSM]
        w_gate = w_in[i][:, D_QKV + D_SSM:]
        q, k, v = _inproj(x2, mix_norm_gain[i], w_qkv, q_gain[i], k_gain[i], tiles["inproj"])
        y_attn = _attention(q, k, v, _rel_bias_table(rel_bias[i]), bsz, seq)
        u_t = _ssm_inproj(x2, mix_norm_gain[i], w_u, bsz, seq)
        tables = _ssm_tables(ssm_lambda_re[i], ssm_lambda_im[i], ssm_log_step[i],
                             ssm_b_re[i], ssm_b_im[i], ssm_c_re[i], ssm_c_im[i], ssm_d[i])
        z = _ssm_out(_ssm(u_t, tables, bsz), bsz, seq)
        routed, counts = _merge(x2, y_attn, z, mix_norm_gain[i], w_gate, b_gate[i], w_glu[i],
                                b_glu[i], w_branch[i], w_out[i], ffn_norm_gain[i],
                                w_group_router[i], group_bias[i], w_expert_router[i],
                                expert_bias[i], tiles["merge"])
        x2 = _moe(routed, counts, ffn_norm_gain[i], w_e_gate[i], w_e_up[i], w_e_down[i],
                  tiles["moe_rows"])
    return x2.reshape(bsz, seq, D_MODEL)
```

```python
import functools

import jax
import jax.numpy as jnp
from jax import lax
from jax.experimental import pallas as pl
from jax.experimental.pallas import tpu as pltpu

F32 = jnp.float32
BF16 = jnp.bfloat16

D_MODEL = 1024
CHUNK = 64
LEFT_CHUNKS = 8
BAND = LEFT_CHUNKS + 1
BAND_KEYS = BAND * CHUNK
PAD_KEYS = LEFT_CHUNKS * CHUNK
ATTN_HEADS = 8
HEAD_DIM = 64
D_ATTN = ATTN_HEADS * HEAD_DIM
MAX_REL = 256
REL_FUTURE = CHUNK - 1
D_SSM = D_MODEL // 2
SSM_GROUP = 16
SSM_GROUPS = D_SSM // SSM_GROUP
SSM_STATE = 64
N_BRANCH = 2
D_QKV = 3 * D_ATTN
D_IN = D_QKV + D_SSM + N_BRANCH * D_MODEL
N_GROUPS = 4
EXPERTS_PER_GROUP = 4
N_EXPERTS = N_GROUPS * EXPERTS_PER_GROUP
D_EXPERT = D_MODEL // 4
EPS = 1e-6
NEG_INF = -1e30
LOG2_E = 1.4426950408889634

LANES = 128
HEAD_PAIRS = D_ATTN // LANES
PROJ_ROWS = 512
SSM_CHUNK = 16
SSM_ROW = SSM_CHUNK * SSM_GROUP
SSM_KBLOCK = 16
SSM_BGROUP = LANES // SSM_KBLOCK
ROUTER_LANES = LANES
D_ROUTED = D_MODEL + ROUTER_LANES
CLASS_LANE = 32
RANK_LANE = 33
MOE_TILE = 256
DMA_THREADS = 2
_PAIRS = [(a, b) for a in range(EXPERTS_PER_GROUP) for b in range(a + 1, EXPERTS_PER_GROUP)]
PAIRS_PER_GROUP = len(_PAIRS)
N_CLASSES = N_GROUPS * PAIRS_PER_GROUP
_CLASS_EXPERTS = [(g * EXPERTS_PER_GROUP + a, g * EXPERTS_PER_GROUP + b)
                  for g in range(N_GROUPS) for a, b in _PAIRS]
VMEM_LIMIT = 56 * 1024 * 1024


def _tiles(n_tokens):
    return dict(merge=512, moe_rows=512)


def _compiler_params(n_axes):
    return pltpu.CompilerParams(
        dimension_semantics=("arbitrary",) * n_axes,
        vmem_limit_bytes=VMEM_LIMIT)


def _full(shape):
    nd = len(shape)
    return pl.BlockSpec(shape, lambda *_: (0,) * nd)


def _rms_norm(x, gain):
    ms = jnp.mean(x * x, axis=-1, keepdims=True)
    return x * lax.rsqrt(ms + EPS) * gain


NT_DIMS = (((1,), (1,)), ((), ()))


def _attn_kernel(x_ref, gain_ref, w_ref, qg_ref, kg_ref, ones_ref, bias_ref, o_ref,
                 q_scr, kpad, v_even, v_odd, s_a, s_b):
    seq = x_ref.shape[0]
    n_chunks = seq // CHUNK
    zeros = jnp.zeros((PAD_KEYS, D_ATTN), BF16)
    kpad[0:PAD_KEYS, :] = zeros
    v_even[0:PAD_KEYS, :] = zeros
    v_odd[0:PAD_KEYS, :] = zeros

    lane_v = lax.broadcasted_iota(jnp.int32, (PROJ_ROWS, D_ATTN), 1)
    even_v = (lane_v % LANES) < HEAD_DIM
    for c in range(seq // PROJ_ROWS):
        rows = slice(c * PROJ_ROWS, (c + 1) * PROJ_ROWS)
        kv_rows = slice(PAD_KEYS + c * PROJ_ROWS, PAD_KEYS + (c + 1) * PROJ_ROWS)
        h = _rms_norm(x_ref[rows, :], gain_ref[...]).astype(BF16)

        def proj(lo):
            return jnp.dot(h, w_ref[:, lo:lo + D_ATTN], preferred_element_type=F32)

        def head_norm(t, g_ref):
            ss = jnp.dot((t * t).astype(BF16), ones_ref[...], preferred_element_type=F32)
            return t * lax.rsqrt(ss * (1.0 / HEAD_DIM) + EPS) * g_ref[...]

        q_scr[rows, :] = head_norm(proj(0), qg_ref).astype(BF16)
        kpad[kv_rows, :] = head_norm(proj(D_ATTN), kg_ref).astype(BF16)
        v = proj(2 * D_ATTN).astype(BF16)
        v_even[kv_rows, :] = jnp.where(even_v, v, jnp.zeros_like(v))
        v_odd[kv_rows, :] = jnp.where(even_v, jnp.zeros_like(v), v)

    lane = lax.broadcasted_iota(jnp.int32, (CHUNK, LANES), 1)
    first_head = lane < HEAD_DIM

    def scores(n, s_buf, skip=0):
        r0 = n * CHUNK if isinstance(n, int) else pl.multiple_of(n * CHUNK, CHUNK)
        for j in range(HEAD_PAIRS):
            cols = slice(j * LANES, (j + 1) * LANES)
            qp = q_scr[pl.ds(r0, CHUNK), cols]
            zq = jnp.zeros_like(qp)
            q2 = jnp.concatenate([jnp.where(first_head, qp, zq), jnp.where(first_head, zq, qp)], axis=0)
            kb = kpad[pl.ds(r0 + skip, BAND_KEYS - skip), cols]
            s_buf[j * 2 * CHUNK:(j + 1) * 2 * CHUNK, skip:] = lax.dot_general(
                q2, kb, NT_DIMS, preferred_element_type=F32)

    def softmax_pv(n, s_buf, skip=0, first_valid=0):
        r0 = n * CHUNK if isinstance(n, int) else pl.multiple_of(n * CHUNK, CHUNK)
        width = BAND_KEYS - skip
        for j in range(HEAD_PAIRS):
            cols = slice(j * LANES, (j + 1) * LANES)
            rows = slice(j * 2 * CHUNK, (j + 1) * 2 * CHUNK)
            s = s_buf[rows, skip:] + bias_ref[rows, skip:]
            if first_valid > skip:
                key_col = lax.broadcasted_iota(jnp.int32, s.shape, 1) + skip
                s = jnp.where(key_col >= first_valid, s, NEG_INF)
            m = jnp.max(s, axis=-1, keepdims=True)
            p = jnp.exp2(s - m)
            inv_l = 1.0 / jnp.sum(p, axis=-1, keepdims=True)
            pb = p.astype(BF16)
            o0 = jnp.dot(pb[:CHUNK], v_even[pl.ds(r0 + skip, width), cols], preferred_element_type=F32)
            o1 = jnp.dot(pb[CHUNK:], v_odd[pl.ds(r0 + skip, width), cols], preferred_element_type=F32)
            o_ref[pl.ds(r0, CHUNK), cols] = (o0 * inv_l[:CHUNK] + o1 * inv_l[CHUNK:]).astype(BF16)

    assert n_chunks % 2 == 0 and LEFT_CHUNKS % 2 == 0 and n_chunks > LEFT_CHUNKS

    def band_start(n):
        first_valid = max(LEFT_CHUNKS - n, 0) * CHUNK
        return first_valid // LANES * LANES, first_valid

    bufs = (s_a, s_b)
    scores(0, s_a, band_start(0)[0])
    for n in range(LEFT_CHUNKS):
        scores(n + 1, bufs[(n + 1) % 2], band_start(n + 1)[0])
        softmax_pv(n, bufs[n % 2], *band_start(n))

    def pair_body(i, carry):
        n0 = 2 * i
        scores(n0 + 1, s_b)
        softmax_pv(n0, s_a)
        scores(jnp.minimum(n0 + 2, n_chunks - 1), s_a)
        softmax_pv(n0 + 1, s_b)
        return carry

    lax.fori_loop(LEFT_CHUNKS // 2, n_chunks // 2, pair_body, 0)


def _attention(x2, gain, w_qkv, q_gain, k_gain, bias, bsz, seq):
    assert seq % PROJ_ROWS == 0
    ones = jnp.kron(jnp.eye(ATTN_HEADS, dtype=F32), jnp.ones((HEAD_DIM, HEAD_DIM), F32)).astype(BF16)
    qg = jnp.tile(q_gain, ATTN_HEADS)[None, :] * (HEAD_DIM ** -0.5 * LOG2_E)
    kg = jnp.tile(k_gain, ATTN_HEADS)[None, :]
    seq_spec = lambda width: pl.BlockSpec((None, seq, width), lambda b: (b, 0, 0))
    padded = pltpu.VMEM((PAD_KEYS + seq, D_ATTN), BF16)
    out = pl.pallas_call(
        _attn_kernel,
        grid=(bsz,),
        in_specs=[seq_spec(D_MODEL), _full((1, D_MODEL)), _full((D_MODEL, D_QKV)),
                  _full((1, D_ATTN)), _full((1, D_ATTN)), _full((D_ATTN, D_ATTN)),
                  _full((ATTN_HEADS * CHUNK, BAND_KEYS))],
        out_specs=seq_spec(D_ATTN),
        out_shape=jax.ShapeDtypeStruct((bsz, seq, D_ATTN), BF16),
        scratch_shapes=[pltpu.VMEM((seq, D_ATTN), BF16), padded, padded, padded,
                        pltpu.VMEM((ATTN_HEADS * CHUNK, BAND_KEYS), F32),
                        pltpu.VMEM((ATTN_HEADS * CHUNK, BAND_KEYS), F32)],
        compiler_params=_compiler_params(1),
        name="attention",
    )(x2.reshape(bsz, seq, D_MODEL), gain[None, :], w_qkv.astype(BF16), qg, kg, ones, bias)
    return out.reshape(bsz * seq, D_ATTN)


def _rel_bias_table(rel_bias):
    n_rel = REL_FUTURE + MAX_REL + 1
    far = jnp.broadcast_to(rel_bias[:, n_rel - 1:], (ATTN_HEADS, PAD_KEYS - MAX_REL + CHUNK))
    ext = jnp.concatenate([far, rel_bias[:, n_rel - 2::-1]], axis=1) * LOG2_E
    rows = [ext[:, CHUNK - 1 - q:CHUNK - 1 - q + BAND_KEYS] for q in range(CHUNK)]
    return jnp.stack(rows, axis=1).reshape(ATTN_HEADS * CHUNK, BAND_KEYS)


SSM_BLOCK_LEN = SSM_KBLOCK * SSM_CHUNK
SSM_BLOCK_TOK = SSM_BGROUP * SSM_KBLOCK


def _token_block_spec(width):
    return pl.BlockSpec((SSM_BGROUP, SSM_BLOCK_LEN, width), lambda kb, bg: (bg, kb, 0))


def _transposed_spec(n_bgroups):
    return pl.BlockSpec((SSM_GROUPS, SSM_ROW, SSM_BLOCK_TOK),
                        lambda kb, bg: (0, 0, kb * n_bgroups + bg))


def _token_grid(bsz, seq):
    assert bsz % SSM_BGROUP == 0 and seq % SSM_BLOCK_LEN == 0
    return (seq // SSM_BLOCK_LEN, bsz // SSM_BGROUP)


def _ssm_inproj_kernel(x_ref, gain_ref, wu_ref, ut_ref, u_scr):
    n_rows = SSM_BGROUP * SSM_BLOCK_LEN
    x = x_ref[...].reshape(n_rows, D_MODEL)
    h = _rms_norm(x, gain_ref[...]).astype(BF16)
    u = jnp.dot(h, wu_ref[...], preferred_element_type=F32)
    for q in range(D_SSM // LANES):
        u_scr[q] = u[:, q * LANES:(q + 1) * LANES]
    groups_per_tile = LANES // SSM_GROUP
    for t in range(SSM_CHUNK):
        for q in range(D_SSM // LANES):
            ut = u_scr[q, pl.ds(t, SSM_BLOCK_TOK, stride=SSM_CHUNK), :].T.astype(BF16)
            for gl in range(groups_per_tile):
                ut_ref[q * groups_per_tile + gl, t * SSM_GROUP:(t + 1) * SSM_GROUP, :] = (
                    ut[gl * SSM_GROUP:(gl + 1) * SSM_GROUP, :])


def _ssm_inproj(x2, gain, w_u, bsz, seq):
    grid = _token_grid(bsz, seq)
    return pl.pallas_call(
        _ssm_inproj_kernel,
        grid=grid,
        in_specs=[_token_block_spec(D_MODEL), _full((1, D_MODEL)), _full((D_MODEL, D_SSM))],
        out_specs=_transposed_spec(grid[1]),
        out_shape=jax.ShapeDtypeStruct((SSM_GROUPS, SSM_ROW, bsz * seq // SSM_CHUNK), BF16),
        scratch_shapes=[pltpu.VMEM((D_SSM // LANES, SSM_BGROUP * SSM_BLOCK_LEN, LANES), F32)],
        compiler_params=_compiler_params(2),
        name="ssm_inproj",
    )(x2.reshape(bsz, seq, D_MODEL), gain[None, :], w_u.astype(BF16))


def _ssm_out_kernel(yt_ref, z_ref, y_scr):
    groups_per_tile = LANES // SSM_GROUP
    for t in range(SSM_CHUNK):
        for q in range(D_SSM // LANES):
            yt = jnp.concatenate(
                [yt_ref[q * groups_per_tile + gl, t * SSM_GROUP:(t + 1) * SSM_GROUP, :]
                 for gl in range(groups_per_tile)], axis=0)
            y_scr[q, pl.ds(t, SSM_BLOCK_TOK, stride=SSM_CHUNK), :] = yt.T
    for q in range(D_SSM // LANES):
        z = jax.nn.gelu(y_scr[q])
        z_ref[:, :, q * LANES:(q + 1) * LANES] = z.astype(BF16).reshape(
            SSM_BGROUP, SSM_BLOCK_LEN, LANES)


def _ssm_out(y_t, bsz, seq):
    grid = _token_grid(bsz, seq)
    z = pl.pallas_call(
        _ssm_out_kernel,
        grid=grid,
        in_specs=[_transposed_spec(grid[1])],
        out_specs=_token_block_spec(D_SSM),
        out_shape=jax.ShapeDtypeStruct((bsz, seq, D_SSM), BF16),
        scratch_shapes=[pltpu.VMEM((D_SSM // LANES, SSM_BGROUP * SSM_BLOCK_LEN, LANES), F32)],
        compiler_params=_compiler_params(2),
        name="ssm_out",
    )(y_t)
    return z.reshape(bsz * seq, D_SSM)


def _ssm_tables(lambda_re, lambda_im, log_step, b_re, b_im, c_re, c_im, d_skip):
    step = jnp.exp(log_step)[:, None]
    mag = jnp.exp(lambda_re * step)
    ang = lambda_im * step
    a_re = mag * jnp.cos(ang)
    a_im = mag * jnp.sin(ang)
    num_re = a_re - 1.0
    num_im = a_im
    den = lambda_re * lambda_re + lambda_im * lambda_im
    f_re = (num_re * lambda_re + num_im * lambda_im) / den
    f_im = (num_im * lambda_re - num_re * lambda_im) / den
    bb_re = f_re[..., None] * b_re - f_im[..., None] * b_im
    bb_im = f_re[..., None] * b_im + f_im[..., None] * b_re
    log_mag = lambda_re * step

    def powers(tau):
        tau = jnp.asarray(tau, F32)[:, None, None]
        pmag = jnp.exp(tau * log_mag[None])
        return pmag * jnp.cos(tau * ang[None]), pmag * jnp.sin(tau * ang[None])

    steps = jnp.arange(SSM_CHUNK)
    up_re, up_im = powers(steps + 1)
    cp_re = c_re[None] * up_re[:, :, None, :] - c_im[None] * up_im[:, :, None, :]
    cp_im = c_re[None] * up_im[:, :, None, :] + c_im[None] * up_re[:, :, None, :]
    readout = jnp.concatenate([cp_re, -cp_im], axis=-1)
    readout = jnp.transpose(readout, (1, 0, 2, 3)).reshape(SSM_GROUPS, SSM_ROW, 2 * SSM_STATE)

    def inject(tau):
        p_re, p_im = powers(tau)
        bt_re = jnp.transpose(bb_re, (0, 2, 1))[None]
        bt_im = jnp.transpose(bb_im, (0, 2, 1))[None]
        i_re = p_re[:, :, None, :] * bt_re - p_im[:, :, None, :] * bt_im
        i_im = p_re[:, :, None, :] * bt_im + p_im[:, :, None, :] * bt_re
        both = jnp.concatenate([i_re, i_im], axis=-1)
        return jnp.transpose(both, (1, 3, 0, 2)).reshape(SSM_GROUPS, 2 * SSM_STATE, SSM_ROW)

    inject_back = inject(-(steps + 1.0))
    inject_end = inject(SSM_CHUNK - 1.0 - steps).astype(BF16)
    d_row = jnp.tile(d_skip.reshape(SSM_GROUPS, 1, SSM_GROUP), (1, 1, SSM_CHUNK))
    (ar,), (ai,) = powers([float(SSM_CHUNK)])
    a_chunk = jnp.stack([jnp.concatenate([ar, ar], -1), jnp.concatenate([-ai, ai], -1),
                         jnp.concatenate([ai, -ai], -1)], axis=1)
    return readout, inject_back, inject_end, d_row, a_chunk


def _ssm_kernel(ut_ref, ro_ref, ib_ref, ie_ref, d_ref, a_ref, yt_ref, p_scr, q_scr, s_scr, *, bsz):
    ut = ut_ref[...]
    n_tok = ut.shape[1]
    r_hi, r_lo = _split_bf16(ro_ref[...])
    i_hi, i_lo = _split_bf16(ib_ref[...])
    kern = (jnp.dot(r_hi, i_hi, preferred_element_type=F32)
            + jnp.dot(r_hi, i_lo, preferred_element_type=F32)
            + jnp.dot(r_lo, i_hi, preferred_element_type=F32))
    row = lax.broadcasted_iota(jnp.int32, kern.shape, 0)
    col = lax.broadcasted_iota(jnp.int32, kern.shape, 1)
    group_shift = SSM_GROUP.bit_length() - 1
    causal = jnp.right_shift(row, group_shift) >= jnp.right_shift(col, group_shift)
    kern = jnp.where(causal, kern, 0.0) + jnp.where(row == col, d_ref[...], 0.0)
    wt = jnp.dot(ie_ref[...], ut, preferred_element_type=F32)
    w = wt.T
    p_scr[...] = w
    q_scr[...] = pltpu.roll(w, SSM_STATE, 1)
    a1 = a_ref[0:1, :]
    a2 = a_ref[1:2, :]
    a3 = a_ref[2:3, :]

    def block(kb, carry):
        cp, cq = carry
        base = pl.multiple_of(kb * (bsz * SSM_KBLOCK), bsz * SSM_KBLOCK)
        for k in range(SSM_KBLOCK):
            rows = pl.ds(base + k, bsz, stride=SSM_KBLOCK)
            s_scr[rows, :] = cp
            cp, cq = (a1 * cp + a2 * cq + p_scr[rows, :],
                      a1 * cq + a3 * cp + q_scr[rows, :])
        return cp, cq

    zero = jnp.zeros((bsz, 2 * SSM_STATE), F32)
    lax.fori_loop(0, n_tok // (bsz * SSM_KBLOCK), block, (zero, zero))
    yt = jnp.dot(kern.astype(BF16), ut, preferred_element_type=F32)
    yt = yt + lax.dot_general(r_hi, s_scr[...].astype(BF16), NT_DIMS, preferred_element_type=F32)
    yt_ref[...] = yt


def _ssm(ut, tables, bsz):
    readout, inject_back, inject_end, d_row, a_chunk = tables
    n_tok = ut.shape[2]
    per_group = lambda a, b: pl.BlockSpec((None, a, b), lambda g: (g, 0, 0))
    state = pltpu.VMEM((n_tok, 2 * SSM_STATE), F32)
    return pl.pallas_call(
        functools.partial(_ssm_kernel, bsz=bsz),
        grid=(SSM_GROUPS,),
        in_specs=[per_group(SSM_ROW, n_tok), per_group(SSM_ROW, 2 * SSM_STATE),
                  per_group(2 * SSM_STATE, SSM_ROW), per_group(2 * SSM_STATE, SSM_ROW),
                  per_group(1, SSM_ROW), per_group(3, 2 * SSM_STATE)],
        out_specs=per_group(SSM_ROW, n_tok),
        out_shape=jax.ShapeDtypeStruct((SSM_GROUPS, SSM_ROW, n_tok), F32),
        scratch_shapes=[state, state, state],
        compiler_params=_compiler_params(1),
        name="ssm",
    )(ut, readout, inject_back, inject_end, d_row, a_chunk)


def _split_bf16(a):
    hi = a.astype(BF16)
    lo = (a - hi.astype(F32)).astype(BF16)
    return hi, lo


def _route(logits):
    lane = lax.broadcasted_iota(jnp.int32, logits.shape, 1)
    lane_f = lane.astype(F32)
    big = float(ROUTER_LANES)

    def first_argmax(vals, vmax):
        return jnp.min(jnp.where(vals == vmax, lane_f, big), axis=-1, keepdims=True)

    g_mask = (lane >= N_EXPERTS) & (lane < N_EXPERTS + N_GROUPS)
    g_log = jnp.where(g_mask, logits, NEG_INF)
    g_max = jnp.max(g_log, axis=-1, keepdims=True)
    g_sum = jnp.sum(jnp.where(g_mask, jnp.exp(g_log - g_max), 0.0), axis=-1, keepdims=True)
    g_prob = 1.0 / g_sum
    g_idx = first_argmax(g_log, g_max) - float(N_EXPERTS)
    e_lo = g_idx * float(EXPERTS_PER_GROUP)
    e_mask = (lane_f >= e_lo) & (lane_f < e_lo + float(EXPERTS_PER_GROUP))
    e_log = jnp.where(e_mask, logits, NEG_INF)
    e_max1 = jnp.max(e_log, axis=-1, keepdims=True)
    idx1 = first_argmax(e_log, e_max1)
    e_log2 = jnp.where(lane_f == idx1, NEG_INF, e_log)
    e_max2 = jnp.max(e_log2, axis=-1, keepdims=True)
    idx2 = first_argmax(e_log2, e_max2)
    r = jnp.exp(e_max2 - e_max1)
    w1 = g_prob / (1.0 + r)
    w2 = g_prob * r / (1.0 + r)
    gates = jnp.where(lane_f == idx1, w1, 0.0) + jnp.where(lane_f == idx2, w2, 0.0)
    lo = jnp.minimum(idx1, idx2) - e_lo
    hi = jnp.maximum(idx1, idx2) - e_lo
    pair = lo * (2.0 * EXPERTS_PER_GROUP - 1.0 - lo) * 0.5 + (hi - lo - 1.0)
    return gates, g_idx * float(PAIRS_PER_GROUP) + pair


def _merge_kernel(x_ref, ya_ref, z_ref, gain1_ref, wgate_ref, bgate_ref, wglu_ref, bglu_ref,
                  wba_ref, wbs_ref, wout_ref, gain2_ref, rcat_ref, rbias_ref, ltri_ref,
                  xg_ref, counts_ref, count_scr):
    x = x_ref[...]
    hn = _rms_norm(x, gain1_ref[...]).astype(BF16)
    z = z_ref[...]
    glu = jnp.dot(z, wglu_ref[...], preferred_element_type=F32) + bglu_ref[...]
    y_ssm = (z.astype(F32) * jax.nn.sigmoid(glu)).astype(BF16)
    merged = None
    for s, (y_b, wb_ref) in enumerate(((ya_ref[...], wba_ref), (y_ssm, wbs_ref))):
        cols = slice(s * D_MODEL, (s + 1) * D_MODEL)
        gate = jax.nn.sigmoid(jnp.dot(hn, wgate_ref[:, cols], preferred_element_type=F32)
                              + bgate_ref[:, cols])
        term = gate * jnp.dot(y_b, wb_ref[...], preferred_element_type=F32)
        merged = term if merged is None else merged + term
    x1 = x + jnp.dot(merged.astype(BF16), wout_ref[...], preferred_element_type=F32)
    xg_ref[:, :D_MODEL] = x1

    h = _rms_norm(x1, gain2_ref[...])
    h_hi, h_lo = _split_bf16(h)
    hi_terms = jnp.dot(h_hi, rcat_ref[...], preferred_element_type=F32)
    logits = (hi_terms[:, :ROUTER_LANES] + hi_terms[:, ROUTER_LANES:]
              + jnp.dot(h_lo, rcat_ref[:, :ROUTER_LANES], preferred_element_type=F32)
              + rbias_ref[...])
    gates, cls = _route(logits)

    @pl.when(pl.program_id(0) == 0)
    def _():
        count_scr[...] = jnp.zeros_like(count_scr)

    lane_f = lax.broadcasted_iota(jnp.int32, gates.shape, 1).astype(F32)
    onehot = jnp.where(lane_f == cls, 1.0, 0.0)
    before = jnp.dot(ltri_ref[...], onehot.astype(BF16), preferred_element_type=F32)
    seen = count_scr[...]
    rank = jnp.sum(onehot * (before + seen), axis=-1, keepdims=True)
    count_scr[...] = seen + jnp.sum(onehot, axis=0, keepdims=True)
    counts_ref[...] = count_scr[...]
    xg_ref[:, D_MODEL:] = (gates + jnp.where(lane_f == float(CLASS_LANE), cls, 0.0)
                           + jnp.where(lane_f == float(RANK_LANE), rank, 0.0))


def _merge(x2, y_attn, z, gain1, w_gate, b_gate, w_glu, b_glu, w_branch, w_out, ffn_gain,
           w_group_router, group_bias, w_expert_router, expert_bias, tm):
    t = x2.shape[0]
    pad = ROUTER_LANES - N_EXPERTS - N_GROUPS
    w_r = jnp.concatenate([w_expert_router, w_group_router,
                           jnp.zeros((D_MODEL, pad), F32)], axis=1)
    r_cat = jnp.concatenate(_split_bf16(w_r), axis=1)
    r_bias = jnp.concatenate([expert_bias, group_bias, jnp.zeros((pad,), F32)])[None, :]
    ltri = jnp.tril(jnp.ones((tm, tm), F32), -1).astype(BF16)
    row = lambda n: pl.BlockSpec((tm, n), lambda i: (i, 0))
    resident = lambda shape: pl.BlockSpec(shape, lambda i: (0, 0), pipeline_mode=pl.Buffered(1))
    return pl.pallas_call(
        _merge_kernel,
        grid=(t // tm,),
        in_specs=[row(D_MODEL), row(D_ATTN), row(D_SSM),
                  resident((1, D_MODEL)), resident((D_MODEL, N_BRANCH * D_MODEL)),
                  resident((1, N_BRANCH * D_MODEL)),
                  resident((D_SSM, D_SSM)), resident((1, D_SSM)),
                  resident((D_ATTN, D_MODEL)), resident((D_SSM, D_MODEL)),
                  resident((D_MODEL, D_MODEL)), resident((1, D_MODEL)),
                  resident((D_MODEL, 2 * ROUTER_LANES)),
                  resident((1, ROUTER_LANES)), resident((tm, tm))],
        out_specs=[row(D_ROUTED), pl.BlockSpec((1, ROUTER_LANES), lambda i: (0, 0))],
        out_shape=[jax.ShapeDtypeStruct((t, D_ROUTED), F32),
                   jax.ShapeDtypeStruct((1, ROUTER_LANES), F32)],
        scratch_shapes=[pltpu.VMEM((1, ROUTER_LANES), F32)],
        compiler_params=_compiler_params(1),
        name="merge",
    )(x2, y_attn, z, gain1[None, :], w_gate.astype(BF16), b_gate[None, :],
      w_glu.astype(BF16), b_glu[None, :],
      w_branch[:D_ATTN].astype(BF16), w_branch[D_ATTN:].astype(BF16),
      w_out.astype(BF16), ffn_gain[None, :], r_cat, r_bias, ltri)


def _moe_plan(routed, counts, n_tiles):
    cnt = counts[0, :N_CLASSES].astype(jnp.int32)
    padded = (cnt + (MOE_TILE - 1)) // MOE_TILE * MOE_TILE
    ends = jnp.cumsum(padded)
    starts = ends - padded
    classes = jnp.arange(N_CLASSES, dtype=jnp.int32)
    cls = routed[:, D_MODEL + CLASS_LANE].astype(jnp.int32)
    rank = routed[:, D_MODEL + RANK_LANE].astype(jnp.int32)
    dest = rank + jnp.sum(jnp.where(cls[:, None] == classes[None, :], starts[None, :], 0), axis=1)
    n_used = ends[-1] // MOE_TILE
    tile_end = ends // MOE_TILE
    tiles = jnp.arange(n_tiles, dtype=jnp.int32)
    tile_cls = jnp.minimum(jnp.sum(tiles[:, None] >= tile_end[None, :], axis=1), N_CLASSES - 1)
    pick = lambda table: jnp.sum(
        jnp.where(tile_cls[:, None] == classes[None, :], jnp.asarray(table, jnp.int32)[None, :], 0),
        axis=1).astype(jnp.int32)
    tile_ea = pick([e for e, _ in _CLASS_EXPERTS])
    tile_eb = pick([e for _, e in _CLASS_EXPERTS])
    clear_a = jnp.where(padded > 0, tile_end - 1, -1)
    tail = n_used + classes
    clear_b = jnp.where(tail < n_tiles, tail, -1)
    clear = jnp.concatenate([clear_a, clear_b]).astype(jnp.int32)
    return dest.astype(jnp.int32), tile_ea, tile_eb, jnp.reshape(n_used, (1,)).astype(jnp.int32), clear


def _row_copy(src, src_row, dst, dst_row, sem):
    return pltpu.make_async_copy(src.at[pl.ds(src_row, 1), :], dst.at[pl.ds(dst_row, 1), :], sem)


def _dispatch_kernel(clear_ref, dest_ref, src_ref, dst_hbm, zero_scr, sem, *, rows):
    step = pl.program_id(0)

    def clear_copy(j):
        start = pl.multiple_of(clear_ref[j] * MOE_TILE, MOE_TILE)
        return pltpu.make_async_copy(zero_scr, dst_hbm.at[pl.ds(start, MOE_TILE), :], sem)

    @pl.when(step == 0)
    def _():
        zero_scr[...] = jnp.zeros_like(zero_scr)
        for wait in (False, True):
            for j in range(2 * N_CLASSES):
                @pl.when(clear_ref[j] >= 0)
                def _():
                    clear_copy(j).wait() if wait else clear_copy(j).start()

    for r in range(rows):
        _row_copy(src_ref, r, dst_hbm, dest_ref[r], sem).start(priority=r % DMA_THREADS)
    for r in range(rows):
        _row_copy(src_ref, r, dst_hbm, 0, sem).wait()


def _dispatch(routed, dest, clear, n_tiles, rows):
    t, width = routed.shape
    return pl.pallas_call(
        functools.partial(_dispatch_kernel, rows=rows),
        grid_spec=pltpu.PrefetchScalarGridSpec(
            num_scalar_prefetch=1,
            grid=(t // rows,),
            in_specs=[pl.BlockSpec((rows,), lambda i, clear: (i,), memory_space=pltpu.SMEM),
                      pl.BlockSpec((rows, width), lambda i, clear: (i, 0))],
            out_specs=pl.BlockSpec(memory_space=pl.ANY),
            scratch_shapes=[pltpu.VMEM((MOE_TILE, width), F32), pltpu.SemaphoreType.DMA(())]),
        out_shape=jax.ShapeDtypeStruct((n_tiles * MOE_TILE, width), F32),
        compiler_params=_compiler_params(1),
        name="moe_dispatch",
    )(clear, dest, routed)


def _expert_kernel(ea_ref, eb_ref, used_ref, xs_ref, gain_ref, wga_ref, wua_ref, wda_ref,
                   wgb_ref, wub_ref, wdb_ref, o_ref):
    i = pl.program_id(0)

    @pl.when(i < used_ref[0])
    def _():
        x1 = xs_ref[:, :D_MODEL]
        gates = xs_ref[:, D_MODEL:]
        h = _rms_norm(x1, gain_ref[...]).astype(BF16)
        lane = lax.broadcasted_iota(jnp.int32, gates.shape, 1)
        acc = x1
        for e_ref, wg_ref, wu_ref, wd_ref in ((ea_ref, wga_ref, wua_ref, wda_ref),
                                              (eb_ref, wgb_ref, wub_ref, wdb_ref)):
            gate = jnp.sum(jnp.where(lane == e_ref[i], gates, 0.0), axis=-1, keepdims=True)
            a = jnp.dot(h, wg_ref[...], preferred_element_type=F32)
            u = jnp.dot(h, wu_ref[...], preferred_element_type=F32)
            hid = (jax.nn.silu(a) * u * gate).astype(BF16)
            acc = acc + jnp.dot(hid, wd_ref[...], preferred_element_type=F32)
        o_ref[...] = acc

    @pl.when(i >= used_ref[0])
    def _():
        o_ref[...] = jnp.zeros_like(o_ref)


def _experts(xs, tile_ea, tile_eb, n_used, gain, w_e_gate, w_e_up, w_e_down):
    n_tiles = xs.shape[0] // MOE_TILE
    by_a = lambda r, c: pl.BlockSpec((None, r, c), lambda i, ea, eb, used: (ea[i], 0, 0))
    by_b = lambda r, c: pl.BlockSpec((None, r, c), lambda i, ea, eb, used: (eb[i], 0, 0))
    wg, wu, wd = (w.astype(BF16) for w in (w_e_gate, w_e_up, w_e_down))
    return pl.pallas_call(
        _expert_kernel,
        grid_spec=pltpu.PrefetchScalarGridSpec(
            num_scalar_prefetch=3,
            grid=(n_tiles,),
            in_specs=[pl.BlockSpec((MOE_TILE, D_ROUTED),
                                   lambda i, ea, eb, used: (jnp.minimum(i, used[0] - 1), 0)),
                      pl.BlockSpec((1, D_MODEL), lambda i, ea, eb, used: (0, 0)),
                      by_a(D_MODEL, D_EXPERT), by_a(D_MODEL, D_EXPERT), by_a(D_EXPERT, D_MODEL),
                      by_b(D_MODEL, D_EXPERT), by_b(D_MODEL, D_EXPERT), by_b(D_EXPERT, D_MODEL)],
            out_specs=pl.BlockSpec((MOE_TILE, D_MODEL), lambda i, ea, eb, used: (i, 0))),
        out_shape=jax.ShapeDtypeStruct((n_tiles * MOE_TILE, D_MODEL), F32),
        compiler_params=_compiler_params(1),
        name="moe_experts",
    )(tile_ea, tile_eb, n_used, xs, gain[None, :], wg, wu, wd, wg, wu, wd)


def _unsort_kernel(dest_ref, src_hbm, dst_ref, sem, *, rows):
    for r in range(rows):
        _row_copy(src_hbm, dest_ref[r], dst_ref, r, sem).start(priority=r % DMA_THREADS)
    for r in range(rows):
        _row_copy(src_hbm, 0, dst_ref, r, sem).wait()


def _unsort(xo, dest, rows):
    t = dest.shape[0]
    return pl.pallas_call(
        functools.partial(_unsort_kernel, rows=rows),
        grid=(t // rows,),
        in_specs=[pl.BlockSpec((rows,), lambda i: (i,), memory_space=pltpu.SMEM),
                  pl.BlockSpec(memory_space=pl.ANY)],
        out_specs=pl.BlockSpec((rows, xo.shape[1]), lambda i: (i, 0)),
        out_shape=jax.ShapeDtypeStruct((t, xo.shape[1]), F32),
        scratch_shapes=[pltpu.SemaphoreType.DMA(())],
        compiler_params=_compiler_params(1),
        name="moe_unsort",
    )(dest, xo)


def _moe(routed, counts, ffn_gain, w_e_gate, w_e_up, w_e_down, rows):
    t = routed.shape[0]
    assert t % MOE_TILE == 0 and t % rows == 0
    n_tiles = t // MOE_TILE + N_CLASSES
    dest, tile_ea, tile_eb, n_used, clear = _moe_plan(routed, counts, n_tiles)
    xs = _dispatch(routed, dest, clear, n_tiles, rows)
    xo = _experts(xs, tile_ea, tile_eb, n_used, ffn_gain, w_e_gate, w_e_up, w_e_down)
    return _unsort(xo, dest, rows)


def kernel(x, mix_norm_gain, w_in, b_gate, q_gain, k_gain, rel_bias, ssm_lambda_re, ssm_lambda_im, ssm_log_step, ssm_b_re, ssm_b_im, ssm_c_re, ssm_c_im, ssm_d, w_glu, b_glu, w_branch, w_out, ffn_norm_gain, w_group_router, group_bias, w_expert_router, expert_bias, w_e_gate, w_e_up, w_e_down):
    bsz, seq, _ = x.shape
    assert seq % (SSM_KBLOCK * SSM_CHUNK) == 0 and seq % (2 * CHUNK) == 0
    depth = w_in.shape[0]
    tiles = _tiles(bsz * seq)
    x2 = x.reshape(bsz * seq, D_MODEL)
    for i in range(depth):
        w_qkv = w_in[i][:, :D_QKV]
        w_u = w_in[i][:, D_QKV:D_QKV + D_SSM]
        w_gate = w_in[i][:, D_QKV + D_SSM:]
        y_attn = _attention(x2, mix_norm_gain[i], w_qkv, q_gain[i], k_gain[i],
                            _rel_bias_table(rel_bias[i]), bsz, seq)
        u_t = _ssm_inproj(x2, mix_norm_gain[i], w_u, bsz, seq)
        tables = _ssm_tables(ssm_lambda_re[i], ssm_lambda_im[i], ssm_log_step[i],
                             ssm_b_re[i], ssm_b_im[i], ssm_c_re[i], ssm_c_im[i], ssm_d[i])
        z = _ssm_out(_ssm(u_t, tables, bsz), bsz, seq)
        routed, counts = _merge(x2, y_attn, z, mix_norm_gain[i], w_gate, b_gate[i], w_glu[i],
                                b_glu[i], w_branch[i], w_out[i], ffn_norm_gain[i],
                                w_group_router[i], group_bias[i], w_expert_router[i],
                                expert_bias[i], tiles["merge"])
        x2 = _moe(routed, counts, ffn_norm_gain[i], w_e_gate[i], w_e_up[i], w_e_down[i],
                  tiles["moe_rows"])
    return x2.reshape(bsz, seq, D_MODEL)
```

```python
import functools

import jax
import jax.numpy as jnp
from jax import lax
from jax.experimental import pallas as pl
from jax.experimental.pallas import tpu as pltpu

F32 = jnp.float32
BF16 = jnp.bfloat16

D_MODEL = 1024
CHUNK = 64
LEFT_CHUNKS = 8
BAND = LEFT_CHUNKS + 1
BAND_KEYS = BAND * CHUNK
PAD_KEYS = LEFT_CHUNKS * CHUNK
ATTN_HEADS = 8
HEAD_DIM = 64
D_ATTN = ATTN_HEADS * HEAD_DIM
MAX_REL = 256
REL_FUTURE = CHUNK - 1
D_SSM = D_MODEL // 2
SSM_GROUP = 16
SSM_GROUPS = D_SSM // SSM_GROUP
SSM_STATE = 64
N_BRANCH = 2
D_QKV = 3 * D_ATTN
D_IN = D_QKV + D_SSM + N_BRANCH * D_MODEL
N_GROUPS = 4
EXPERTS_PER_GROUP = 4
N_EXPERTS = N_GROUPS * EXPERTS_PER_GROUP
D_EXPERT = D_MODEL // 4
EPS = 1e-6
NEG_INF = -1e30
LOG2_E = 1.4426950408889634

LANES = 128
HEAD_PAIRS = D_ATTN // LANES
PROJ_ROWS = 512
EVEN_SUM_LANE = HEAD_DIM
ODD_SUM_LANE = 0
SSM_CHUNK = 16
SSM_ROW = SSM_CHUNK * SSM_GROUP
SSM_KBLOCK = 16
SSM_BGROUP = LANES // SSM_KBLOCK
ROUTER_LANES = LANES
D_ROUTED = D_MODEL + ROUTER_LANES
CLASS_LANE = 32
RANK_LANE = 33
MOE_TILE = 256
DMA_THREADS = 2
_PAIRS = [(a, b) for a in range(EXPERTS_PER_GROUP) for b in range(a + 1, EXPERTS_PER_GROUP)]
PAIRS_PER_GROUP = len(_PAIRS)
N_CLASSES = N_GROUPS * PAIRS_PER_GROUP
_CLASS_EXPERTS = [(g * EXPERTS_PER_GROUP + a, g * EXPERTS_PER_GROUP + b)
                  for g in range(N_GROUPS) for a, b in _PAIRS]
VMEM_LIMIT = 56 * 1024 * 1024


def _tiles(n_tokens):
    return dict(merge=512, moe_rows=512)


def _compiler_params(n_axes):
    return pltpu.CompilerParams(
        dimension_semantics=("arbitrary",) * n_axes,
        vmem_limit_bytes=VMEM_LIMIT)


def _full(shape):
    nd = len(shape)
    return pl.BlockSpec(shape, lambda *_: (0,) * nd)


def _rms_norm(x, gain):
    ms = jnp.mean(x * x, axis=-1, keepdims=True)
    return x * lax.rsqrt(ms + EPS) * gain


NT_DIMS = (((1,), (1,)), ((), ()))


def _attn_kernel(x_ref, gain_ref, w_ref, qg_ref, kg_ref, ones_ref, bias_ref, o_ref,
                 q_scr, kpad, v_even, v_odd, s_a, s_b):
    seq = x_ref.shape[0]
    n_chunks = seq // CHUNK
    zeros = jnp.zeros((PAD_KEYS, D_ATTN), BF16)
    kpad[0:PAD_KEYS, :] = zeros
    v_even[0:PAD_KEYS, :] = zeros
    v_odd[0:PAD_KEYS, :] = zeros

    lane_v = lax.broadcasted_iota(jnp.int32, (PROJ_ROWS, D_ATTN), 1)
    even_v = (lane_v % LANES) < HEAD_DIM
    ones_even = (lane_v % LANES) == EVEN_SUM_LANE
    ones_odd = (lane_v % LANES) == ODD_SUM_LANE
    for c in range(seq // PROJ_ROWS):
        rows = slice(c * PROJ_ROWS, (c + 1) * PROJ_ROWS)
        kv_rows = slice(PAD_KEYS + c * PROJ_ROWS, PAD_KEYS + (c + 1) * PROJ_ROWS)
        h = _rms_norm(x_ref[rows, :], gain_ref[...]).astype(BF16)

        def proj(lo):
            return jnp.dot(h, w_ref[:, lo:lo + D_ATTN], preferred_element_type=F32)

        def head_norm(t, g_ref):
            ss = jnp.dot((t * t).astype(BF16), ones_ref[...], preferred_element_type=F32)
            return t * lax.rsqrt(ss * (1.0 / HEAD_DIM) + EPS) * g_ref[...]

        q_scr[rows, :] = head_norm(proj(0), qg_ref).astype(BF16)
        kpad[kv_rows, :] = head_norm(proj(D_ATTN), kg_ref).astype(BF16)
        v = proj(2 * D_ATTN).astype(BF16)
        v_even[kv_rows, :] = jnp.where(even_v, v, jnp.where(ones_even, 1.0, 0.0).astype(BF16))
        v_odd[kv_rows, :] = jnp.where(even_v, jnp.where(ones_odd, 1.0, 0.0).astype(BF16), v)

    lane = lax.broadcasted_iota(jnp.int32, (CHUNK, LANES), 1)
    first_head = lane < HEAD_DIM

    def scores(n, s_buf, skip=0):
        r0 = n * CHUNK if isinstance(n, int) else pl.multiple_of(n * CHUNK, CHUNK)
        for j in range(HEAD_PAIRS):
            cols = slice(j * LANES, (j + 1) * LANES)
            qp = q_scr[pl.ds(r0, CHUNK), cols]
            zq = jnp.zeros_like(qp)
            q2 = jnp.concatenate([jnp.where(first_head, qp, zq), jnp.where(first_head, zq, qp)], axis=0)
            kb = kpad[pl.ds(r0 + skip, BAND_KEYS - skip), cols]
            s_buf[j * 2 * CHUNK:(j + 1) * 2 * CHUNK, skip:] = lax.dot_general(
                q2, kb, NT_DIMS, preferred_element_type=F32)

    def softmax_pv(n, s_buf, skip=0, first_valid=0):
        r0 = n * CHUNK if isinstance(n, int) else pl.multiple_of(n * CHUNK, CHUNK)
        width = BAND_KEYS - skip
        for j in range(HEAD_PAIRS):
            cols = slice(j * LANES, (j + 1) * LANES)
            rows = slice(j * 2 * CHUNK, (j + 1) * 2 * CHUNK)
            s = s_buf[rows, skip:] + bias_ref[rows, skip:]
            if first_valid > skip:
                key_col = lax.broadcasted_iota(jnp.int32, s.shape, 1) + skip
                s = jnp.where(key_col >= first_valid, s, NEG_INF)
            m = jnp.max(s, axis=-1, keepdims=True)
            pb = jnp.exp2(s - m).astype(BF16)
            o0 = jnp.dot(pb[:CHUNK], v_even[pl.ds(r0 + skip, width), cols], preferred_element_type=F32)
            o1 = jnp.dot(pb[CHUNK:], v_odd[pl.ds(r0 + skip, width), cols], preferred_element_type=F32)
            l0 = o0[:, EVEN_SUM_LANE:EVEN_SUM_LANE + 1]
            l1 = o1[:, ODD_SUM_LANE:ODD_SUM_LANE + 1]
            o_ref[pl.ds(r0, CHUNK), cols] = jnp.where(first_head, o0 / l0, o1 / l1).astype(BF16)

    assert n_chunks % 2 == 0 and LEFT_CHUNKS % 2 == 0 and n_chunks > LEFT_CHUNKS

    def band_start(n):
        first_valid = max(LEFT_CHUNKS - n, 0) * CHUNK
        return first_valid // LANES * LANES, first_valid

    bufs = (s_a, s_b)
    scores(0, s_a, band_start(0)[0])
    for n in range(LEFT_CHUNKS):
        scores(n + 1, bufs[(n + 1) % 2], band_start(n + 1)[0])
        softmax_pv(n, bufs[n % 2], *band_start(n))

    def pair_body(i, carry):
        n0 = 2 * i
        scores(n0 + 1, s_b)
        softmax_pv(n0, s_a)
        scores(jnp.minimum(n0 + 2, n_chunks - 1), s_a)
        softmax_pv(n0 + 1, s_b)
        return carry

    lax.fori_loop(LEFT_CHUNKS // 2, n_chunks // 2, pair_body, 0)


def _attention(x2, gain, w_qkv, q_gain, k_gain, bias, bsz, seq):
    assert seq % PROJ_ROWS == 0
    ones = jnp.kron(jnp.eye(ATTN_HEADS, dtype=F32), jnp.ones((HEAD_DIM, HEAD_DIM), F32)).astype(BF16)
    qg = jnp.tile(q_gain, ATTN_HEADS)[None, :] * (HEAD_DIM ** -0.5 * LOG2_E)
    kg = jnp.tile(k_gain, ATTN_HEADS)[None, :]
    seq_spec = lambda width: pl.BlockSpec((None, seq, width), lambda b: (b, 0, 0))
    padded = pltpu.VMEM((PAD_KEYS + seq, D_ATTN), BF16)
    out = pl.pallas_call(
        _attn_kernel,
        grid=(bsz,),
        in_specs=[seq_spec(D_MODEL), _full((1, D_MODEL)), _full((D_MODEL, D_QKV)),
                  _full((1, D_ATTN)), _full((1, D_ATTN)), _full((D_ATTN, D_ATTN)),
                  _full((ATTN_HEADS * CHUNK, BAND_KEYS))],
        out_specs=seq_spec(D_ATTN),
        out_shape=jax.ShapeDtypeStruct((bsz, seq, D_ATTN), BF16),
        scratch_shapes=[pltpu.VMEM((seq, D_ATTN), BF16), padded, padded, padded,
                        pltpu.VMEM((ATTN_HEADS * CHUNK, BAND_KEYS), F32),
                        pltpu.VMEM((ATTN_HEADS * CHUNK, BAND_KEYS), F32)],
        compiler_params=_compiler_params(1),
        name="attention",
    )(x2.reshape(bsz, seq, D_MODEL), gain[None, :], w_qkv.astype(BF16), qg, kg, ones, bias)
    return out.reshape(bsz * seq, D_ATTN)


def _rel_bias_table(rel_bias):
    n_rel = REL_FUTURE + MAX_REL + 1
    far = jnp.broadcast_to(rel_bias[:, n_rel - 1:], (ATTN_HEADS, PAD_KEYS - MAX_REL + CHUNK))
    ext = jnp.concatenate([far, rel_bias[:, n_rel - 2::-1]], axis=1) * LOG2_E
    rows = [ext[:, CHUNK - 1 - q:CHUNK - 1 - q + BAND_KEYS] for q in range(CHUNK)]
    return jnp.stack(rows, axis=1).reshape(ATTN_HEADS * CHUNK, BAND_KEYS)


SSM_BLOCK_LEN = SSM_KBLOCK * SSM_CHUNK
SSM_BLOCK_TOK = SSM_BGROUP * SSM_KBLOCK


def _token_block_spec(width):
    return pl.BlockSpec((SSM_BGROUP, SSM_BLOCK_LEN, width), lambda kb, bg: (bg, kb, 0))


def _transposed_spec(n_bgroups):
    return pl.BlockSpec((SSM_GROUPS, SSM_ROW, SSM_BLOCK_TOK),
                        lambda kb, bg: (0, 0, kb * n_bgroups + bg))


def _token_grid(bsz, seq):
    assert bsz % SSM_BGROUP == 0 and seq % SSM_BLOCK_LEN == 0
    return (seq // SSM_BLOCK_LEN, bsz // SSM_BGROUP)


def _ssm_inproj_kernel(x_ref, gain_ref, wu_ref, ut_ref, u_scr):
    n_rows = SSM_BGROUP * SSM_BLOCK_LEN
    x = x_ref[...].reshape(n_rows, D_MODEL)
    h = _rms_norm(x, gain_ref[...]).astype(BF16)
    u = jnp.dot(h, wu_ref[...], preferred_element_type=F32)
    for q in range(D_SSM // LANES):
        u_scr[q] = u[:, q * LANES:(q + 1) * LANES]
    groups_per_tile = LANES // SSM_GROUP
    for t in range(SSM_CHUNK):
        for q in range(D_SSM // LANES):
            ut = u_scr[q, pl.ds(t, SSM_BLOCK_TOK, stride=SSM_CHUNK), :].T.astype(BF16)
            for gl in range(groups_per_tile):
                ut_ref[q * groups_per_tile + gl, t * SSM_GROUP:(t + 1) * SSM_GROUP, :] = (
                    ut[gl * SSM_GROUP:(gl + 1) * SSM_GROUP, :])


def _ssm_inproj(x2, gain, w_u, bsz, seq):
    grid = _token_grid(bsz, seq)
    return pl.pallas_call(
        _ssm_inproj_kernel,
        grid=grid,
        in_specs=[_token_block_spec(D_MODEL), _full((1, D_MODEL)), _full((D_MODEL, D_SSM))],
        out_specs=_transposed_spec(grid[1]),
        out_shape=jax.ShapeDtypeStruct((SSM_GROUPS, SSM_ROW, bsz * seq // SSM_CHUNK), BF16),
        scratch_shapes=[pltpu.VMEM((D_SSM // LANES, SSM_BGROUP * SSM_BLOCK_LEN, LANES), F32)],
        compiler_params=_compiler_params(2),
        name="ssm_inproj",
    )(x2.reshape(bsz, seq, D_MODEL), gain[None, :], w_u.astype(BF16))


def _ssm_out_kernel(yt_ref, z_ref, y_scr):
    groups_per_tile = LANES // SSM_GROUP
    for t in range(SSM_CHUNK):
        for q in range(D_SSM // LANES):
            yt = jnp.concatenate(
                [yt_ref[q * groups_per_tile + gl, t * SSM_GROUP:(t + 1) * SSM_GROUP, :]
                 for gl in range(groups_per_tile)], axis=0)
            y_scr[q, pl.ds(t, SSM_BLOCK_TOK, stride=SSM_CHUNK), :] = yt.T
    for q in range(D_SSM // LANES):
        z = jax.nn.gelu(y_scr[q])
        z_ref[:, :, q * LANES:(q + 1) * LANES] = z.astype(BF16).reshape(
            SSM_BGROUP, SSM_BLOCK_LEN, LANES)


def _ssm_out(y_t, bsz, seq):
    grid = _token_grid(bsz, seq)
    z = pl.pallas_call(
        _ssm_out_kernel,
        grid=grid,
        in_specs=[_transposed_spec(grid[1])],
        out_specs=_token_block_spec(D_SSM),
        out_shape=jax.ShapeDtypeStruct((bsz, seq, D_SSM), BF16),
        scratch_shapes=[pltpu.VMEM((D_SSM // LANES, SSM_BGROUP * SSM_BLOCK_LEN, LANES), F32)],
        compiler_params=_compiler_params(2),
        name="ssm_out",
    )(y_t)
    return z.reshape(bsz * seq, D_SSM)


def _ssm_tables(lambda_re, lambda_im, log_step, b_re, b_im, c_re, c_im, d_skip):
    step = jnp.exp(log_step)[:, None]
    mag = jnp.exp(lambda_re * step)
    ang = lambda_im * step
    a_re = mag * jnp.cos(ang)
    a_im = mag * jnp.sin(ang)
    num_re = a_re - 1.0
    num_im = a_im
    den = lambda_re * lambda_re + lambda_im * lambda_im
    f_re = (num_re * lambda_re + num_im * lambda_im) / den
    f_im = (num_im * lambda_re - num_re * lambda_im) / den
    bb_re = f_re[..., None] * b_re - f_im[..., None] * b_im
    bb_im = f_re[..., None] * b_im + f_im[..., None] * b_re
    log_mag = lambda_re * step

    def powers(tau):
        tau = jnp.asarray(tau, F32)[:, None, None]
        pmag = jnp.exp(tau * log_mag[None])
        return pmag * jnp.cos(tau * ang[None]), pmag * jnp.sin(tau * ang[None])

    steps = jnp.arange(SSM_CHUNK)
    up_re, up_im = powers(steps + 1)
    cp_re = c_re[None] * up_re[:, :, None, :] - c_im[None] * up_im[:, :, None, :]
    cp_im = c_re[None] * up_im[:, :, None, :] + c_im[None] * up_re[:, :, None, :]
    readout = jnp.concatenate([cp_re, -cp_im], axis=-1)
    readout = jnp.transpose(readout, (1, 0, 2, 3)).reshape(SSM_GROUPS, SSM_ROW, 2 * SSM_STATE)

    def inject(tau):
        p_re, p_im = powers(tau)
        bt_re = jnp.transpose(bb_re, (0, 2, 1))[None]
        bt_im = jnp.transpose(bb_im, (0, 2, 1))[None]
        i_re = p_re[:, :, None, :] * bt_re - p_im[:, :, None, :] * bt_im
        i_im = p_re[:, :, None, :] * bt_im + p_im[:, :, None, :] * bt_re
        both = jnp.concatenate([i_re, i_im], axis=-1)
        return jnp.transpose(both, (1, 3, 0, 2)).reshape(SSM_GROUPS, 2 * SSM_STATE, SSM_ROW)

    inject_back = inject(-(steps + 1.0))
    inject_end = inject(SSM_CHUNK - 1.0 - steps).astype(BF16)
    d_row = jnp.tile(d_skip.reshape(SSM_GROUPS, 1, SSM_GROUP), (1, 1, SSM_CHUNK))
    (ar,), (ai,) = powers([float(SSM_CHUNK)])
    a_chunk = jnp.stack([jnp.concatenate([ar, ar], -1), jnp.concatenate([-ai, ai], -1),
                         jnp.concatenate([ai, -ai], -1)], axis=1)
    return readout, inject_back, inject_end, d_row, a_chunk


def _ssm_kernel(ut_ref, ro_ref, ib_ref, ie_ref, d_ref, a_ref, yt_ref, p_scr, q_scr, s_scr, *, bsz):
    ut = ut_ref[...]
    n_tok = ut.shape[1]
    r_hi, r_lo = _split_bf16(ro_ref[...])
    i_hi, i_lo = _split_bf16(ib_ref[...])
    kern = (jnp.dot(r_hi, i_hi, preferred_element_type=F32)
            + jnp.dot(r_hi, i_lo, preferred_element_type=F32)
            + jnp.dot(r_lo, i_hi, preferred_element_type=F32))
    row = lax.broadcasted_iota(jnp.int32, kern.shape, 0)
    col = lax.broadcasted_iota(jnp.int32, kern.shape, 1)
    group_shift = SSM_GROUP.bit_length() - 1
    causal = jnp.right_shift(row, group_shift) >= jnp.right_shift(col, group_shift)
    kern = jnp.where(causal, kern, 0.0) + jnp.where(row == col, d_ref[...], 0.0)
    wt = jnp.dot(ie_ref[...], ut, preferred_element_type=F32)
    w = wt.T
    p_scr[...] = w
    q_scr[...] = pltpu.roll(w, SSM_STATE, 1)
    a1 = a_ref[0:1, :]
    a2 = a_ref[1:2, :]
    a3 = a_ref[2:3, :]

    def block(kb, carry):
        cp, cq = carry
        base = pl.multiple_of(kb * (bsz * SSM_KBLOCK), bsz * SSM_KBLOCK)
        for k in range(SSM_KBLOCK):
            rows = pl.ds(base + k, bsz, stride=SSM_KBLOCK)
            s_scr[rows, :] = cp
            cp, cq = (a1 * cp + a2 * cq + p_scr[rows, :],
                      a1 * cq + a3 * cp + q_scr[rows, :])
        return cp, cq

    zero = jnp.zeros((bsz, 2 * SSM_STATE), F32)
    lax.fori_loop(0, n_tok // (bsz * SSM_KBLOCK), block, (zero, zero))
    yt = jnp.dot(kern.astype(BF16), ut, preferred_element_type=F32)
    yt = yt + lax.dot_general(r_hi, s_scr[...].astype(BF16), NT_DIMS, preferred_element_type=F32)
    yt_ref[...] = yt


def _ssm(ut, tables, bsz):
    readout, inject_back, inject_end, d_row, a_chunk = tables
    n_tok = ut.shape[2]
    per_group = lambda a, b: pl.BlockSpec((None, a, b), lambda g: (g, 0, 0))
    state = pltpu.VMEM((n_tok, 2 * SSM_STATE), F32)
    return pl.pallas_call(
        functools.partial(_ssm_kernel, bsz=bsz),
        grid=(SSM_GROUPS,),
        in_specs=[per_group(SSM_ROW, n_tok), per_group(SSM_ROW, 2 * SSM_STATE),
                  per_group(2 * SSM_STATE, SSM_ROW), per_group(2 * SSM_STATE, SSM_ROW),
                  per_group(1, SSM_ROW), per_group(3, 2 * SSM_STATE)],
        out_specs=per_group(SSM_ROW, n_tok),
        out_shape=jax.ShapeDtypeStruct((SSM_GROUPS, SSM_ROW, n_tok), F32),
        scratch_shapes=[state, state, state],
        compiler_params=_compiler_params(1),
        name="ssm",
    )(ut, readout, inject_back, inject_end, d_row, a_chunk)


def _split_bf16(a):
    hi = a.astype(BF16)
    lo = (a - hi.astype(F32)).astype(BF16)
    return hi, lo


def _route(logits):
    lane = lax.broadcasted_iota(jnp.int32, logits.shape, 1)
    lane_f = lane.astype(F32)
    big = float(ROUTER_LANES)

    def first_argmax(vals, vmax):
        return jnp.min(jnp.where(vals == vmax, lane_f, big), axis=-1, keepdims=True)

    g_mask = (lane >= N_EXPERTS) & (lane < N_EXPERTS + N_GROUPS)
    g_log = jnp.where(g_mask, logits, NEG_INF)
    g_max = jnp.max(g_log, axis=-1, keepdims=True)
    g_sum = jnp.sum(jnp.where(g_mask, jnp.exp(g_log - g_max), 0.0), axis=-1, keepdims=True)
    g_prob = 1.0 / g_sum
    g_idx = first_argmax(g_log, g_max) - float(N_EXPERTS)
    e_lo = g_idx * float(EXPERTS_PER_GROUP)
    e_mask = (lane_f >= e_lo) & (lane_f < e_lo + float(EXPERTS_PER_GROUP))
    e_log = jnp.where(e_mask, logits, NEG_INF)
    e_max1 = jnp.max(e_log, axis=-1, keepdims=True)
    idx1 = first_argmax(e_log, e_max1)
    e_log2 = jnp.where(lane_f == idx1, NEG_INF, e_log)
    e_max2 = jnp.max(e_log2, axis=-1, keepdims=True)
    idx2 = first_argmax(e_log2, e_max2)
    r = jnp.exp(e_max2 - e_max1)
    w1 = g_prob / (1.0 + r)
    w2 = g_prob * r / (1.0 + r)
    gates = jnp.where(lane_f == idx1, w1, 0.0) + jnp.where(lane_f == idx2, w2, 0.0)
    lo = jnp.minimum(idx1, idx2) - e_lo
    hi = jnp.maximum(idx1, idx2) - e_lo
    pair = lo * (2.0 * EXPERTS_PER_GROUP - 1.0 - lo) * 0.5 + (hi - lo - 1.0)
    return gates, g_idx * float(PAIRS_PER_GROUP) + pair


def _merge_kernel(x_ref, ya_ref, z_ref, gain1_ref, wgate_ref, bgate_ref, wglu_ref, bglu_ref,
                  wba_ref, wbs_ref, wout_ref, gain2_ref, rcat_ref, rbias_ref, ltri_ref,
                  xg_ref, counts_ref, count_scr):
    x = x_ref[...]
    hn = _rms_norm(x, gain1_ref[...]).astype(BF16)
    z = z_ref[...]
    glu = jnp.dot(z, wglu_ref[...], preferred_element_type=F32) + bglu_ref[...]
    y_ssm = (z.astype(F32) * jax.nn.sigmoid(glu)).astype(BF16)
    merged = None
    for s, (y_b, wb_ref) in enumerate(((ya_ref[...], wba_ref), (y_ssm, wbs_ref))):
        cols = slice(s * D_MODEL, (s + 1) * D_MODEL)
        gate = jax.nn.sigmoid(jnp.dot(hn, wgate_ref[:, cols], preferred_element_type=F32)
                              + bgate_ref[:, cols])
        term = gate * jnp.dot(y_b, wb_ref[...], preferred_element_type=F32)
        merged = term if merged is None else merged + term
    x1 = x + jnp.dot(merged.astype(BF16), wout_ref[...], preferred_element_type=F32)
    xg_ref[:, :D_MODEL] = x1

    h = _rms_norm(x1, gain2_ref[...])
    h_hi, h_lo = _split_bf16(h)
    hi_terms = jnp.dot(h_hi, rcat_ref[...], preferred_element_type=F32)
    logits = (hi_terms[:, :ROUTER_LANES] + hi_terms[:, ROUTER_LANES:]
              + jnp.dot(h_lo, rcat_ref[:, :ROUTER_LANES], preferred_element_type=F32)
              + rbias_ref[...])
    gates, cls = _route(logits)

    @pl.when(pl.program_id(0) == 0)
    def _():
        count_scr[...] = jnp.zeros_like(count_scr)

    lane_f = lax.broadcasted_iota(jnp.int32, gates.shape, 1).astype(F32)
    onehot = jnp.where(lane_f == cls, 1.0, 0.0)
    before = jnp.dot(ltri_ref[...], onehot.astype(BF16), preferred_element_type=F32)
    seen = count_scr[...]
    rank = jnp.sum(onehot * (before + seen), axis=-1, keepdims=True)
    count_scr[...] = seen + jnp.sum(onehot, axis=0, keepdims=True)
    counts_ref[...] = count_scr[...]
    xg_ref[:, D_MODEL:] = (gates + jnp.where(lane_f == float(CLASS_LANE), cls, 0.0)
                           + jnp.where(lane_f == float(RANK_LANE), rank, 0.0))


def _merge(x2, y_attn, z, gain1, w_gate, b_gate, w_glu, b_glu, w_branch, w_out, ffn_gain,
           w_group_router, group_bias, w_expert_router, expert_bias, tm):
    t = x2.shape[0]
    pad = ROUTER_LANES - N_EXPERTS - N_GROUPS
    w_r = jnp.concatenate([w_expert_router, w_group_router,
                           jnp.zeros((D_MODEL, pad), F32)], axis=1)
    r_cat = jnp.concatenate(_split_bf16(w_r), axis=1)
    r_bias = jnp.concatenate([expert_bias, group_bias, jnp.zeros((pad,), F32)])[None, :]
    ltri = jnp.tril(jnp.ones((tm, tm), F32), -1).astype(BF16)
    row = lambda n: pl.BlockSpec((tm, n), lambda i: (i, 0))
    resident = lambda shape: pl.BlockSpec(shape, lambda i: (0, 0), pipeline_mode=pl.Buffered(1))
    return pl.pallas_call(
        _merge_kernel,
        grid=(t // tm,),
        in_specs=[row(D_MODEL), row(D_ATTN), row(D_SSM),
                  resident((1, D_MODEL)), resident((D_MODEL, N_BRANCH * D_MODEL)),
                  resident((1, N_BRANCH * D_MODEL)),
                  resident((D_SSM, D_SSM)), resident((1, D_SSM)),
                  resident((D_ATTN, D_MODEL)), resident((D_SSM, D_MODEL)),
                  resident((D_MODEL, D_MODEL)), resident((1, D_MODEL)),
                  resident((D_MODEL, 2 * ROUTER_LANES)),
                  resident((1, ROUTER_LANES)), resident((tm, tm))],
        out_specs=[row(D_ROUTED), pl.BlockSpec((1, ROUTER_LANES), lambda i: (0, 0))],
        out_shape=[jax.ShapeDtypeStruct((t, D_ROUTED), F32),
                   jax.ShapeDtypeStruct((1, ROUTER_LANES), F32)],
        scratch_shapes=[pltpu.VMEM((1, ROUTER_LANES), F32)],
        compiler_params=_compiler_params(1),
        name="merge",
    )(x2, y_attn, z, gain1[None, :], w_gate.astype(BF16), b_gate[None, :],
      w_glu.astype(BF16), b_glu[None, :],
      w_branch[:D_ATTN].astype(BF16), w_branch[D_ATTN:].astype(BF16),
      w_out.astype(BF16), ffn_gain[None, :], r_cat, r_bias, ltri)


def _moe_plan(routed, counts, n_tiles):
    cnt = counts[0, :N_CLASSES].astype(jnp.int32)
    padded = (cnt + (MOE_TILE - 1)) // MOE_TILE * MOE_TILE
    ends = jnp.cumsum(padded)
    starts = ends - padded
    classes = jnp.arange(N_CLASSES, dtype=jnp.int32)
    cls = routed[:, D_MODEL + CLASS_LANE].astype(jnp.int32)
    rank = routed[:, D_MODEL + RANK_LANE].astype(jnp.int32)
    dest = rank + jnp.sum(jnp.where(cls[:, None] == classes[None, :], starts[None, :], 0), axis=1)
    n_used = ends[-1] // MOE_TILE
    tile_end = ends // MOE_TILE
    tiles = jnp.arange(n_tiles, dtype=jnp.int32)
    tile_cls = jnp.minimum(jnp.sum(tiles[:, None] >= tile_end[None, :], axis=1), N_CLASSES - 1)
    pick = lambda table: jnp.sum(
        jnp.where(tile_cls[:, None] == classes[None, :], jnp.asarray(table, jnp.int32)[None, :], 0),
        axis=1).astype(jnp.int32)
    tile_ea = pick([e for e, _ in _CLASS_EXPERTS])
    tile_eb = pick([e for _, e in _CLASS_EXPERTS])
    clear_a = jnp.where(padded > 0, tile_end - 1, -1)
    tail = n_used + classes
    clear_b = jnp.where(tail < n_tiles, tail, -1)
    clear = jnp.concatenate([clear_a, clear_b]).astype(jnp.int32)
    return dest.astype(jnp.int32), tile_ea, tile_eb, jnp.reshape(n_used, (1,)).astype(jnp.int32), clear


def _row_copy(src, src_row, dst, dst_row, sem):
    return pltpu.make_async_copy(src.at[pl.ds(src_row, 1), :], dst.at[pl.ds(dst_row, 1), :], sem)


def _dispatch_kernel(clear_ref, dest_ref, src_ref, dst_hbm, zero_scr, sem, *, rows):
    step = pl.program_id(0)

    def clear_copy(j):
        start = pl.multiple_of(clear_ref[j] * MOE_TILE, MOE_TILE)
        return pltpu.make_async_copy(zero_scr, dst_hbm.at[pl.ds(start, MOE_TILE), :], sem)

    @pl.when(step == 0)
    def _():
        zero_scr[...] = jnp.zeros_like(zero_scr)
        for wait in (False, True):
            for j in range(2 * N_CLASSES):
                @pl.when(clear_ref[j] >= 0)
                def _():
                    clear_copy(j).wait() if wait else clear_copy(j).start()

    for r in range(rows):
        _row_copy(src_ref, r, dst_hbm, dest_ref[r], sem).start(priority=r % DMA_THREADS)
    for r in range(rows):
        _row_copy(src_ref, r, dst_hbm, 0, sem).wait()


def _dispatch(routed, dest, clear, n_tiles, rows):
    t, width = routed.shape
    return pl.pallas_call(
        functools.partial(_dispatch_kernel, rows=rows),
        grid_spec=pltpu.PrefetchScalarGridSpec(
            num_scalar_prefetch=1,
            grid=(t // rows,),
            in_specs=[pl.BlockSpec((rows,), lambda i, clear: (i,), memory_space=pltpu.SMEM),
                      pl.BlockSpec((rows, width), lambda i, clear: (i, 0))],
            out_specs=pl.BlockSpec(memory_space=pl.ANY),
            scratch_shapes=[pltpu.VMEM((MOE_TILE, width), F32), pltpu.SemaphoreType.DMA(())]),
        out_shape=jax.ShapeDtypeStruct((n_tiles * MOE_TILE, width), F32),
        compiler_params=_compiler_params(1),
        name="moe_dispatch",
    )(clear, dest, routed)


def _expert_kernel(ea_ref, eb_ref, used_ref, xs_ref, gain_ref, wga_ref, wua_ref, wda_ref,
                   wgb_ref, wub_ref, wdb_ref, o_ref):
    i = pl.program_id(0)

    @pl.when(i < used_ref[0])
    def _():
        x1 = xs_ref[:, :D_MODEL]
        gates = xs_ref[:, D_MODEL:]
        h = _rms_norm(x1, gain_ref[...]).astype(BF16)
        lane = lax.broadcasted_iota(jnp.int32, gates.shape, 1)
        acc = x1
        for e_ref, wg_ref, wu_ref, wd_ref in ((ea_ref, wga_ref, wua_ref, wda_ref),
                                              (eb_ref, wgb_ref, wub_ref, wdb_ref)):
            gate = jnp.sum(jnp.where(lane == e_ref[i], gates, 0.0), axis=-1, keepdims=True)
            a = jnp.dot(h, wg_ref[...], preferred_element_type=F32)
            u = jnp.dot(h, wu_ref[...], preferred_element_type=F32)
            hid = (jax.nn.silu(a) * u * gate).astype(BF16)
            acc = acc + jnp.dot(hid, wd_ref[...], preferred_element_type=F32)
        o_ref[...] = acc

    @pl.when(i >= used_ref[0])
    def _():
        o_ref[...] = jnp.zeros_like(o_ref)


def _experts(xs, tile_ea, tile_eb, n_used, gain, w_e_gate, w_e_up, w_e_down):
    n_tiles = xs.shape[0] // MOE_TILE
    by_a = lambda r, c: pl.BlockSpec((None, r, c), lambda i, ea, eb, used: (ea[i], 0, 0))
    by_b = lambda r, c: pl.BlockSpec((None, r, c), lambda i, ea, eb, used: (eb[i], 0, 0))
    wg, wu, wd = (w.astype(BF16) for w in (w_e_gate, w_e_up, w_e_down))
    return pl.pallas_call(
        _expert_kernel,
        grid_spec=pltpu.PrefetchScalarGridSpec(
            num_scalar_prefetch=3,
            grid=(n_tiles,),
            in_specs=[pl.BlockSpec((MOE_TILE, D_ROUTED),
                                   lambda i, ea, eb, used: (jnp.minimum(i, used[0] - 1), 0)),
                      pl.BlockSpec((1, D_MODEL), lambda i, ea, eb, used: (0, 0)),
                      by_a(D_MODEL, D_EXPERT), by_a(D_MODEL, D_EXPERT), by_a(D_EXPERT, D_MODEL),
                      by_b(D_MODEL, D_EXPERT), by_b(D_MODEL, D_EXPERT), by_b(D_EXPERT, D_MODEL)],
            out_specs=pl.BlockSpec((MOE_TILE, D_MODEL), lambda i, ea, eb, used: (i, 0))),
        out_shape=jax.ShapeDtypeStruct((n_tiles * MOE_TILE, D_MODEL), F32),
        compiler_params=_compiler_params(1),
        name="moe_experts",
    )(tile_ea, tile_eb, n_used, xs, gain[None, :], wg, wu, wd, wg, wu, wd)


def _unsort_kernel(dest_ref, src_hbm, dst_ref, sem, *, rows):
    for r in range(rows):
        _row_copy(src_hbm, dest_ref[r], dst_ref, r, sem).start(priority=r % DMA_THREADS)
    for r in range(rows):
        _row_copy(src_hbm, 0, dst_ref, r, sem).wait()


def _unsort(xo, dest, rows):
    t = dest.shape[0]
    return pl.pallas_call(
        functools.partial(_unsort_kernel, rows=rows),
        grid=(t // rows,),
        in_specs=[pl.BlockSpec((rows,), lambda i: (i,), memory_space=pltpu.SMEM),
                  pl.BlockSpec(memory_space=pl.ANY)],
        out_specs=pl.BlockSpec((rows, xo.shape[1]), lambda i: (i, 0)),
        out_shape=jax.ShapeDtypeStruct((t, xo.shape[1]), F32),
        scratch_shapes=[pltpu.SemaphoreType.DMA(())],
        compiler_params=_compiler_params(1),
        name="moe_unsort",
    )(dest, xo)


def _moe(routed, counts, ffn_gain, w_e_gate, w_e_up, w_e_down, rows):
    t = routed.shape[0]
    assert t % MOE_TILE == 0 and t % rows == 0
    n_tiles = t // MOE_TILE + N_CLASSES
    dest, tile_ea, tile_eb, n_used, clear = _moe_plan(routed, counts, n_tiles)
    xs = _dispatch(routed, dest, clear, n_tiles, rows)
    xo = _experts(xs, tile_ea, tile_eb, n_used, ffn_gain, w_e_gate, w_e_up, w_e_down)
    return _unsort(xo, dest, rows)


def kernel(x, mix_norm_gain, w_in, b_gate, q_gain, k_gain, rel_bias, ssm_lambda_re, ssm_lambda_im, ssm_log_step, ssm_b_re, ssm_b_im, ssm_c_re, ssm_c_im, ssm_d, w_glu, b_glu, w_branch, w_out, ffn_norm_gain, w_group_router, group_bias, w_expert_router, expert_bias, w_e_gate, w_e_up, w_e_down):
    bsz, seq, _ = x.shape
    assert seq % (SSM_KBLOCK * SSM_CHUNK) == 0 and seq % (2 * CHUNK) == 0
    depth = w_in.shape[0]
    tiles = _tiles(bsz * seq)
    x2 = x.reshape(bsz * seq, D_MODEL)
    for i in range(depth):
        w_qkv = w_in[i][:, :D_QKV]
        w_u = w_in[i][:, D_QKV:D_QKV + D_SSM]
        w_gate = w_in[i][:, D_QKV + D_SSM:]
        y_attn = _attention(x2, mix_norm_gain[i], w_qkv, q_gain[i], k_gain[i],
                            _rel_bias_table(rel_bias[i]), bsz, seq)
        u_t = _ssm_inproj(x2, mix_norm_gain[i], w_u, bsz, seq)
        tables = _ssm_tables(ssm_lambda_re[i], ssm_lambda_im[i], ssm_log_step[i],
                             ssm_b_re[i], ssm_b_im[i], ssm_c_re[i], ssm_c_im[i], ssm_d[i])
        z = _ssm_out(_ssm(u_t, tables, bsz), bsz, seq)
        routed, counts = _merge(x2, y_attn, z, mix_norm_gain[i], w_gate, b_gate[i], w_glu[i],
                                b_glu[i], w_branch[i], w_out[i], ffn_norm_gain[i],
                                w_group_router[i], group_bias[i], w_expert_router[i],
                                expert_bias[i], tiles["merge"])
        x2 = _moe(routed, counts, ffn_norm_gain[i], w_e_gate[i], w_e_up[i], w_e_down[i],
                  tiles["moe_rows"])
    return x2.reshape(bsz, seq, D_MODEL)
```

```python
import functools

import jax
import jax.numpy as jnp
from jax import lax
from jax.experimental import pallas as pl
from jax.experimental.pallas import tpu as pltpu

F32 = jnp.float32
BF16 = jnp.bfloat16

D_MODEL = 1024
CHUNK = 64
LEFT_CHUNKS = 8
BAND = LEFT_CHUNKS + 1
BAND_KEYS = BAND * CHUNK
PAD_KEYS = LEFT_CHUNKS * CHUNK
ATTN_HEADS = 8
HEAD_DIM = 64
D_ATTN = ATTN_HEADS * HEAD_DIM
MAX_REL = 256
REL_FUTURE = CHUNK - 1
D_SSM = D_MODEL // 2
SSM_GROUP = 16
SSM_GROUPS = D_SSM // SSM_GROUP
SSM_STATE = 64
N_BRANCH = 2
D_QKV = 3 * D_ATTN
D_IN = D_QKV + D_SSM + N_BRANCH * D_MODEL
N_GROUPS = 4
EXPERTS_PER_GROUP = 4
N_EXPERTS = N_GROUPS * EXPERTS_PER_GROUP
D_EXPERT = D_MODEL // 4
EPS = 1e-6
NEG_INF = -1e30
LOG2_E = 1.4426950408889634

LANES = 128
HEAD_PAIRS = D_ATTN // LANES
PROJ_ROWS = 512
EVEN_SUM_LANE = HEAD_DIM
ODD_SUM_LANE = 0
SSM_CHUNK = 16
SSM_ROW = SSM_CHUNK * SSM_GROUP
SSM_KBLOCK = 16
SSM_BGROUP = LANES // SSM_KBLOCK
ROUTER_LANES = LANES
D_ROUTED = D_MODEL + ROUTER_LANES
CLASS_LANE = 32
RANK_LANE = 33
MOE_TILE = 512
DMA_THREADS = 2
_PAIRS = [(a, b) for a in range(EXPERTS_PER_GROUP) for b in range(a + 1, EXPERTS_PER_GROUP)]
PAIRS_PER_GROUP = len(_PAIRS)
N_CLASSES = N_GROUPS * PAIRS_PER_GROUP
_CLASS_EXPERTS = [(g * EXPERTS_PER_GROUP + a, g * EXPERTS_PER_GROUP + b)
                  for g in range(N_GROUPS) for a, b in _PAIRS]
VMEM_LIMIT = 56 * 1024 * 1024


def _tiles(n_tokens):
    return dict(merge=512, moe_rows=512)


def _compiler_params(n_axes):
    return pltpu.CompilerParams(
        dimension_semantics=("arbitrary",) * n_axes,
        vmem_limit_bytes=VMEM_LIMIT)


def _full(shape):
    nd = len(shape)
    return pl.BlockSpec(shape, lambda *_: (0,) * nd)


def _rms_norm(x, gain):
    ms = jnp.mean(x * x, axis=-1, keepdims=True)
    return x * lax.rsqrt(ms + EPS) * gain


NT_DIMS = (((1,), (1,)), ((), ()))


def _attn_kernel(x_ref, gain_ref, w_ref, qg_ref, kg_ref, ones_ref, bias_ref, o_ref,
                 q_scr, kpad, v_even, v_odd, s_a, s_b):
    seq = x_ref.shape[0]
    n_chunks = seq // CHUNK
    zeros = jnp.zeros((PAD_KEYS, D_ATTN), BF16)
    kpad[0:PAD_KEYS, :] = zeros
    v_even[0:PAD_KEYS, :] = zeros
    v_odd[0:PAD_KEYS, :] = zeros

    lane_v = lax.broadcasted_iota(jnp.int32, (PROJ_ROWS, D_ATTN), 1)
    even_v = (lane_v % LANES) < HEAD_DIM
    ones_even = (lane_v % LANES) == EVEN_SUM_LANE
    ones_odd = (lane_v % LANES) == ODD_SUM_LANE
    for c in range(seq // PROJ_ROWS):
        rows = slice(c * PROJ_ROWS, (c + 1) * PROJ_ROWS)
        kv_rows = slice(PAD_KEYS + c * PROJ_ROWS, PAD_KEYS + (c + 1) * PROJ_ROWS)
        h = _rms_norm(x_ref[rows, :], gain_ref[...]).astype(BF16)

        def proj(lo):
            return jnp.dot(h, w_ref[:, lo:lo + D_ATTN], preferred_element_type=F32)

        def head_norm(t, g_ref):
            ss = jnp.dot((t * t).astype(BF16), ones_ref[...], preferred_element_type=F32)
            return t * lax.rsqrt(ss * (1.0 / HEAD_DIM) + EPS) * g_ref[...]

        q_scr[rows, :] = head_norm(proj(0), qg_ref).astype(BF16)
        kpad[kv_rows, :] = head_norm(proj(D_ATTN), kg_ref).astype(BF16)
        v = proj(2 * D_ATTN).astype(BF16)
        v_even[kv_rows, :] = jnp.where(even_v, v, jnp.where(ones_even, 1.0, 0.0).astype(BF16))
        v_odd[kv_rows, :] = jnp.where(even_v, jnp.where(ones_odd, 1.0, 0.0).astype(BF16), v)

    lane = lax.broadcasted_iota(jnp.int32, (CHUNK, LANES), 1)
    first_head = lane < HEAD_DIM

    def scores(n, s_buf, skip=0):
        r0 = n * CHUNK if isinstance(n, int) else pl.multiple_of(n * CHUNK, CHUNK)
        for j in range(HEAD_PAIRS):
            cols = slice(j * LANES, (j + 1) * LANES)
            qp = q_scr[pl.ds(r0, CHUNK), cols]
            zq = jnp.zeros_like(qp)
            q2 = jnp.concatenate([jnp.where(first_head, qp, zq), jnp.where(first_head, zq, qp)], axis=0)
            kb = kpad[pl.ds(r0 + skip, BAND_KEYS - skip), cols]
            s_buf[j * 2 * CHUNK:(j + 1) * 2 * CHUNK, skip:] = lax.dot_general(
                q2, kb, NT_DIMS, preferred_element_type=F32)

    def softmax_pv(n, s_buf, skip=0, first_valid=0):
        r0 = n * CHUNK if isinstance(n, int) else pl.multiple_of(n * CHUNK, CHUNK)
        width = BAND_KEYS - skip
        for j in range(HEAD_PAIRS):
            cols = slice(j * LANES, (j + 1) * LANES)
            rows = slice(j * 2 * CHUNK, (j + 1) * 2 * CHUNK)
            s = s_buf[rows, skip:] + bias_ref[rows, skip:]
            if first_valid > skip:
                key_col = lax.broadcasted_iota(jnp.int32, s.shape, 1) + skip
                s = jnp.where(key_col >= first_valid, s, NEG_INF)
            m = jnp.max(s, axis=-1, keepdims=True)
            pb = jnp.exp2(s - m).astype(BF16)
            o0 = jnp.dot(pb[:CHUNK], v_even[pl.ds(r0 + skip, width), cols], preferred_element_type=F32)
            o1 = jnp.dot(pb[CHUNK:], v_odd[pl.ds(r0 + skip, width), cols], preferred_element_type=F32)
            l0 = o0[:, EVEN_SUM_LANE:EVEN_SUM_LANE + 1]
            l1 = o1[:, ODD_SUM_LANE:ODD_SUM_LANE + 1]
            o_ref[pl.ds(r0, CHUNK), cols] = jnp.where(first_head, o0 / l0, o1 / l1).astype(BF16)

    assert n_chunks % 4 == 0 and LEFT_CHUNKS % 4 == 0 and n_chunks > LEFT_CHUNKS

    def band_start(n):
        first_valid = max(LEFT_CHUNKS - n, 0) * CHUNK
        return first_valid // LANES * LANES, first_valid

    bufs = (s_a, s_b)
    scores(0, s_a, band_start(0)[0])
    for n in range(LEFT_CHUNKS):
        scores(n + 1, bufs[(n + 1) % 2], band_start(n + 1)[0])
        softmax_pv(n, bufs[n % 2], *band_start(n))

    def pair_body(i, carry):
        n0 = 4 * i
        scores(n0 + 1, s_b)
        softmax_pv(n0, s_a)
        scores(n0 + 2, s_a)
        softmax_pv(n0 + 1, s_b)
        scores(n0 + 3, s_b)
        softmax_pv(n0 + 2, s_a)
        scores(jnp.minimum(n0 + 4, n_chunks - 1), s_a)
        softmax_pv(n0 + 3, s_b)
        return carry

    lax.fori_loop(LEFT_CHUNKS // 4, n_chunks // 4, pair_body, 0)


def _attention(x2, gain, w_qkv, q_gain, k_gain, bias, bsz, seq):
    assert seq % PROJ_ROWS == 0
    ones = jnp.kron(jnp.eye(ATTN_HEADS, dtype=F32), jnp.ones((HEAD_DIM, HEAD_DIM), F32)).astype(BF16)
    qg = jnp.tile(q_gain, ATTN_HEADS)[None, :] * (HEAD_DIM ** -0.5 * LOG2_E)
    kg = jnp.tile(k_gain, ATTN_HEADS)[None, :]
    seq_spec = lambda width: pl.BlockSpec((None, seq, width), lambda b: (b, 0, 0))
    padded = pltpu.VMEM((PAD_KEYS + seq, D_ATTN), BF16)
    out = pl.pallas_call(
        _attn_kernel,
        grid=(bsz,),
        in_specs=[seq_spec(D_MODEL), _full((1, D_MODEL)), _full((D_MODEL, D_QKV)),
                  _full((1, D_ATTN)), _full((1, D_ATTN)), _full((D_ATTN, D_ATTN)),
                  _full((ATTN_HEADS * CHUNK, BAND_KEYS))],
        out_specs=seq_spec(D_ATTN),
        out_shape=jax.ShapeDtypeStruct((bsz, seq, D_ATTN), BF16),
        scratch_shapes=[pltpu.VMEM((seq, D_ATTN), BF16), padded, padded, padded,
                        pltpu.VMEM((ATTN_HEADS * CHUNK, BAND_KEYS), F32),
                        pltpu.VMEM((ATTN_HEADS * CHUNK, BAND_KEYS), F32)],
        compiler_params=_compiler_params(1),
        name="attention",
    )(x2.reshape(bsz, seq, D_MODEL), gain[None, :], w_qkv.astype(BF16), qg, kg, ones, bias)
    return out.reshape(bsz * seq, D_ATTN)


def _rel_bias_table(rel_bias):
    n_rel = REL_FUTURE + MAX_REL + 1
    far = jnp.broadcast_to(rel_bias[:, n_rel - 1:], (ATTN_HEADS, PAD_KEYS - MAX_REL + CHUNK))
    ext = jnp.concatenate([far, rel_bias[:, n_rel - 2::-1]], axis=1) * LOG2_E
    rows = [ext[:, CHUNK - 1 - q:CHUNK - 1 - q + BAND_KEYS] for q in range(CHUNK)]
    return jnp.stack(rows, axis=1).reshape(ATTN_HEADS * CHUNK, BAND_KEYS)


SSM_BLOCK_LEN = SSM_KBLOCK * SSM_CHUNK
SSM_BLOCK_TOK = SSM_BGROUP * SSM_KBLOCK


def _token_block_spec(width):
    return pl.BlockSpec((SSM_BGROUP, SSM_BLOCK_LEN, width), lambda kb, bg: (bg, kb, 0))


def _transposed_spec(n_bgroups):
    return pl.BlockSpec((SSM_GROUPS, SSM_ROW, SSM_BLOCK_TOK),
                        lambda kb, bg: (0, 0, kb * n_bgroups + bg))


def _token_grid(bsz, seq):
    assert bsz % SSM_BGROUP == 0 and seq % SSM_BLOCK_LEN == 0
    return (seq // SSM_BLOCK_LEN, bsz // SSM_BGROUP)


def _ssm_inproj_kernel(x_ref, gain_ref, wu_ref, ut_ref, u_scr):
    n_rows = SSM_BGROUP * SSM_BLOCK_LEN
    x = x_ref[...].reshape(n_rows, D_MODEL)
    h = _rms_norm(x, gain_ref[...]).astype(BF16)
    u = jnp.dot(h, wu_ref[...], preferred_element_type=F32)
    for q in range(D_SSM // LANES):
        u_scr[q] = u[:, q * LANES:(q + 1) * LANES]
    groups_per_tile = LANES // SSM_GROUP
    for t in range(SSM_CHUNK):
        for q in range(D_SSM // LANES):
            ut = u_scr[q, pl.ds(t, SSM_BLOCK_TOK, stride=SSM_CHUNK), :].T.astype(BF16)
            for gl in range(groups_per_tile):
                ut_ref[q * groups_per_tile + gl, t * SSM_GROUP:(t + 1) * SSM_GROUP, :] = (
                    ut[gl * SSM_GROUP:(gl + 1) * SSM_GROUP, :])


def _ssm_inproj(x2, gain, w_u, bsz, seq):
    grid = _token_grid(bsz, seq)
    return pl.pallas_call(
        _ssm_inproj_kernel,
        grid=grid,
        in_specs=[_token_block_spec(D_MODEL), _full((1, D_MODEL)), _full((D_MODEL, D_SSM))],
        out_specs=_transposed_spec(grid[1]),
        out_shape=jax.ShapeDtypeStruct((SSM_GROUPS, SSM_ROW, bsz * seq // SSM_CHUNK), BF16),
        scratch_shapes=[pltpu.VMEM((D_SSM // LANES, SSM_BGROUP * SSM_BLOCK_LEN, LANES), F32)],
        compiler_params=_compiler_params(2),
        name="ssm_inproj",
    )(x2.reshape(bsz, seq, D_MODEL), gain[None, :], w_u.astype(BF16))


def _ssm_out_kernel(yt_ref, z_ref, y_scr):
    groups_per_tile = LANES // SSM_GROUP
    for t in range(SSM_CHUNK):
        for q in range(D_SSM // LANES):
            yt = jnp.concatenate(
                [yt_ref[q * groups_per_tile + gl, t * SSM_GROUP:(t + 1) * SSM_GROUP, :]
                 for gl in range(groups_per_tile)], axis=0)
            y_scr[q, pl.ds(t, SSM_BLOCK_TOK, stride=SSM_CHUNK), :] = yt.astype(F32).T
    for q in range(D_SSM // LANES):
        z = jax.nn.gelu(y_scr[q])
        z_ref[:, :, q * LANES:(q + 1) * LANES] = z.astype(BF16).reshape(
            SSM_BGROUP, SSM_BLOCK_LEN, LANES)


def _ssm_out(y_t, bsz, seq):
    grid = _token_grid(bsz, seq)
    z = pl.pallas_call(
        _ssm_out_kernel,
        grid=grid,
        in_specs=[_transposed_spec(grid[1])],
        out_specs=_token_block_spec(D_SSM),
        out_shape=jax.ShapeDtypeStruct((bsz, seq, D_SSM), BF16),
        scratch_shapes=[pltpu.VMEM((D_SSM // LANES, SSM_BGROUP * SSM_BLOCK_LEN, LANES), F32)],
        compiler_params=_compiler_params(2),
        name="ssm_out",
    )(y_t)
    return z.reshape(bsz * seq, D_SSM)


def _ssm_tables(lambda_re, lambda_im, log_step, b_re, b_im, c_re, c_im, d_skip):
    step = jnp.exp(log_step)[:, None]
    mag = jnp.exp(lambda_re * step)
    ang = lambda_im * step
    a_re = mag * jnp.cos(ang)
    a_im = mag * jnp.sin(ang)
    num_re = a_re - 1.0
    num_im = a_im
    den = lambda_re * lambda_re + lambda_im * lambda_im
    f_re = (num_re * lambda_re + num_im * lambda_im) / den
    f_im = (num_im * lambda_re - num_re * lambda_im) / den
    bb_re = f_re[..., None] * b_re - f_im[..., None] * b_im
    bb_im = f_re[..., None] * b_im + f_im[..., None] * b_re
    log_mag = lambda_re * step

    def powers(tau):
        tau = jnp.asarray(tau, F32)[:, None, None]
        pmag = jnp.exp(tau * log_mag[None])
        return pmag * jnp.cos(tau * ang[None]), pmag * jnp.sin(tau * ang[None])

    steps = jnp.arange(SSM_CHUNK)
    up_re, up_im = powers(steps + 1)
    cp_re = c_re[None] * up_re[:, :, None, :] - c_im[None] * up_im[:, :, None, :]
    cp_im = c_re[None] * up_im[:, :, None, :] + c_im[None] * up_re[:, :, None, :]
    readout = jnp.concatenate([cp_re, -cp_im], axis=-1)
    readout = jnp.transpose(readout, (1, 0, 2, 3)).reshape(SSM_GROUPS, SSM_ROW, 2 * SSM_STATE)

    def inject(tau):
        p_re, p_im = powers(tau)
        bt_re = jnp.transpose(bb_re, (0, 2, 1))[None]
        bt_im = jnp.transpose(bb_im, (0, 2, 1))[None]
        i_re = p_re[:, :, None, :] * bt_re - p_im[:, :, None, :] * bt_im
        i_im = p_re[:, :, None, :] * bt_im + p_im[:, :, None, :] * bt_re
        both = jnp.concatenate([i_re, i_im], axis=-1)
        return jnp.transpose(both, (1, 3, 0, 2)).reshape(SSM_GROUPS, 2 * SSM_STATE, SSM_ROW)

    inject_back = inject(-(steps + 1.0))
    inject_end = inject(SSM_CHUNK - 1.0 - steps).astype(BF16)
    d_row = jnp.tile(d_skip.reshape(SSM_GROUPS, 1, SSM_GROUP), (1, 1, SSM_CHUNK))
    (ar,), (ai,) = powers([float(SSM_CHUNK)])
    a_chunk = jnp.stack([jnp.concatenate([ar, ar], -1), jnp.concatenate([-ai, ai], -1),
                         jnp.concatenate([ai, -ai], -1)], axis=1)
    return readout, inject_back, inject_end, d_row, a_chunk


def _ssm_kernel(ut_ref, ro_ref, ib_ref, ie_ref, d_ref, a_ref, yt_ref, p_scr, q_scr, s_scr, *, bsz):
    ut = ut_ref[...]
    n_tok = ut.shape[1]
    r_hi, r_lo = _split_bf16(ro_ref[...])
    i_hi, i_lo = _split_bf16(ib_ref[...])
    kern = (jnp.dot(r_hi, i_hi, preferred_element_type=F32)
            + jnp.dot(r_hi, i_lo, preferred_element_type=F32)
            + jnp.dot(r_lo, i_hi, preferred_element_type=F32))
    row = lax.broadcasted_iota(jnp.int32, kern.shape, 0)
    col = lax.broadcasted_iota(jnp.int32, kern.shape, 1)
    group_shift = SSM_GROUP.bit_length() - 1
    causal = jnp.right_shift(row, group_shift) >= jnp.right_shift(col, group_shift)
    kern = jnp.where(causal, kern, 0.0) + jnp.where(row == col, d_ref[...], 0.0)
    wt = jnp.dot(ie_ref[...], ut, preferred_element_type=F32)
    w = wt.T
    p_scr[...] = w
    q_scr[...] = pltpu.roll(w, SSM_STATE, 1)
    a1 = a_ref[0:1, :]
    a2 = a_ref[1:2, :]
    a3 = a_ref[2:3, :]

    def block(kb, carry):
        cp, cq = carry
        base = pl.multiple_of(kb * (bsz * SSM_KBLOCK), bsz * SSM_KBLOCK)
        for k in range(SSM_KBLOCK):
            rows = pl.ds(base + k, bsz, stride=SSM_KBLOCK)
            s_scr[rows, :] = cp
            cp, cq = (a1 * cp + a2 * cq + p_scr[rows, :],
                      a1 * cq + a3 * cp + q_scr[rows, :])
        return cp, cq

    zero = jnp.zeros((bsz, 2 * SSM_STATE), F32)
    lax.fori_loop(0, n_tok // (bsz * SSM_KBLOCK), block, (zero, zero))
    yt = jnp.dot(kern.astype(BF16), ut, preferred_element_type=F32)
    yt = yt + lax.dot_general(r_hi, s_scr[...].astype(BF16), NT_DIMS, preferred_element_type=F32)
    yt_ref[...] = yt.astype(BF16)


def _ssm(ut, tables, bsz):
    readout, inject_back, inject_end, d_row, a_chunk = tables
    n_tok = ut.shape[2]
    per_group = lambda a, b: pl.BlockSpec((None, a, b), lambda g: (g, 0, 0))
    state = pltpu.VMEM((n_tok, 2 * SSM_STATE), F32)
    return pl.pallas_call(
        functools.partial(_ssm_kernel, bsz=bsz),
        grid=(SSM_GROUPS,),
        in_specs=[per_group(SSM_ROW, n_tok), per_group(SSM_ROW, 2 * SSM_STATE),
                  per_group(2 * SSM_STATE, SSM_ROW), per_group(2 * SSM_STATE, SSM_ROW),
                  per_group(1, SSM_ROW), per_group(3, 2 * SSM_STATE)],
        out_specs=per_group(SSM_ROW, n_tok),
        out_shape=jax.ShapeDtypeStruct((SSM_GROUPS, SSM_ROW, n_tok), BF16),
        scratch_shapes=[state, state, state],
        compiler_params=_compiler_params(1),
        name="ssm",
    )(ut, readout, inject_back, inject_end, d_row, a_chunk)


def _split_bf16(a):
    hi = a.astype(BF16)
    lo = (a - hi.astype(F32)).astype(BF16)
    return hi, lo


def _route(logits):
    lane = lax.broadcasted_iota(jnp.int32, logits.shape, 1)
    lane_f = lane.astype(F32)
    big = float(ROUTER_LANES)

    def first_argmax(vals, vmax):
        return jnp.min(jnp.where(vals == vmax, lane_f, big), axis=-1, keepdims=True)

    g_mask = (lane >= N_EXPERTS) & (lane < N_EXPERTS + N_GROUPS)
    g_log = jnp.where(g_mask, logits, NEG_INF)
    g_max = jnp.max(g_log, axis=-1, keepdims=True)
    g_sum = jnp.sum(jnp.where(g_mask, jnp.exp(g_log - g_max), 0.0), axis=-1, keepdims=True)
    g_prob = 1.0 / g_sum
    g_idx = first_argmax(g_log, g_max) - float(N_EXPERTS)
    e_lo = g_idx * float(EXPERTS_PER_GROUP)
    e_mask = (lane_f >= e_lo) & (lane_f < e_lo + float(EXPERTS_PER_GROUP))
    e_log = jnp.where(e_mask, logits, NEG_INF)
    e_max1 = jnp.max(e_log, axis=-1, keepdims=True)
    idx1 = first_argmax(e_log, e_max1)
    e_log2 = jnp.where(lane_f == idx1, NEG_INF, e_log)
    e_max2 = jnp.max(e_log2, axis=-1, keepdims=True)
    idx2 = first_argmax(e_log2, e_max2)
    r = jnp.exp(e_max2 - e_max1)
    w1 = g_prob / (1.0 + r)
    w2 = g_prob * r / (1.0 + r)
    gates = jnp.where(lane_f == idx1, w1, 0.0) + jnp.where(lane_f == idx2, w2, 0.0)
    lo = jnp.minimum(idx1, idx2) - e_lo
    hi = jnp.maximum(idx1, idx2) - e_lo
    pair = lo * (2.0 * EXPERTS_PER_GROUP - 1.0 - lo) * 0.5 + (hi - lo - 1.0)
    return gates, g_idx * float(PAIRS_PER_GROUP) + pair


def _merge_kernel(x_ref, ya_ref, z_ref, gain1_ref, wgate_ref, bgate_ref, wglu_ref, bglu_ref,
                  wba_ref, wbs_ref, wout_ref, gain2_ref, rcat_ref, rbias_ref, ltri_ref,
                  xg_ref, counts_ref, count_scr):
    x = x_ref[...]
    hn = _rms_norm(x, gain1_ref[...]).astype(BF16)
    z = z_ref[...]
    glu = jnp.dot(z, wglu_ref[...], preferred_element_type=F32) + bglu_ref[...]
    y_ssm = (z.astype(F32) * jax.nn.sigmoid(glu)).astype(BF16)
    merged = None
    for s, (y_b, wb_ref) in enumerate(((ya_ref[...], wba_ref), (y_ssm, wbs_ref))):
        cols = slice(s * D_MODEL, (s + 1) * D_MODEL)
        gate = jax.nn.sigmoid(jnp.dot(hn, wgate_ref[:, cols], preferred_element_type=F32)
                              + bgate_ref[:, cols])
        term = gate * jnp.dot(y_b, wb_ref[...], preferred_element_type=F32)
        merged = term if merged is None else merged + term
    x1 = x + jnp.dot(merged.astype(BF16), wout_ref[...], preferred_element_type=F32)
    xg_ref[:, :D_MODEL] = x1

    h = _rms_norm(x1, gain2_ref[...])
    h_hi, h_lo = _split_bf16(h)
    hi_terms = jnp.dot(h_hi, rcat_ref[...], preferred_element_type=F32)
    logits = (hi_terms[:, :ROUTER_LANES] + hi_terms[:, ROUTER_LANES:]
              + jnp.dot(h_lo, rcat_ref[:, :ROUTER_LANES], preferred_element_type=F32)
              + rbias_ref[...])
    gates, cls = _route(logits)

    @pl.when(pl.program_id(0) == 0)
    def _():
        count_scr[...] = jnp.zeros_like(count_scr)

    lane_f = lax.broadcasted_iota(jnp.int32, gates.shape, 1).astype(F32)
    onehot = jnp.where(lane_f == cls, 1.0, 0.0)
    before = jnp.dot(ltri_ref[...], onehot.astype(BF16), preferred_element_type=F32)
    seen = count_scr[...]
    rank = jnp.sum(onehot * (before + seen), axis=-1, keepdims=True)
    count_scr[...] = seen + jnp.sum(onehot, axis=0, keepdims=True)
    counts_ref[...] = count_scr[...]
    xg_ref[:, D_MODEL:] = (gates + jnp.where(lane_f == float(CLASS_LANE), cls, 0.0)
                           + jnp.where(lane_f == float(RANK_LANE), rank, 0.0))


def _merge(x2, y_attn, z, gain1, w_gate, b_gate, w_glu, b_glu, w_branch, w_out, ffn_gain,
           w_group_router, group_bias, w_expert_router, expert_bias, tm):
    t = x2.shape[0]
    pad = ROUTER_LANES - N_EXPERTS - N_GROUPS
    w_r = jnp.concatenate([w_expert_router, w_group_router,
                           jnp.zeros((D_MODEL, pad), F32)], axis=1)
    r_cat = jnp.concatenate(_split_bf16(w_r), axis=1)
    r_bias = jnp.concatenate([expert_bias, group_bias, jnp.zeros((pad,), F32)])[None, :]
    ltri = jnp.tril(jnp.ones((tm, tm), F32), -1).astype(BF16)
    row = lambda n: pl.BlockSpec((tm, n), lambda i: (i, 0))
    resident = lambda shape: pl.BlockSpec(shape, lambda i: (0, 0), pipeline_mode=pl.Buffered(1))
    return pl.pallas_call(
        _merge_kernel,
        grid=(t // tm,),
        in_specs=[row(D_MODEL), row(D_ATTN), row(D_SSM),
                  resident((1, D_MODEL)), resident((D_MODEL, N_BRANCH * D_MODEL)),
                  resident((1, N_BRANCH * D_MODEL)),
                  resident((D_SSM, D_SSM)), resident((1, D_SSM)),
                  resident((D_ATTN, D_MODEL)), resident((D_SSM, D_MODEL)),
                  resident((D_MODEL, D_MODEL)), resident((1, D_MODEL)),
                  resident((D_MODEL, 2 * ROUTER_LANES)),
                  resident((1, ROUTER_LANES)), resident((tm, tm))],
        out_specs=[row(D_ROUTED), pl.BlockSpec((1, ROUTER_LANES), lambda i: (0, 0))],
        out_shape=[jax.ShapeDtypeStruct((t, D_ROUTED), F32),
                   jax.ShapeDtypeStruct((1, ROUTER_LANES), F32)],
        scratch_shapes=[pltpu.VMEM((1, ROUTER_LANES), F32)],
        compiler_params=_compiler_params(1),
        name="merge",
    )(x2, y_attn, z, gain1[None, :], w_gate.astype(BF16), b_gate[None, :],
      w_glu.astype(BF16), b_glu[None, :],
      w_branch[:D_ATTN].astype(BF16), w_branch[D_ATTN:].astype(BF16),
      w_out.astype(BF16), ffn_gain[None, :], r_cat, r_bias, ltri)


def _moe_plan(routed, counts, n_tiles):
    cnt = counts[0, :N_CLASSES].astype(jnp.int32)
    padded = (cnt + (MOE_TILE - 1)) // MOE_TILE * MOE_TILE
    ends = jnp.cumsum(padded)
    starts = ends - padded
    classes = jnp.arange(N_CLASSES, dtype=jnp.int32)
    cls = routed[:, D_MODEL + CLASS_LANE].astype(jnp.int32)
    rank = routed[:, D_MODEL + RANK_LANE].astype(jnp.int32)
    dest = rank + jnp.sum(jnp.where(cls[:, None] == classes[None, :], starts[None, :], 0), axis=1)
    n_used = ends[-1] // MOE_TILE
    tile_end = ends // MOE_TILE
    tiles = jnp.arange(n_tiles, dtype=jnp.int32)
    tile_cls = jnp.minimum(jnp.sum(tiles[:, None] >= tile_end[None, :], axis=1), N_CLASSES - 1)
    pick = lambda table: jnp.sum(
        jnp.where(tile_cls[:, None] == classes[None, :], jnp.asarray(table, jnp.int32)[None, :], 0),
        axis=1).astype(jnp.int32)
    tile_ea = pick([e for e, _ in _CLASS_EXPERTS])
    tile_eb = pick([e for _, e in _CLASS_EXPERTS])
    clear_a = jnp.where(padded > 0, tile_end - 1, -1)
    tail = n_used + classes
    clear_b = jnp.where(tail < n_tiles, tail, -1)
    clear = jnp.concatenate([clear_a, clear_b]).astype(jnp.int32)
    return dest.astype(jnp.int32), tile_ea, tile_eb, jnp.reshape(n_used, (1,)).astype(jnp.int32), clear


def _row_copy(src, src_row, dst, dst_row, sem):
    return pltpu.make_async_copy(src.at[pl.ds(src_row, 1), :], dst.at[pl.ds(dst_row, 1), :], sem)


def _dispatch_kernel(clear_ref, dest_ref, src_ref, dst_hbm, zero_scr, sem, *, rows):
    step = pl.program_id(0)

    def clear_copy(j):
        start = pl.multiple_of(clear_ref[j] * MOE_TILE, MOE_TILE)
        return pltpu.make_async_copy(zero_scr, dst_hbm.at[pl.ds(start, MOE_TILE), :], sem)

    @pl.when(step == 0)
    def _():
        zero_scr[...] = jnp.zeros_like(zero_scr)
        for wait in (False, True):
            for j in range(2 * N_CLASSES):
                @pl.when(clear_ref[j] >= 0)
                def _():
                    clear_copy(j).wait() if wait else clear_copy(j).start()

    for r in range(rows):
        _row_copy(src_ref, r, dst_hbm, dest_ref[r], sem).start(priority=r % DMA_THREADS)
    for r in range(rows):
        _row_copy(src_ref, r, dst_hbm, 0, sem).wait()


def _dispatch(routed, dest, clear, n_tiles, rows):
    t, width = routed.shape
    return pl.pallas_call(
        functools.partial(_dispatch_kernel, rows=rows),
        grid_spec=pltpu.PrefetchScalarGridSpec(
            num_scalar_prefetch=1,
            grid=(t // rows,),
            in_specs=[pl.BlockSpec((rows,), lambda i, clear: (i,), memory_space=pltpu.SMEM),
                      pl.BlockSpec((rows, width), lambda i, clear: (i, 0))],
            out_specs=pl.BlockSpec(memory_space=pl.ANY),
            scratch_shapes=[pltpu.VMEM((MOE_TILE, width), F32), pltpu.SemaphoreType.DMA(())]),
        out_shape=jax.ShapeDtypeStruct((n_tiles * MOE_TILE, width), F32),
        compiler_params=_compiler_params(1),
        name="moe_dispatch",
    )(clear, dest, routed)


def _expert_kernel(ea_ref, eb_ref, used_ref, xs_ref, gain_ref, wga_ref, wua_ref, wda_ref,
                   wgb_ref, wub_ref, wdb_ref, o_ref):
    i = pl.program_id(0)

    @pl.when(i < used_ref[0])
    def _():
        x1 = xs_ref[:, :D_MODEL]
        gates = xs_ref[:, D_MODEL:]
        h = _rms_norm(x1, gain_ref[...]).astype(BF16)
        lane = lax.broadcasted_iota(jnp.int32, gates.shape, 1)
        acc = x1
        for e_ref, wg_ref, wu_ref, wd_ref in ((ea_ref, wga_ref, wua_ref, wda_ref),
                                              (eb_ref, wgb_ref, wub_ref, wdb_ref)):
            gate = jnp.sum(jnp.where(lane == e_ref[i], gates, 0.0), axis=-1, keepdims=True)
            a = jnp.dot(h, wg_ref[...], preferred_element_type=F32)
            u = jnp.dot(h, wu_ref[...], preferred_element_type=F32)
            hid = (jax.nn.silu(a) * u * gate).astype(BF16)
            acc = acc + jnp.dot(hid, wd_ref[...], preferred_element_type=F32)
        o_ref[...] = acc

    @pl.when(i >= used_ref[0])
    def _():
        o_ref[...] = jnp.zeros_like(o_ref)


def _experts(xs, tile_ea, tile_eb, n_used, gain, w_e_gate, w_e_up, w_e_down):
    n_tiles = xs.shape[0] // MOE_TILE
    by_a = lambda r, c: pl.BlockSpec((None, r, c), lambda i, ea, eb, used: (ea[i], 0, 0))
    by_b = lambda r, c: pl.BlockSpec((None, r, c), lambda i, ea, eb, used: (eb[i], 0, 0))
    wg, wu, wd = (w.astype(BF16) for w in (w_e_gate, w_e_up, w_e_down))
    return pl.pallas_call(
        _expert_kernel,
        grid_spec=pltpu.PrefetchScalarGridSpec(
            num_scalar_prefetch=3,
            grid=(n_tiles,),
            in_specs=[pl.BlockSpec((MOE_TILE, D_ROUTED),
                                   lambda i, ea, eb, used: (jnp.minimum(i, used[0] - 1), 0)),
                      pl.BlockSpec((1, D_MODEL), lambda i, ea, eb, used: (0, 0)),
                      by_a(D_MODEL, D_EXPERT), by_a(D_MODEL, D_EXPERT), by_a(D_EXPERT, D_MODEL),
                      by_b(D_MODEL, D_EXPERT), by_b(D_MODEL, D_EXPERT), by_b(D_EXPERT, D_MODEL)],
            out_specs=pl.BlockSpec((MOE_TILE, D_MODEL), lambda i, ea, eb, used: (i, 0))),
        out_shape=jax.ShapeDtypeStruct((n_tiles * MOE_TILE, D_MODEL), F32),
        compiler_params=_compiler_params(1),
        name="moe_experts",
    )(tile_ea, tile_eb, n_used, xs, gain[None, :], wg, wu, wd, wg, wu, wd)


def _unsort_kernel(dest_ref, src_hbm, dst_ref, sem, *, rows):
    for r in range(rows):
        _row_copy(src_hbm, dest_ref[r], dst_ref, r, sem).start(priority=r % DMA_THREADS)
    for r in range(rows):
        _row_copy(src_hbm, 0, dst_ref, r, sem).wait()


def _unsort(xo, dest, rows):
    t = dest.shape[0]
    return pl.pallas_call(
        functools.partial(_unsort_kernel, rows=rows),
        grid=(t // rows,),
        in_specs=[pl.BlockSpec((rows,), lambda i: (i,), memory_space=pltpu.SMEM),
                  pl.BlockSpec(memory_space=pl.ANY)],
        out_specs=pl.BlockSpec((rows, xo.shape[1]), lambda i: (i, 0)),
        out_shape=jax.ShapeDtypeStruct((t, xo.shape[1]), F32),
        scratch_shapes=[pltpu.SemaphoreType.DMA(())],
        compiler_params=_compiler_params(1),
        name="moe_unsort",
    )(dest, xo)


def _moe(routed, counts, ffn_gain, w_e_gate, w_e_up, w_e_down, rows):
    t = routed.shape[0]
    assert t % MOE_TILE == 0 and t % rows == 0
    n_tiles = t // MOE_TILE + N_CLASSES
    dest, tile_ea, tile_eb, n_used, clear = _moe_plan(routed, counts, n_tiles)
    xs = _dispatch(routed, dest, clear, n_tiles, rows)
    xo = _experts(xs, tile_ea, tile_eb, n_used, ffn_gain, w_e_gate, w_e_up, w_e_down)
    return _unsort(xo, dest, rows)


def kernel(x, mix_norm_gain, w_in, b_gate, q_gain, k_gain, rel_bias, ssm_lambda_re, ssm_lambda_im, ssm_log_step, ssm_b_re, ssm_b_im, ssm_c_re, ssm_c_im, ssm_d, w_glu, b_glu, w_branch, w_out, ffn_norm_gain, w_group_router, group_bias, w_expert_router, expert_bias, w_e_gate, w_e_up, w_e_down):
    bsz, seq, _ = x.shape
    assert seq % (SSM_KBLOCK * SSM_CHUNK) == 0 and seq % (4 * CHUNK) == 0
    depth = w_in.shape[0]
    tiles = _tiles(bsz * seq)
    x2 = x.reshape(bsz * seq, D_MODEL)
    for i in range(depth):
        w_qkv = w_in[i][:, :D_QKV]
        w_u = w_in[i][:, D_QKV:D_QKV + D_SSM]
        w_gate = w_in[i][:, D_QKV + D_SSM:]
        y_attn = _attention(x2, mix_norm_gain[i], w_qkv, q_gain[i], k_gain[i],
                            _rel_bias_table(rel_bias[i]), bsz, seq)
        u_t = _ssm_inproj(x2, mix_norm_gain[i], w_u, bsz, seq)
        tables = _ssm_tables(ssm_lambda_re[i], ssm_lambda_im[i], ssm_log_step[i],
                             ssm_b_re[i], ssm_b_im[i], ssm_c_re[i], ssm_c_im[i], ssm_d[i])
        z = _ssm_out(_ssm(u_t, tables, bsz), bsz, seq)
        routed, counts = _merge(x2, y_attn, z, mix_norm_gain[i], w_gate, b_gate[i], w_glu[i],
                                b_glu[i], w_branch[i], w_out[i], ffn_norm_gain[i],
                                w_group_router[i], group_bias[i], w_expert_router[i],
                                expert_bias[i], tiles["merge"])
        x2 = _moe(routed, counts, ffn_norm_gain[i], w_e_gate[i], w_e_up[i], w_e_down[i],
                  tiles["moe_rows"])
    return x2.reshape(bsz, seq, D_MODEL)
```

```python
import functools

import jax
import jax.numpy as jnp
from jax import lax
from jax.experimental import pallas as pl
from jax.experimental.pallas import tpu as pltpu

F32 = jnp.float32
BF16 = jnp.bfloat16

D_MODEL = 1024
CHUNK = 64
LEFT_CHUNKS = 8
BAND = LEFT_CHUNKS + 1
BAND_KEYS = BAND * CHUNK
PAD_KEYS = LEFT_CHUNKS * CHUNK
ATTN_HEADS = 8
HEAD_DIM = 64
D_ATTN = ATTN_HEADS * HEAD_DIM
MAX_REL = 256
REL_FUTURE = CHUNK - 1
D_SSM = D_MODEL // 2
SSM_GROUP = 16
SSM_GROUPS = D_SSM // SSM_GROUP
SSM_STATE = 64
N_BRANCH = 2
D_QKV = 3 * D_ATTN
D_IN = D_QKV + D_SSM + N_BRANCH * D_MODEL
N_GROUPS = 4
EXPERTS_PER_GROUP = 4
N_EXPERTS = N_GROUPS * EXPERTS_PER_GROUP
D_EXPERT = D_MODEL // 4
EPS = 1e-6
NEG_INF = -1e30
LOG2_E = 1.4426950408889634

LANES = 128
HEAD_PAIRS = D_ATTN // LANES
PROJ_ROWS = 512
EVEN_SUM_LANE = HEAD_DIM
ODD_SUM_LANE = 0
SSM_CHUNK = 16
SSM_ROW = SSM_CHUNK * SSM_GROUP
SSM_KBLOCK = 16
SSM_BGROUP = LANES // SSM_KBLOCK
ROUTER_LANES = LANES
D_ROUTED = D_MODEL + ROUTER_LANES
CLASS_LANE = 32
RANK_LANE = 33
MOE_TILE = 512
DMA_THREADS = 2
_PAIRS = [(a, b) for a in range(EXPERTS_PER_GROUP) for b in range(a + 1, EXPERTS_PER_GROUP)]
PAIRS_PER_GROUP = len(_PAIRS)
N_CLASSES = N_GROUPS * PAIRS_PER_GROUP
_CLASS_EXPERTS = [(g * EXPERTS_PER_GROUP + a, g * EXPERTS_PER_GROUP + b)
                  for g in range(N_GROUPS) for a, b in _PAIRS]
VMEM_LIMIT = 56 * 1024 * 1024


def _tiles(n_tokens):
    return dict(merge=512, moe_rows=512)


def _compiler_params(n_axes):
    return pltpu.CompilerParams(
        dimension_semantics=("arbitrary",) * n_axes,
        vmem_limit_bytes=VMEM_LIMIT)


def _full(shape):
    nd = len(shape)
    return pl.BlockSpec(shape, lambda *_: (0,) * nd)


def _rms_norm(x, gain):
    ms = jnp.mean(x * x, axis=-1, keepdims=True)
    return x * lax.rsqrt(ms + EPS) * gain


NT_DIMS = (((1,), (1,)), ((), ()))


def _attn_kernel(x_ref, gain_ref, w_ref, qg_ref, kg_ref, ones_ref, bias_ref, o_ref,
                 q_scr, kpad, v_even, v_odd, s_a, s_b):
    seq = x_ref.shape[0]
    n_chunks = seq // CHUNK
    zeros = jnp.zeros((PAD_KEYS, D_ATTN), BF16)
    kpad[0:PAD_KEYS, :] = zeros
    v_even[0:PAD_KEYS, :] = zeros
    v_odd[0:PAD_KEYS, :] = zeros

    lane_v = lax.broadcasted_iota(jnp.int32, (PROJ_ROWS, D_ATTN), 1)
    even_v = (lane_v % LANES) < HEAD_DIM
    ones_even = (lane_v % LANES) == EVEN_SUM_LANE
    ones_odd = (lane_v % LANES) == ODD_SUM_LANE
    for c in range(seq // PROJ_ROWS):
        rows = slice(c * PROJ_ROWS, (c + 1) * PROJ_ROWS)
        kv_rows = slice(PAD_KEYS + c * PROJ_ROWS, PAD_KEYS + (c + 1) * PROJ_ROWS)
        h = _rms_norm(x_ref[rows, :], gain_ref[...]).astype(BF16)

        def proj(lo):
            return jnp.dot(h, w_ref[:, lo:lo + D_ATTN], preferred_element_type=F32)

        def head_norm(t, g_ref):
            ss = jnp.dot((t * t).astype(BF16), ones_ref[...], preferred_element_type=F32)
            return t * lax.rsqrt(ss * (1.0 / HEAD_DIM) + EPS) * g_ref[...]

        q_scr[rows, :] = head_norm(proj(0), qg_ref).astype(BF16)
        kpad[kv_rows, :] = head_norm(proj(D_ATTN), kg_ref).astype(BF16)
        v = proj(2 * D_ATTN).astype(BF16)
        v_even[kv_rows, :] = jnp.where(even_v, v, jnp.where(ones_even, 1.0, 0.0).astype(BF16))
        v_odd[kv_rows, :] = jnp.where(even_v, jnp.where(ones_odd, 1.0, 0.0).astype(BF16), v)

    lane = lax.broadcasted_iota(jnp.int32, (CHUNK, LANES), 1)
    first_head = lane < HEAD_DIM

    def scores(n, s_buf, skip=0):
        r0 = n * CHUNK if isinstance(n, int) else pl.multiple_of(n * CHUNK, CHUNK)
        for j in range(HEAD_PAIRS):
            cols = slice(j * LANES, (j + 1) * LANES)
            qp = q_scr[pl.ds(r0, CHUNK), cols]
            zq = jnp.zeros_like(qp)
            q2 = jnp.concatenate([jnp.where(first_head, qp, zq), jnp.where(first_head, zq, qp)], axis=0)
            kb = kpad[pl.ds(r0 + skip, BAND_KEYS - skip), cols]
            s_buf[j * 2 * CHUNK:(j + 1) * 2 * CHUNK, skip:] = lax.dot_general(
                q2, kb, NT_DIMS, preferred_element_type=F32)

    def softmax_pv(n, s_buf, skip=0, first_valid=0):
        r0 = n * CHUNK if isinstance(n, int) else pl.multiple_of(n * CHUNK, CHUNK)
        width = BAND_KEYS - skip
        for j in range(HEAD_PAIRS):
            cols = slice(j * LANES, (j + 1) * LANES)
            rows = slice(j * 2 * CHUNK, (j + 1) * 2 * CHUNK)
            s = s_buf[rows, skip:] + bias_ref[rows, skip:]
            if first_valid > skip:
                key_col = lax.broadcasted_iota(jnp.int32, s.shape, 1) + skip
                s = jnp.where(key_col >= first_valid, s, NEG_INF)
            m = jnp.max(s, axis=-1, keepdims=True)
            pb = jnp.exp2(s - m).astype(BF16)
            o0 = jnp.dot(pb[:CHUNK], v_even[pl.ds(r0 + skip, width), cols], preferred_element_type=F32)
            o1 = jnp.dot(pb[CHUNK:], v_odd[pl.ds(r0 + skip, width), cols], preferred_element_type=F32)
            l0 = o0[:, EVEN_SUM_LANE:EVEN_SUM_LANE + 1]
            l1 = o1[:, ODD_SUM_LANE:ODD_SUM_LANE + 1]
            o_ref[pl.ds(r0, CHUNK), cols] = jnp.where(first_head, o0 / l0, o1 / l1).astype(BF16)

    assert n_chunks % 4 == 0 and LEFT_CHUNKS % 4 == 0 and n_chunks > LEFT_CHUNKS

    def band_start(n):
        first_valid = max(LEFT_CHUNKS - n, 0) * CHUNK
        return first_valid // LANES * LANES, first_valid

    bufs = (s_a, s_b)
    scores(0, s_a, band_start(0)[0])
    for n in range(LEFT_CHUNKS):
        scores(n + 1, bufs[(n + 1) % 2], band_start(n + 1)[0])
        softmax_pv(n, bufs[n % 2], *band_start(n))

    def pair_body(i, carry):
        n0 = 4 * i
        scores(n0 + 1, s_b)
        softmax_pv(n0, s_a)
        scores(n0 + 2, s_a)
        softmax_pv(n0 + 1, s_b)
        scores(n0 + 3, s_b)
        softmax_pv(n0 + 2, s_a)
        scores(jnp.minimum(n0 + 4, n_chunks - 1), s_a)
        softmax_pv(n0 + 3, s_b)
        return carry

    lax.fori_loop(LEFT_CHUNKS // 4, n_chunks // 4, pair_body, 0)


def _attention(x2, gain, w_qkv, q_gain, k_gain, bias, bsz, seq):
    assert seq % PROJ_ROWS == 0
    ones = jnp.kron(jnp.eye(ATTN_HEADS, dtype=F32), jnp.ones((HEAD_DIM, HEAD_DIM), F32)).astype(BF16)
    qg = jnp.tile(q_gain, ATTN_HEADS)[None, :] * (HEAD_DIM ** -0.5 * LOG2_E)
    kg = jnp.tile(k_gain, ATTN_HEADS)[None, :]
    seq_spec = lambda width: pl.BlockSpec((None, seq, width), lambda b: (b, 0, 0))
    padded = pltpu.VMEM((PAD_KEYS + seq, D_ATTN), BF16)
    out = pl.pallas_call(
        _attn_kernel,
        grid=(bsz,),
        in_specs=[seq_spec(D_MODEL), _full((1, D_MODEL)), _full((D_MODEL, D_QKV)),
                  _full((1, D_ATTN)), _full((1, D_ATTN)), _full((D_ATTN, D_ATTN)),
                  _full((ATTN_HEADS * CHUNK, BAND_KEYS))],
        out_specs=seq_spec(D_ATTN),
        out_shape=jax.ShapeDtypeStruct((bsz, seq, D_ATTN), BF16),
        scratch_shapes=[pltpu.VMEM((seq, D_ATTN), BF16), padded, padded, padded,
                        pltpu.VMEM((ATTN_HEADS * CHUNK, BAND_KEYS), F32),
                        pltpu.VMEM((ATTN_HEADS * CHUNK, BAND_KEYS), F32)],
        compiler_params=_compiler_params(1),
        name="attention",
    )(x2.reshape(bsz, seq, D_MODEL), gain[None, :], w_qkv.astype(BF16), qg, kg, ones, bias)
    return out.reshape(bsz * seq, D_ATTN)


def _rel_bias_table(rel_bias):
    n_rel = REL_FUTURE + MAX_REL + 1
    far = jnp.broadcast_to(rel_bias[:, n_rel - 1:], (ATTN_HEADS, PAD_KEYS - MAX_REL + CHUNK))
    ext = jnp.concatenate([far, rel_bias[:, n_rel - 2::-1]], axis=1) * LOG2_E
    rows = [ext[:, CHUNK - 1 - q:CHUNK - 1 - q + BAND_KEYS] for q in range(CHUNK)]
    return jnp.stack(rows, axis=1).reshape(ATTN_HEADS * CHUNK, BAND_KEYS)


SSM_BLOCK_LEN = SSM_KBLOCK * SSM_CHUNK
SSM_BLOCK_TOK = SSM_BGROUP * SSM_KBLOCK


def _token_block_spec(width):
    return pl.BlockSpec((SSM_BGROUP, SSM_BLOCK_LEN, width), lambda kb, bg: (bg, kb, 0))


def _transposed_spec(n_bgroups):
    return pl.BlockSpec((SSM_GROUPS, SSM_ROW, SSM_BLOCK_TOK),
                        lambda kb, bg: (0, 0, kb * n_bgroups + bg))


def _token_grid(bsz, seq):
    assert bsz % SSM_BGROUP == 0 and seq % SSM_BLOCK_LEN == 0
    return (seq // SSM_BLOCK_LEN, bsz // SSM_BGROUP)


def _ssm_inproj_kernel(x_ref, gain_ref, wu_ref, ut_ref, u_scr):
    n_rows = SSM_BGROUP * SSM_BLOCK_LEN
    x = x_ref[...].reshape(n_rows, D_MODEL)
    h = _rms_norm(x, gain_ref[...]).astype(BF16)
    u = jnp.dot(h, wu_ref[...], preferred_element_type=F32)
    for q in range(D_SSM // LANES):
        u_scr[q] = u[:, q * LANES:(q + 1) * LANES]
    groups_per_tile = LANES // SSM_GROUP
    for t in range(SSM_CHUNK):
        for q in range(D_SSM // LANES):
            ut = u_scr[q, pl.ds(t, SSM_BLOCK_TOK, stride=SSM_CHUNK), :].T.astype(BF16)
            for gl in range(groups_per_tile):
                ut_ref[q * groups_per_tile + gl, t * SSM_GROUP:(t + 1) * SSM_GROUP, :] = (
                    ut[gl * SSM_GROUP:(gl + 1) * SSM_GROUP, :])


def _ssm_inproj(x2, gain, w_u, bsz, seq):
    grid = _token_grid(bsz, seq)
    return pl.pallas_call(
        _ssm_inproj_kernel,
        grid=grid,
        in_specs=[_token_block_spec(D_MODEL), _full((1, D_MODEL)), _full((D_MODEL, D_SSM))],
        out_specs=_transposed_spec(grid[1]),
        out_shape=jax.ShapeDtypeStruct((SSM_GROUPS, SSM_ROW, bsz * seq // SSM_CHUNK), BF16),
        scratch_shapes=[pltpu.VMEM((D_SSM // LANES, SSM_BGROUP * SSM_BLOCK_LEN, LANES), F32)],
        compiler_params=_compiler_params(2),
        name="ssm_inproj",
    )(x2.reshape(bsz, seq, D_MODEL), gain[None, :], w_u.astype(BF16))


def _ssm_out_kernel(yt_ref, y_ref, y_scr):
    groups_per_tile = LANES // SSM_GROUP
    for t in range(SSM_CHUNK):
        for q in range(D_SSM // LANES):
            yt = jnp.concatenate(
                [yt_ref[q * groups_per_tile + gl, t * SSM_GROUP:(t + 1) * SSM_GROUP, :]
                 for gl in range(groups_per_tile)], axis=0)
            y_scr[q, pl.ds(t, SSM_BLOCK_TOK, stride=SSM_CHUNK), :] = yt.astype(F32).T
    for q in range(D_SSM // LANES):
        y_ref[:, :, q * LANES:(q + 1) * LANES] = y_scr[q].astype(BF16).reshape(
            SSM_BGROUP, SSM_BLOCK_LEN, LANES)


def _ssm_out(y_t, bsz, seq):
    grid = _token_grid(bsz, seq)
    y = pl.pallas_call(
        _ssm_out_kernel,
        grid=grid,
        in_specs=[_transposed_spec(grid[1])],
        out_specs=_token_block_spec(D_SSM),
        out_shape=jax.ShapeDtypeStruct((bsz, seq, D_SSM), BF16),
        scratch_shapes=[pltpu.VMEM((D_SSM // LANES, SSM_BGROUP * SSM_BLOCK_LEN, LANES), F32)],
        compiler_params=_compiler_params(2),
        name="ssm_out",
    )(y_t)
    return y.reshape(bsz * seq, D_SSM)


def _ssm_tables(lambda_re, lambda_im, log_step, b_re, b_im, c_re, c_im, d_skip):
    step = jnp.exp(log_step)[:, None]
    mag = jnp.exp(lambda_re * step)
    ang = lambda_im * step
    a_re = mag * jnp.cos(ang)
    a_im = mag * jnp.sin(ang)
    num_re = a_re - 1.0
    num_im = a_im
    den = lambda_re * lambda_re + lambda_im * lambda_im
    f_re = (num_re * lambda_re + num_im * lambda_im) / den
    f_im = (num_im * lambda_re - num_re * lambda_im) / den
    bb_re = f_re[..., None] * b_re - f_im[..., None] * b_im
    bb_im = f_re[..., None] * b_im + f_im[..., None] * b_re
    log_mag = lambda_re * step

    def powers(tau):
        tau = jnp.asarray(tau, F32)[:, None, None]
        pmag = jnp.exp(tau * log_mag[None])
        return pmag * jnp.cos(tau * ang[None]), pmag * jnp.sin(tau * ang[None])

    steps = jnp.arange(SSM_CHUNK)
    up_re, up_im = powers(steps + 1)
    cp_re = c_re[None] * up_re[:, :, None, :] - c_im[None] * up_im[:, :, None, :]
    cp_im = c_re[None] * up_im[:, :, None, :] + c_im[None] * up_re[:, :, None, :]
    readout = jnp.concatenate([cp_re, -cp_im], axis=-1)
    readout = jnp.transpose(readout, (1, 0, 2, 3)).reshape(SSM_GROUPS, SSM_ROW, 2 * SSM_STATE)

    def inject(tau):
        p_re, p_im = powers(tau)
        bt_re = jnp.transpose(bb_re, (0, 2, 1))[None]
        bt_im = jnp.transpose(bb_im, (0, 2, 1))[None]
        i_re = p_re[:, :, None, :] * bt_re - p_im[:, :, None, :] * bt_im
        i_im = p_re[:, :, None, :] * bt_im + p_im[:, :, None, :] * bt_re
        both = jnp.concatenate([i_re, i_im], axis=-1)
        return jnp.transpose(both, (1, 3, 0, 2)).reshape(SSM_GROUPS, 2 * SSM_STATE, SSM_ROW)

    inject_back = inject(-(steps + 1.0))
    inject_end = inject(SSM_CHUNK - 1.0 - steps).astype(BF16)
    d_row = jnp.tile(d_skip.reshape(SSM_GROUPS, 1, SSM_GROUP), (1, 1, SSM_CHUNK))
    (ar,), (ai,) = powers([float(SSM_CHUNK)])
    a_chunk = jnp.stack([jnp.concatenate([ar, ar], -1), jnp.concatenate([-ai, ai], -1),
                         jnp.concatenate([ai, -ai], -1)], axis=1)
    return readout, inject_back, inject_end, d_row, a_chunk


def _ssm_kernel(ut_ref, ro_ref, ib_ref, ie_ref, d_ref, a_ref, yt_ref, p_scr, q_scr, s_scr, *, bsz):
    ut = ut_ref[...]
    n_tok = ut.shape[1]
    r_hi, r_lo = _split_bf16(ro_ref[...])
    i_hi, i_lo = _split_bf16(ib_ref[...])
    kern = (jnp.dot(r_hi, i_hi, preferred_element_type=F32)
            + jnp.dot(r_hi, i_lo, preferred_element_type=F32)
            + jnp.dot(r_lo, i_hi, preferred_element_type=F32))
    row = lax.broadcasted_iota(jnp.int32, kern.shape, 0)
    col = lax.broadcasted_iota(jnp.int32, kern.shape, 1)
    group_shift = SSM_GROUP.bit_length() - 1
    causal = jnp.right_shift(row, group_shift) >= jnp.right_shift(col, group_shift)
    kern = jnp.where(causal, kern, 0.0) + jnp.where(row == col, d_ref[...], 0.0)
    wt = jnp.dot(ie_ref[...], ut, preferred_element_type=F32)
    w = wt.T
    p_scr[...] = w
    q_scr[...] = pltpu.roll(w, SSM_STATE, 1)
    a1 = a_ref[0:1, :]
    a2 = a_ref[1:2, :]
    a3 = a_ref[2:3, :]

    def block(kb, carry):
        cp, cq = carry
        base = pl.multiple_of(kb * (bsz * SSM_KBLOCK), bsz * SSM_KBLOCK)
        for k in range(SSM_KBLOCK):
            rows = pl.ds(base + k, bsz, stride=SSM_KBLOCK)
            s_scr[rows, :] = cp
            cp, cq = (a1 * cp + a2 * cq + p_scr[rows, :],
                      a1 * cq + a3 * cp + q_scr[rows, :])
        return cp, cq

    zero = jnp.zeros((bsz, 2 * SSM_STATE), F32)
    lax.fori_loop(0, n_tok // (bsz * SSM_KBLOCK), block, (zero, zero))
    yt = jnp.dot(kern.astype(BF16), ut, preferred_element_type=F32)
    yt = yt + lax.dot_general(r_hi, s_scr[...].astype(BF16), NT_DIMS, preferred_element_type=F32)
    yt_ref[...] = yt.astype(BF16)


def _ssm(ut, tables, bsz):
    readout, inject_back, inject_end, d_row, a_chunk = tables
    n_tok = ut.shape[2]
    per_group = lambda a, b: pl.BlockSpec((None, a, b), lambda g: (g, 0, 0))
    state = pltpu.VMEM((n_tok, 2 * SSM_STATE), F32)
    return pl.pallas_call(
        functools.partial(_ssm_kernel, bsz=bsz),
        grid=(SSM_GROUPS,),
        in_specs=[per_group(SSM_ROW, n_tok), per_group(SSM_ROW, 2 * SSM_STATE),
                  per_group(2 * SSM_STATE, SSM_ROW), per_group(2 * SSM_STATE, SSM_ROW),
                  per_group(1, SSM_ROW), per_group(3, 2 * SSM_STATE)],
        out_specs=per_group(SSM_ROW, n_tok),
        out_shape=jax.ShapeDtypeStruct((SSM_GROUPS, SSM_ROW, n_tok), BF16),
        scratch_shapes=[state, state, state],
        compiler_params=_compiler_params(1),
        name="ssm",
    )(ut, readout, inject_back, inject_end, d_row, a_chunk)


def _split_bf16(a):
    hi = a.astype(BF16)
    lo = (a - hi.astype(F32)).astype(BF16)
    return hi, lo


def _route(logits):
    lane = lax.broadcasted_iota(jnp.int32, logits.shape, 1)
    lane_f = lane.astype(F32)
    big = float(ROUTER_LANES)

    def first_argmax(vals, vmax):
        return jnp.min(jnp.where(vals == vmax, lane_f, big), axis=-1, keepdims=True)

    g_mask = (lane >= N_EXPERTS) & (lane < N_EXPERTS + N_GROUPS)
    g_log = jnp.where(g_mask, logits, NEG_INF)
    g_max = jnp.max(g_log, axis=-1, keepdims=True)
    g_sum = jnp.sum(jnp.where(g_mask, jnp.exp(g_log - g_max), 0.0), axis=-1, keepdims=True)
    g_prob = 1.0 / g_sum
    g_idx = first_argmax(g_log, g_max) - float(N_EXPERTS)
    e_lo = g_idx * float(EXPERTS_PER_GROUP)
    e_mask = (lane_f >= e_lo) & (lane_f < e_lo + float(EXPERTS_PER_GROUP))
    e_log = jnp.where(e_mask, logits, NEG_INF)
    e_max1 = jnp.max(e_log, axis=-1, keepdims=True)
    idx1 = first_argmax(e_log, e_max1)
    e_log2 = jnp.where(lane_f == idx1, NEG_INF, e_log)
    e_max2 = jnp.max(e_log2, axis=-1, keepdims=True)
    idx2 = first_argmax(e_log2, e_max2)
    r = jnp.exp(e_max2 - e_max1)
    w1 = g_prob / (1.0 + r)
    w2 = g_prob * r / (1.0 + r)
    gates = jnp.where(lane_f == idx1, w1, 0.0) + jnp.where(lane_f == idx2, w2, 0.0)
    lo = jnp.minimum(idx1, idx2) - e_lo
    hi = jnp.maximum(idx1, idx2) - e_lo
    pair = lo * (2.0 * EXPERTS_PER_GROUP - 1.0 - lo) * 0.5 + (hi - lo - 1.0)
    return gates, g_idx * float(PAIRS_PER_GROUP) + pair


def _merge_kernel(x_ref, ya_ref, ys_ref, gain1_ref, wgate_ref, bgate_ref, wglu_ref, bglu_ref,
                  wba_ref, wbs_ref, wout_ref, gain2_ref, rcat_ref, rbias_ref, ltri_ref,
                  xg_ref, counts_ref, count_scr):
    x = x_ref[...]
    hn = _rms_norm(x, gain1_ref[...]).astype(BF16)
    z = jax.nn.gelu(ys_ref[...].astype(F32))
    glu = jnp.dot(z.astype(BF16), wglu_ref[...], preferred_element_type=F32) + bglu_ref[...]
    y_ssm = (z * jax.nn.sigmoid(glu)).astype(BF16)
    merged = None
    for s, (y_b, wb_ref) in enumerate(((ya_ref[...], wba_ref), (y_ssm, wbs_ref))):
        cols = slice(s * D_MODEL, (s + 1) * D_MODEL)
        gate = jax.nn.sigmoid(jnp.dot(hn, wgate_ref[:, cols], preferred_element_type=F32)
                              + bgate_ref[:, cols])
        term = gate * jnp.dot(y_b, wb_ref[...], preferred_element_type=F32)
        merged = term if merged is None else merged + term
    x1 = x + jnp.dot(merged.astype(BF16), wout_ref[...], preferred_element_type=F32)
    xg_ref[:, :D_MODEL] = x1

    h = _rms_norm(x1, gain2_ref[...])
    h_hi, h_lo = _split_bf16(h)
    hi_terms = jnp.dot(h_hi, rcat_ref[...], preferred_element_type=F32)
    logits = (hi_terms[:, :ROUTER_LANES] + hi_terms[:, ROUTER_LANES:]
              + jnp.dot(h_lo, rcat_ref[:, :ROUTER_LANES], preferred_element_type=F32)
              + rbias_ref[...])
    gates, cls = _route(logits)

    @pl.when(pl.program_id(0) == 0)
    def _():
        count_scr[...] = jnp.zeros_like(count_scr)

    lane_f = lax.broadcasted_iota(jnp.int32, gates.shape, 1).astype(F32)
    onehot = jnp.where(lane_f == cls, 1.0, 0.0)
    before = jnp.dot(ltri_ref[...], onehot.astype(BF16), preferred_element_type=F32)
    seen = count_scr[...]
    rank = jnp.sum(onehot * (before + seen), axis=-1, keepdims=True)
    count_scr[...] = seen + jnp.sum(onehot, axis=0, keepdims=True)
    counts_ref[...] = count_scr[...]
    xg_ref[:, D_MODEL:] = (gates + jnp.where(lane_f == float(CLASS_LANE), cls, 0.0)
                           + jnp.where(lane_f == float(RANK_LANE), rank, 0.0))


def _merge(x2, y_attn, y_pre, gain1, w_gate, b_gate, w_glu, b_glu, w_branch, w_out, ffn_gain,
           w_group_router, group_bias, w_expert_router, expert_bias, tm):
    t = x2.shape[0]
    pad = ROUTER_LANES - N_EXPERTS - N_GROUPS
    w_r = jnp.concatenate([w_expert_router, w_group_router,
                           jnp.zeros((D_MODEL, pad), F32)], axis=1)
    r_cat = jnp.concatenate(_split_bf16(w_r), axis=1)
    r_bias = jnp.concatenate([expert_bias, group_bias, jnp.zeros((pad,), F32)])[None, :]
    ltri = jnp.tril(jnp.ones((tm, tm), F32), -1).astype(BF16)
    row = lambda n: pl.BlockSpec((tm, n), lambda i: (i, 0))
    resident = lambda shape: pl.BlockSpec(shape, lambda i: (0, 0), pipeline_mode=pl.Buffered(1))
    return pl.pallas_call(
        _merge_kernel,
        grid=(t // tm,),
        in_specs=[row(D_MODEL), row(D_ATTN), row(D_SSM),
                  resident((1, D_MODEL)), resident((D_MODEL, N_BRANCH * D_MODEL)),
                  resident((1, N_BRANCH * D_MODEL)),
                  resident((D_SSM, D_SSM)), resident((1, D_SSM)),
                  resident((D_ATTN, D_MODEL)), resident((D_SSM, D_MODEL)),
                  resident((D_MODEL, D_MODEL)), resident((1, D_MODEL)),
                  resident((D_MODEL, 2 * ROUTER_LANES)),
                  resident((1, ROUTER_LANES)), resident((tm, tm))],
        out_specs=[row(D_ROUTED), pl.BlockSpec((1, ROUTER_LANES), lambda i: (0, 0))],
        out_shape=[jax.ShapeDtypeStruct((t, D_ROUTED), F32),
                   jax.ShapeDtypeStruct((1, ROUTER_LANES), F32)],
        scratch_shapes=[pltpu.VMEM((1, ROUTER_LANES), F32)],
        compiler_params=_compiler_params(1),
        name="merge",
    )(x2, y_attn, y_pre, gain1[None, :], w_gate.astype(BF16), b_gate[None, :],
      w_glu.astype(BF16), b_glu[None, :],
      w_branch[:D_ATTN].astype(BF16), w_branch[D_ATTN:].astype(BF16),
      w_out.astype(BF16), ffn_gain[None, :], r_cat, r_bias, ltri)


def _moe_plan(routed, counts, n_items):
    i32 = jnp.int32
    cnt = counts[0, :N_CLASSES].astype(i32)
    ends = jnp.cumsum(cnt)
    starts = ends - cnt
    classes = jnp.arange(N_CLASSES, dtype=i32)
    cls = routed[:, D_MODEL + CLASS_LANE].astype(i32)
    rank = routed[:, D_MODEL + RANK_LANE].astype(i32)
    dest = rank + jnp.sum(jnp.where(cls[:, None] == classes[None, :], starts[None, :], 0), axis=1)

    first_tile = starts // MOE_TILE
    n_class_items = jnp.where(cnt > 0, (ends - 1) // MOE_TILE - first_tile + 1, 0)
    item_end = jnp.cumsum(n_class_items)
    item_start = item_end - n_class_items
    n_used = item_end[-1]
    item = jnp.arange(n_items, dtype=i32)
    item_cls = jnp.minimum(jnp.sum(item[:, None] >= item_end[None, :], axis=1), N_CLASSES - 1)
    pick = lambda table: jnp.sum(
        jnp.where(item_cls[:, None] == classes[None, :], jnp.asarray(table, i32)[None, :], 0),
        axis=1).astype(i32)
    valid = item < n_used
    last_tile = routed.shape[0] // MOE_TILE - 1
    tile = jnp.where(valid, pick(first_tile) + item - pick(item_start), last_tile)
    row0 = tile * MOE_TILE
    lo = jnp.where(valid, jnp.maximum(pick(starts), row0) - row0, 0)
    hi = jnp.where(valid, jnp.minimum(pick(ends), row0 + MOE_TILE) - row0, 0)
    prev_tile = jnp.concatenate([jnp.full((1,), -1, i32), tile[:-1]])
    first = (valid & (tile != prev_tile)).astype(i32)
    plan = (tile.astype(i32), pick([e for e, _ in _CLASS_EXPERTS]), pick([e for _, e in _CLASS_EXPERTS]),
            lo.astype(i32), hi.astype(i32), first, jnp.reshape(n_used, (1,)).astype(i32))
    return dest.astype(i32), plan


def _row_copy(src, src_row, dst, dst_row, sem):
    return pltpu.make_async_copy(src.at[pl.ds(src_row, 1), :], dst.at[pl.ds(dst_row, 1), :], sem)


def _dispatch_kernel(dest_ref, src_ref, dst_hbm, sem, *, rows):
    for r in range(rows):
        _row_copy(src_ref, r, dst_hbm, dest_ref[r], sem).start(priority=r % DMA_THREADS)
    for r in range(rows):
        _row_copy(src_ref, r, dst_hbm, 0, sem).wait()


def _dispatch(routed, dest, rows):
    t, width = routed.shape
    return pl.pallas_call(
        functools.partial(_dispatch_kernel, rows=rows),
        grid=(t // rows,),
        in_specs=[pl.BlockSpec((rows,), lambda i: (i,), memory_space=pltpu.SMEM),
                  pl.BlockSpec((rows, width), lambda i: (i, 0))],
        out_specs=pl.BlockSpec(memory_space=pl.ANY),
        out_shape=jax.ShapeDtypeStruct((t, width), F32),
        scratch_shapes=[pltpu.SemaphoreType.DMA(())],
        compiler_params=_compiler_params(1),
        name="moe_dispatch",
    )(dest, routed)


def _expert_kernel(tile_ref, ea_ref, eb_ref, lo_ref, hi_ref, first_ref, used_ref,
                   xs_ref, gain_ref, wga_ref, wua_ref, wda_ref, wgb_ref, wub_ref, wdb_ref, o_ref):
    i = pl.program_id(0)

    @pl.when(i < used_ref[0])
    def _():
        x1 = xs_ref[:, :D_MODEL]
        gates = xs_ref[:, D_MODEL:]
        h = _rms_norm(x1, gain_ref[...]).astype(BF16)
        lane = lax.broadcasted_iota(jnp.int32, gates.shape, 1)
        row = lax.broadcasted_iota(jnp.int32, gates.shape, 0)
        mine = (row >= lo_ref[i]) & (row < hi_ref[i])
        moe = None
        for e_ref, wg_ref, wu_ref, wd_ref in ((ea_ref, wga_ref, wua_ref, wda_ref),
                                              (eb_ref, wgb_ref, wub_ref, wdb_ref)):
            gate = jnp.sum(jnp.where(mine & (lane == e_ref[i]), gates, 0.0), axis=-1, keepdims=True)
            a = jnp.dot(h, wg_ref[...], preferred_element_type=F32)
            u = jnp.dot(h, wu_ref[...], preferred_element_type=F32)
            hid = (jax.nn.silu(a) * u * gate).astype(BF16)
            term = jnp.dot(hid, wd_ref[...], preferred_element_type=F32)
            moe = term if moe is None else moe + term

        @pl.when(first_ref[i] == 1)
        def _():
            o_ref[...] = x1 + moe

        @pl.when(first_ref[i] == 0)
        def _():
            o_ref[...] += moe


def _experts(xs, plan, gain, w_e_gate, w_e_up, w_e_down):
    n_items = plan[0].shape[0]
    by = lambda which, r, c: pl.BlockSpec((None, r, c), lambda i, *p: (p[which][i], 0, 0))
    tile_rows = lambda width: pl.BlockSpec((MOE_TILE, width), lambda i, *p: (p[0][i], 0))
    wg, wu, wd = (w.astype(BF16) for w in (w_e_gate, w_e_up, w_e_down))
    return pl.pallas_call(
        _expert_kernel,
        grid_spec=pltpu.PrefetchScalarGridSpec(
            num_scalar_prefetch=len(plan),
            grid=(n_items,),
            in_specs=[tile_rows(D_ROUTED),
                      pl.BlockSpec((1, D_MODEL), lambda i, *p: (0, 0)),
                      by(1, D_MODEL, D_EXPERT), by(1, D_MODEL, D_EXPERT), by(1, D_EXPERT, D_MODEL),
                      by(2, D_MODEL, D_EXPERT), by(2, D_MODEL, D_EXPERT), by(2, D_EXPERT, D_MODEL)],
            out_specs=tile_rows(D_MODEL)),
        out_shape=jax.ShapeDtypeStruct((xs.shape[0], D_MODEL), F32),
        compiler_params=_compiler_params(1),
        name="moe_experts",
    )(*plan, xs, gain[None, :], wg, wu, wd, wg, wu, wd)


def _unsort_kernel(dest_ref, src_hbm, dst_ref, sem, *, rows):
    for r in range(rows):
        _row_copy(src_hbm, dest_ref[r], dst_ref, r, sem).start(priority=r % DMA_THREADS)
    for r in range(rows):
        _row_copy(src_hbm, 0, dst_ref, r, sem).wait()


def _unsort(xo, dest, rows):
    t = dest.shape[0]
    return pl.pallas_call(
        functools.partial(_unsort_kernel, rows=rows),
        grid=(t // rows,),
        in_specs=[pl.BlockSpec((rows,), lambda i: (i,), memory_space=pltpu.SMEM),
                  pl.BlockSpec(memory_space=pl.ANY)],
        out_specs=pl.BlockSpec((rows, xo.shape[1]), lambda i: (i, 0)),
        out_shape=jax.ShapeDtypeStruct((t, xo.shape[1]), F32),
        scratch_shapes=[pltpu.SemaphoreType.DMA(())],
        compiler_params=_compiler_params(1),
        name="moe_unsort",
    )(dest, xo)


def _moe(routed, counts, ffn_gain, w_e_gate, w_e_up, w_e_down, rows):
    t = routed.shape[0]
    assert t % MOE_TILE == 0 and t % rows == 0
    n_items = t // MOE_TILE + N_CLASSES
    dest, plan = _moe_plan(routed, counts, n_items)
    xs = _dispatch(routed, dest, rows)
    xo = _experts(xs, plan, ffn_gain, w_e_gate, w_e_up, w_e_down)
    return _unsort(xo, dest, rows)


def kernel(x, mix_norm_gain, w_in, b_gate, q_gain, k_gain, rel_bias, ssm_lambda_re, ssm_lambda_im, ssm_log_step, ssm_b_re, ssm_b_im, ssm_c_re, ssm_c_im, ssm_d, w_glu, b_glu, w_branch, w_out, ffn_norm_gain, w_group_router, group_bias, w_expert_router, expert_bias, w_e_gate, w_e_up, w_e_down):
    bsz, seq, _ = x.shape
    assert seq % (SSM_KBLOCK * SSM_CHUNK) == 0 and seq % (4 * CHUNK) == 0
    depth = w_in.shape[0]
    tiles = _tiles(bsz * seq)
    x2 = x.reshape(bsz * seq, D_MODEL)
    for i in range(depth):
        w_qkv = w_in[i][:, :D_QKV]
        w_u = w_in[i][:, D_QKV:D_QKV + D_SSM]
        w_gate = w_in[i][:, D_QKV + D_SSM:]
        y_attn = _attention(x2, mix_norm_gain[i], w_qkv, q_gain[i], k_gain[i],
                            _rel_bias_table(rel_bias[i]), bsz, seq)
        u_t = _ssm_inproj(x2, mix_norm_gain[i], w_u, bsz, seq)
        tables = _ssm_tables(ssm_lambda_re[i], ssm_lambda_im[i], ssm_log_step[i],
                             ssm_b_re[i], ssm_b_im[i], ssm_c_re[i], ssm_c_im[i], ssm_d[i])
        y_pre = _ssm_out(_ssm(u_t, tables, bsz), bsz, seq)
        routed, counts = _merge(x2, y_attn, y_pre, mix_norm_gain[i], w_gate, b_gate[i], w_glu[i],
                                b_glu[i], w_branch[i], w_out[i], ffn_norm_gain[i],
                                w_group_router[i], group_bias[i], w_expert_router[i],
                                expert_bias[i], tiles["merge"])
        x2 = _moe(routed, counts, ffn_norm_gain[i], w_e_gate[i], w_e_up[i], w_e_down[i],
                  tiles["moe_rows"])
    return x2.reshape(bsz, seq, D_MODEL)
```

```python
import functools

import jax
import jax.numpy as jnp
from jax import lax
from jax.experimental import pallas as pl
from jax.experimental.pallas import tpu as pltpu

F32 = jnp.float32
BF16 = jnp.bfloat16

D_MODEL = 1024
CHUNK = 64
LEFT_CHUNKS = 8
BAND = LEFT_CHUNKS + 1
BAND_KEYS = BAND * CHUNK
PAD_KEYS = LEFT_CHUNKS * CHUNK
ATTN_HEADS = 8
HEAD_DIM = 64
D_ATTN = ATTN_HEADS * HEAD_DIM
MAX_REL = 256
REL_FUTURE = CHUNK - 1
D_SSM = D_MODEL // 2
SSM_GROUP = 16
SSM_GROUPS = D_SSM // SSM_GROUP
SSM_STATE = 64
N_BRANCH = 2
D_QKV = 3 * D_ATTN
D_IN = D_QKV + D_SSM + N_BRANCH * D_MODEL
N_GROUPS = 4
EXPERTS_PER_GROUP = 4
N_EXPERTS = N_GROUPS * EXPERTS_PER_GROUP
D_EXPERT = D_MODEL // 4
EPS = 1e-6
NEG_INF = -1e30
LOG2_E = 1.4426950408889634

LANES = 128
HEAD_PAIRS = D_ATTN // LANES
PROJ_ROWS = 512
EVEN_SUM_LANE = HEAD_DIM
ODD_SUM_LANE = 0
SSM_CHUNK = 16
SSM_ROW = SSM_CHUNK * SSM_GROUP
SSM_KBLOCK = 16
SSM_BGROUP = LANES // SSM_KBLOCK
ROUTER_LANES = LANES
D_ROUTED = D_MODEL + ROUTER_LANES
CLASS_LANE = 32
RANK_LANE = 33
MOE_TILE = 512
DMA_THREADS = 2
_PAIRS = [(a, b) for a in range(EXPERTS_PER_GROUP) for b in range(a + 1, EXPERTS_PER_GROUP)]
PAIRS_PER_GROUP = len(_PAIRS)
N_CLASSES = N_GROUPS * PAIRS_PER_GROUP
_CLASS_EXPERTS = [(g * EXPERTS_PER_GROUP + a, g * EXPERTS_PER_GROUP + b)
                  for g in range(N_GROUPS) for a, b in _PAIRS]
VMEM_LIMIT = 56 * 1024 * 1024


def _tiles(n_tokens):
    return dict(merge=512, moe_rows=512)


def _compiler_params(n_axes):
    return pltpu.CompilerParams(
        dimension_semantics=("arbitrary",) * n_axes,
        vmem_limit_bytes=VMEM_LIMIT)


def _full(shape):
    nd = len(shape)
    return pl.BlockSpec(shape, lambda *_: (0,) * nd)


def _rms_norm(x, gain):
    ms = jnp.mean(x * x, axis=-1, keepdims=True)
    return x * lax.rsqrt(ms + EPS) * gain


NT_DIMS = (((1,), (1,)), ((), ()))


def _attn_kernel(x_ref, gain_ref, w_ref, qg_ref, kg_ref, ones_ref, bias_ref, o_ref,
                 q_scr, kpad, v_even, v_odd, s_a, s_b):
    seq = x_ref.shape[0]
    n_chunks = seq // CHUNK
    zeros = jnp.zeros((PAD_KEYS, D_ATTN), BF16)
    kpad[0:PAD_KEYS, :] = zeros
    v_even[0:PAD_KEYS, :] = zeros
    v_odd[0:PAD_KEYS, :] = zeros

    lane_v = lax.broadcasted_iota(jnp.int32, (PROJ_ROWS, D_ATTN), 1)
    even_v = (lane_v % LANES) < HEAD_DIM
    ones_even = (lane_v % LANES) == EVEN_SUM_LANE
    ones_odd = (lane_v % LANES) == ODD_SUM_LANE
    for c in range(seq // PROJ_ROWS):
        rows = slice(c * PROJ_ROWS, (c + 1) * PROJ_ROWS)
        kv_rows = slice(PAD_KEYS + c * PROJ_ROWS, PAD_KEYS + (c + 1) * PROJ_ROWS)
        h = _rms_norm(x_ref[rows, :], gain_ref[...]).astype(BF16)

        def proj(lo):
            return jnp.dot(h, w_ref[:, lo:lo + D_ATTN], preferred_element_type=F32)

        def head_norm(t, g_ref):
            ss = jnp.dot((t * t).astype(BF16), ones_ref[...], preferred_element_type=F32)
            return t * lax.rsqrt(ss * (1.0 / HEAD_DIM) + EPS) * g_ref[...]

        q_scr[rows, :] = head_norm(proj(0), qg_ref).astype(BF16)
        kpad[kv_rows, :] = head_norm(proj(D_ATTN), kg_ref).astype(BF16)
        v = proj(2 * D_ATTN).astype(BF16)
        v_even[kv_rows, :] = jnp.where(even_v, v, jnp.where(ones_even, 1.0, 0.0).astype(BF16))
        v_odd[kv_rows, :] = jnp.where(even_v, jnp.where(ones_odd, 1.0, 0.0).astype(BF16), v)

    lane = lax.broadcasted_iota(jnp.int32, (CHUNK, LANES), 1)
    first_head = lane < HEAD_DIM

    def scores(n, s_buf, skip=0):
        r0 = n * CHUNK if isinstance(n, int) else pl.multiple_of(n * CHUNK, CHUNK)
        for j in range(HEAD_PAIRS):
            cols = slice(j * LANES, (j + 1) * LANES)
            qp = q_scr[pl.ds(r0, CHUNK), cols]
            zq = jnp.zeros_like(qp)
            q2 = jnp.concatenate([jnp.where(first_head, qp, zq), jnp.where(first_head, zq, qp)], axis=0)
            kb = kpad[pl.ds(r0 + skip, BAND_KEYS - skip), cols]
            s_buf[j * 2 * CHUNK:(j + 1) * 2 * CHUNK, skip:] = lax.dot_general(
                q2, kb, NT_DIMS, preferred_element_type=F32)

    def softmax_pv(n, s_buf, skip=0, first_valid=0):
        r0 = n * CHUNK if isinstance(n, int) else pl.multiple_of(n * CHUNK, CHUNK)
        width = BAND_KEYS - skip
        for j in range(HEAD_PAIRS):
            cols = slice(j * LANES, (j + 1) * LANES)
            rows = slice(j * 2 * CHUNK, (j + 1) * 2 * CHUNK)
            s = s_buf[rows, skip:] + bias_ref[rows, skip:]
            if first_valid > skip:
                key_col = lax.broadcasted_iota(jnp.int32, s.shape, 1) + skip
                s = jnp.where(key_col >= first_valid, s, NEG_INF)
            m = jnp.max(s, axis=-1, keepdims=True)
            pb = jnp.exp2(s - m).astype(BF16)
            o0 = jnp.dot(pb[:CHUNK], v_even[pl.ds(r0 + skip, width), cols], preferred_element_type=F32)
            o1 = jnp.dot(pb[CHUNK:], v_odd[pl.ds(r0 + skip, width), cols], preferred_element_type=F32)
            l0 = o0[:, EVEN_SUM_LANE:EVEN_SUM_LANE + 1]
            l1 = o1[:, ODD_SUM_LANE:ODD_SUM_LANE + 1]
            o_ref[pl.ds(r0, CHUNK), cols] = jnp.where(first_head, o0 / l0, o1 / l1).astype(BF16)

    assert n_chunks % 4 == 0 and LEFT_CHUNKS % 4 == 0 and n_chunks > LEFT_CHUNKS

    def band_start(n):
        first_valid = max(LEFT_CHUNKS - n, 0) * CHUNK
        return first_valid // LANES * LANES, first_valid

    bufs = (s_a, s_b)
    scores(0, s_a, band_start(0)[0])
    for n in range(LEFT_CHUNKS):
        scores(n + 1, bufs[(n + 1) % 2], band_start(n + 1)[0])
        softmax_pv(n, bufs[n % 2], *band_start(n))

    def pair_body(i, carry):
        n0 = 4 * i
        scores(n0 + 1, s_b)
        softmax_pv(n0, s_a)
        scores(n0 + 2, s_a)
        softmax_pv(n0 + 1, s_b)
        scores(n0 + 3, s_b)
        softmax_pv(n0 + 2, s_a)
        scores(jnp.minimum(n0 + 4, n_chunks - 1), s_a)
        softmax_pv(n0 + 3, s_b)
        return carry

    lax.fori_loop(LEFT_CHUNKS // 4, n_chunks // 4, pair_body, 0)


def _attention(x2, gain, w_qkv, q_gain, k_gain, bias, bsz, seq):
    assert seq % PROJ_ROWS == 0
    ones = jnp.kron(jnp.eye(ATTN_HEADS, dtype=F32), jnp.ones((HEAD_DIM, HEAD_DIM), F32)).astype(BF16)
    qg = jnp.tile(q_gain, ATTN_HEADS)[None, :] * (HEAD_DIM ** -0.5 * LOG2_E)
    kg = jnp.tile(k_gain, ATTN_HEADS)[None, :]
    seq_spec = lambda width: pl.BlockSpec((None, seq, width), lambda b: (b, 0, 0))
    padded = pltpu.VMEM((PAD_KEYS + seq, D_ATTN), BF16)
    out = pl.pallas_call(
        _attn_kernel,
        grid=(bsz,),
        in_specs=[seq_spec(D_MODEL), _full((1, D_MODEL)), _full((D_MODEL, D_QKV)),
                  _full((1, D_ATTN)), _full((1, D_ATTN)), _full((D_ATTN, D_ATTN)),
                  _full((ATTN_HEADS * CHUNK, BAND_KEYS))],
        out_specs=seq_spec(D_ATTN),
        out_shape=jax.ShapeDtypeStruct((bsz, seq, D_ATTN), BF16),
        scratch_shapes=[pltpu.VMEM((seq, D_ATTN), BF16), padded, padded, padded,
                        pltpu.VMEM((ATTN_HEADS * CHUNK, BAND_KEYS), F32),
                        pltpu.VMEM((ATTN_HEADS * CHUNK, BAND_KEYS), F32)],
        compiler_params=_compiler_params(1),
        name="attention",
    )(x2.reshape(bsz, seq, D_MODEL), gain[None, :], w_qkv.astype(BF16), qg, kg, ones, bias)
    return out.reshape(bsz * seq, D_ATTN)


def _rel_bias_table(rel_bias):
    n_rel = REL_FUTURE + MAX_REL + 1
    far = jnp.broadcast_to(rel_bias[:, n_rel - 1:], (ATTN_HEADS, PAD_KEYS - MAX_REL + CHUNK))
    ext = jnp.concatenate([far, rel_bias[:, n_rel - 2::-1]], axis=1) * LOG2_E
    rows = [ext[:, CHUNK - 1 - q:CHUNK - 1 - q + BAND_KEYS] for q in range(CHUNK)]
    return jnp.stack(rows, axis=1).reshape(ATTN_HEADS * CHUNK, BAND_KEYS)


SSM_BLOCK_LEN = SSM_KBLOCK * SSM_CHUNK
SSM_BLOCK_TOK = SSM_BGROUP * SSM_KBLOCK


def _token_block_spec(width):
    return pl.BlockSpec((SSM_BGROUP, SSM_BLOCK_LEN, width), lambda kb, bg: (bg, kb, 0))


def _transposed_spec(n_bgroups):
    return pl.BlockSpec((SSM_GROUPS, SSM_ROW, SSM_BLOCK_TOK),
                        lambda kb, bg: (0, 0, kb * n_bgroups + bg))


def _token_grid(bsz, seq):
    assert bsz % SSM_BGROUP == 0 and seq % SSM_BLOCK_LEN == 0
    return (seq // SSM_BLOCK_LEN, bsz // SSM_BGROUP)


def _ssm_inproj_kernel(x_ref, gain_ref, wu_ref, ut_ref, u_scr):
    n_rows = SSM_BGROUP * SSM_BLOCK_LEN
    x = x_ref[...].reshape(n_rows, D_MODEL)
    h = _rms_norm(x, gain_ref[...]).astype(BF16)
    u = jnp.dot(h, wu_ref[...], preferred_element_type=F32)
    for q in range(D_SSM // LANES):
        u_scr[q] = u[:, q * LANES:(q + 1) * LANES]
    groups_per_tile = LANES // SSM_GROUP
    for t in range(SSM_CHUNK):
        for q in range(D_SSM // LANES):
            ut = u_scr[q, pl.ds(t, SSM_BLOCK_TOK, stride=SSM_CHUNK), :].T.astype(BF16)
            for gl in range(groups_per_tile):
                ut_ref[q * groups_per_tile + gl, t * SSM_GROUP:(t + 1) * SSM_GROUP, :] = (
                    ut[gl * SSM_GROUP:(gl + 1) * SSM_GROUP, :])


def _ssm_inproj(x2, gain, w_u, bsz, seq):
    grid = _token_grid(bsz, seq)
    return pl.pallas_call(
        _ssm_inproj_kernel,
        grid=grid,
        in_specs=[_token_block_spec(D_MODEL), _full((1, D_MODEL)), _full((D_MODEL, D_SSM))],
        out_specs=_transposed_spec(grid[1]),
        out_shape=jax.ShapeDtypeStruct((SSM_GROUPS, SSM_ROW, bsz * seq // SSM_CHUNK), BF16),
        scratch_shapes=[pltpu.VMEM((D_SSM // LANES, SSM_BGROUP * SSM_BLOCK_LEN, LANES), F32)],
        compiler_params=_compiler_params(2),
        name="ssm_inproj",
    )(x2.reshape(bsz, seq, D_MODEL), gain[None, :], w_u.astype(BF16))


def _ssm_out_kernel(yt_ref, y_ref, y_scr):
    groups_per_tile = LANES // SSM_GROUP
    for t in range(SSM_CHUNK):
        for q in range(D_SSM // LANES):
            yt = jnp.concatenate(
                [yt_ref[q * groups_per_tile + gl, t * SSM_GROUP:(t + 1) * SSM_GROUP, :]
                 for gl in range(groups_per_tile)], axis=0)
            y_scr[q, pl.ds(t, SSM_BLOCK_TOK, stride=SSM_CHUNK), :] = yt.astype(F32).T
    for q in range(D_SSM // LANES):
        y_ref[:, :, q * LANES:(q + 1) * LANES] = y_scr[q].astype(BF16).reshape(
            SSM_BGROUP, SSM_BLOCK_LEN, LANES)


def _ssm_out(y_t, bsz, seq):
    grid = _token_grid(bsz, seq)
    y = pl.pallas_call(
        _ssm_out_kernel,
        grid=grid,
        in_specs=[_transposed_spec(grid[1])],
        out_specs=_token_block_spec(D_SSM),
        out_shape=jax.ShapeDtypeStruct((bsz, seq, D_SSM), BF16),
        scratch_shapes=[pltpu.VMEM((D_SSM // LANES, SSM_BGROUP * SSM_BLOCK_LEN, LANES), F32)],
        compiler_params=_compiler_params(2),
        name="ssm_out",
    )(y_t)
    return y.reshape(bsz * seq, D_SSM)


def _ssm_tables(lambda_re, lambda_im, log_step, b_re, b_im, c_re, c_im, d_skip):
    step = jnp.exp(log_step)[:, None]
    mag = jnp.exp(lambda_re * step)
    ang = lambda_im * step
    a_re = mag * jnp.cos(ang)
    a_im = mag * jnp.sin(ang)
    num_re = a_re - 1.0
    num_im = a_im
    den = lambda_re * lambda_re + lambda_im * lambda_im
    f_re = (num_re * lambda_re + num_im * lambda_im) / den
    f_im = (num_im * lambda_re - num_re * lambda_im) / den
    bb_re = f_re[..., None] * b_re - f_im[..., None] * b_im
    bb_im = f_re[..., None] * b_im + f_im[..., None] * b_re
    log_mag = lambda_re * step

    def powers(tau):
        tau = jnp.asarray(tau, F32)[:, None, None]
        pmag = jnp.exp(tau * log_mag[None])
        return pmag * jnp.cos(tau * ang[None]), pmag * jnp.sin(tau * ang[None])

    steps = jnp.arange(SSM_CHUNK)
    up_re, up_im = powers(steps + 1)
    cp_re = c_re[None] * up_re[:, :, None, :] - c_im[None] * up_im[:, :, None, :]
    cp_im = c_re[None] * up_im[:, :, None, :] + c_im[None] * up_re[:, :, None, :]
    readout = jnp.concatenate([cp_re, -cp_im], axis=-1)
    readout = jnp.transpose(readout, (1, 0, 2, 3)).reshape(SSM_GROUPS, SSM_ROW, 2 * SSM_STATE)

    def inject(tau):
        p_re, p_im = powers(tau)
        bt_re = jnp.transpose(bb_re, (0, 2, 1))[None]
        bt_im = jnp.transpose(bb_im, (0, 2, 1))[None]
        i_re = p_re[:, :, None, :] * bt_re - p_im[:, :, None, :] * bt_im
        i_im = p_re[:, :, None, :] * bt_im + p_im[:, :, None, :] * bt_re
        both = jnp.concatenate([i_re, i_im], axis=-1)
        return jnp.transpose(both, (1, 3, 0, 2)).reshape(SSM_GROUPS, 2 * SSM_STATE, SSM_ROW)

    inject_back = inject(-(steps + 1.0))
    inject_end = inject(SSM_CHUNK - 1.0 - steps).astype(BF16)
    d_row = jnp.tile(d_skip.reshape(SSM_GROUPS, 1, SSM_GROUP), (1, 1, SSM_CHUNK))
    (ar,), (ai,) = powers([float(SSM_CHUNK)])
    a_chunk = jnp.stack([jnp.concatenate([ar, ar], -1), jnp.concatenate([-ai, ai], -1),
                         jnp.concatenate([ai, -ai], -1)], axis=1)
    return readout, inject_back, inject_end, d_row, a_chunk


def _ssm_kernel(ut_ref, ro_ref, ib_ref, ie_ref, d_ref, a_ref, yt_ref, p_scr, q_scr, s_scr, *, bsz):
    ut = ut_ref[...]
    n_tok = ut.shape[1]
    r_hi, r_lo = _split_bf16(ro_ref[...])
    i_hi, i_lo = _split_bf16(ib_ref[...])
    kern = (jnp.dot(r_hi, i_hi, preferred_element_type=F32)
            + jnp.dot(r_hi, i_lo, preferred_element_type=F32)
            + jnp.dot(r_lo, i_hi, preferred_element_type=F32))
    row = lax.broadcasted_iota(jnp.int32, kern.shape, 0)
    col = lax.broadcasted_iota(jnp.int32, kern.shape, 1)
    group_shift = SSM_GROUP.bit_length() - 1
    causal = jnp.right_shift(row, group_shift) >= jnp.right_shift(col, group_shift)
    kern = jnp.where(causal, kern, 0.0) + jnp.where(row == col, d_ref[...], 0.0)
    wt = jnp.dot(ie_ref[...], ut, preferred_element_type=F32)
    w = wt.T
    p_scr[...] = w
    q_scr[...] = pltpu.roll(w, SSM_STATE, 1)
    a1 = a_ref[0:1, :]
    a2 = a_ref[1:2, :]
    a3 = a_ref[2:3, :]

    def block(kb, carry):
        cp, cq = carry
        base = pl.multiple_of(kb * (bsz * SSM_KBLOCK), bsz * SSM_KBLOCK)
        for k in range(SSM_KBLOCK):
            rows = pl.ds(base + k, bsz, stride=SSM_KBLOCK)
            s_scr[rows, :] = cp
            cp, cq = (a1 * cp + a2 * cq + p_scr[rows, :],
                      a1 * cq + a3 * cp + q_scr[rows, :])
        return cp, cq

    zero = jnp.zeros((bsz, 2 * SSM_STATE), F32)
    lax.fori_loop(0, n_tok // (bsz * SSM_KBLOCK), block, (zero, zero))
    yt = jnp.dot(kern.astype(BF16), ut, preferred_element_type=F32)
    yt = yt + lax.dot_general(r_hi, s_scr[...].astype(BF16), NT_DIMS, preferred_element_type=F32)
    yt_ref[...] = yt.astype(BF16)


def _ssm(ut, tables, bsz):
    readout, inject_back, inject_end, d_row, a_chunk = tables
    n_tok = ut.shape[2]
    per_group = lambda a, b: pl.BlockSpec((None, a, b), lambda g: (g, 0, 0))
    state = pltpu.VMEM((n_tok, 2 * SSM_STATE), F32)
    return pl.pallas_call(
        functools.partial(_ssm_kernel, bsz=bsz),
        grid=(SSM_GROUPS,),
        in_specs=[per_group(SSM_ROW, n_tok), per_group(SSM_ROW, 2 * SSM_STATE),
                  per_group(2 * SSM_STATE, SSM_ROW), per_group(2 * SSM_STATE, SSM_ROW),
                  per_group(1, SSM_ROW), per_group(3, 2 * SSM_STATE)],
        out_specs=per_group(SSM_ROW, n_tok),
        out_shape=jax.ShapeDtypeStruct((SSM_GROUPS, SSM_ROW, n_tok), BF16),
        scratch_shapes=[state, state, state],
        compiler_params=_compiler_params(1),
        name="ssm",
    )(ut, readout, inject_back, inject_end, d_row, a_chunk)


def _split_bf16(a):
    hi = a.astype(BF16)
    lo = (a - hi.astype(F32)).astype(BF16)
    return hi, lo


def _route(logits):
    lane = lax.broadcasted_iota(jnp.int32, logits.shape, 1)
    lane_f = lane.astype(F32)
    big = float(ROUTER_LANES)

    def first_argmax(vals, vmax):
        return jnp.min(jnp.where(vals == vmax, lane_f, big), axis=-1, keepdims=True)

    g_mask = (lane >= N_EXPERTS) & (lane < N_EXPERTS + N_GROUPS)
    g_log = jnp.where(g_mask, logits, NEG_INF)
    g_max = jnp.max(g_log, axis=-1, keepdims=True)
    g_sum = jnp.sum(jnp.where(g_mask, jnp.exp(g_log - g_max), 0.0), axis=-1, keepdims=True)
    g_prob = 1.0 / g_sum
    g_idx = first_argmax(g_log, g_max) - float(N_EXPERTS)
    e_lo = g_idx * float(EXPERTS_PER_GROUP)
    e_mask = (lane_f >= e_lo) & (lane_f < e_lo + float(EXPERTS_PER_GROUP))
    e_log = jnp.where(e_mask, logits, NEG_INF)
    e_max1 = jnp.max(e_log, axis=-1, keepdims=True)
    idx1 = first_argmax(e_log, e_max1)
    e_log2 = jnp.where(lane_f == idx1, NEG_INF, e_log)
    e_max2 = jnp.max(e_log2, axis=-1, keepdims=True)
    idx2 = first_argmax(e_log2, e_max2)
    r = jnp.exp(e_max2 - e_max1)
    w1 = g_prob / (1.0 + r)
    w2 = g_prob * r / (1.0 + r)
    gates = jnp.where(lane_f == idx1, w1, 0.0) + jnp.where(lane_f == idx2, w2, 0.0)
    lo = jnp.minimum(idx1, idx2) - e_lo
    hi = jnp.maximum(idx1, idx2) - e_lo
    pair = lo * (2.0 * EXPERTS_PER_GROUP - 1.0 - lo) * 0.5 + (hi - lo - 1.0)
    return gates, g_idx * float(PAIRS_PER_GROUP) + pair


def _merge_kernel(x_ref, ya_ref, ys_ref, gain1_ref, wgate_ref, bgate_ref, wglu_ref, bglu_ref,
                  wba_ref, wbs_ref, wout_ref, gain2_ref, rcat_ref, rbias_ref, ltri_ref,
                  xg_ref, counts_ref, count_scr):
    x = x_ref[...]
    hn = _rms_norm(x, gain1_ref[...]).astype(BF16)
    z = jax.nn.gelu(ys_ref[...].astype(F32))
    glu = jnp.dot(z.astype(BF16), wglu_ref[...], preferred_element_type=F32) + bglu_ref[...]
    y_ssm = (z * jax.nn.sigmoid(glu)).astype(BF16)
    merged = None
    for s, (y_b, wb_ref) in enumerate(((ya_ref[...], wba_ref), (y_ssm, wbs_ref))):
        cols = slice(s * D_MODEL, (s + 1) * D_MODEL)
        gate = jax.nn.sigmoid(jnp.dot(hn, wgate_ref[:, cols], preferred_element_type=F32)
                              + bgate_ref[:, cols])
        term = gate * jnp.dot(y_b, wb_ref[...], preferred_element_type=F32)
        merged = term if merged is None else merged + term
    x1 = x + jnp.dot(merged.astype(BF16), wout_ref[...], preferred_element_type=F32)
    xg_ref[:, :D_MODEL] = x1

    h = _rms_norm(x1, gain2_ref[...])
    h_hi, h_lo = _split_bf16(h)
    hi_terms = jnp.dot(h_hi, rcat_ref[...], preferred_element_type=F32)
    logits = (hi_terms[:, :ROUTER_LANES] + hi_terms[:, ROUTER_LANES:]
              + jnp.dot(h_lo, rcat_ref[:, :ROUTER_LANES], preferred_element_type=F32)
              + rbias_ref[...])
    gates, cls = _route(logits)

    @pl.when(pl.program_id(0) == 0)
    def _():
        count_scr[...] = jnp.zeros_like(count_scr)

    lane_f = lax.broadcasted_iota(jnp.int32, gates.shape, 1).astype(F32)
    onehot = jnp.where(lane_f == cls, 1.0, 0.0)
    before = jnp.dot(ltri_ref[...], onehot.astype(BF16), preferred_element_type=F32)
    seen = count_scr[...]
    rank = jnp.sum(onehot * (before + seen), axis=-1, keepdims=True)
    count_scr[...] = seen + jnp.sum(onehot, axis=0, keepdims=True)
    counts_ref[...] = count_scr[...]
    xg_ref[:, D_MODEL:] = (gates + jnp.where(lane_f == float(CLASS_LANE), cls, 0.0)
                           + jnp.where(lane_f == float(RANK_LANE), rank, 0.0))


def _merge(x2, y_attn, y_pre, gain1, w_gate, b_gate, w_glu, b_glu, w_branch, w_out, ffn_gain,
           w_group_router, group_bias, w_expert_router, expert_bias, tm):
    t = x2.shape[0]
    pad = ROUTER_LANES - N_EXPERTS - N_GROUPS
    w_r = jnp.concatenate([w_expert_router, w_group_router,
                           jnp.zeros((D_MODEL, pad), F32)], axis=1)
    r_cat = jnp.concatenate(_split_bf16(w_r), axis=1)
    r_bias = jnp.concatenate([expert_bias, group_bias, jnp.zeros((pad,), F32)])[None, :]
    ltri = jnp.tril(jnp.ones((tm, tm), F32), -1).astype(BF16)
    row = lambda n: pl.BlockSpec((tm, n), lambda i: (i, 0))
    resident = lambda shape: pl.BlockSpec(shape, lambda i: (0, 0), pipeline_mode=pl.Buffered(1))
    return pl.pallas_call(
        _merge_kernel,
        grid=(t // tm,),
        in_specs=[row(D_MODEL), row(D_ATTN), row(D_SSM),
                  resident((1, D_MODEL)), resident((D_MODEL, N_BRANCH * D_MODEL)),
                  resident((1, N_BRANCH * D_MODEL)),
                  resident((D_SSM, D_SSM)), resident((1, D_SSM)),
                  resident((D_ATTN, D_MODEL)), resident((D_SSM, D_MODEL)),
                  resident((D_MODEL, D_MODEL)), resident((1, D_MODEL)),
                  resident((D_MODEL, 2 * ROUTER_LANES)),
                  resident((1, ROUTER_LANES)), resident((tm, tm))],
        out_specs=[row(D_ROUTED), pl.BlockSpec((1, ROUTER_LANES), lambda i: (0, 0))],
        out_shape=[jax.ShapeDtypeStruct((t, D_ROUTED), F32),
                   jax.ShapeDtypeStruct((1, ROUTER_LANES), F32)],
        scratch_shapes=[pltpu.VMEM((1, ROUTER_LANES), F32)],
        compiler_params=_compiler_params(1),
        name="merge",
    )(x2, y_attn, y_pre, gain1[None, :], w_gate.astype(BF16), b_gate[None, :],
      w_glu.astype(BF16), b_glu[None, :],
      w_branch[:D_ATTN].astype(BF16), w_branch[D_ATTN:].astype(BF16),
      w_out.astype(BF16), ffn_gain[None, :], r_cat, r_bias, ltri)


def _moe_plan(routed, counts, n_items):
    i32 = jnp.int32
    cnt = counts[0, :N_CLASSES].astype(i32)
    ends = jnp.cumsum(cnt)
    starts = ends - cnt
    classes = jnp.arange(N_CLASSES, dtype=i32)
    cls = routed[:, D_MODEL + CLASS_LANE].astype(i32)
    rank = routed[:, D_MODEL + RANK_LANE].astype(i32)

    first_tile = starts // MOE_TILE
    n_class_items = jnp.where(cnt > 0, (ends - 1) // MOE_TILE - first_tile + 1, 0)
    item_end = jnp.cumsum(n_class_items)
    item_start = item_end - n_class_items
    n_used = item_end[-1]
    item = jnp.arange(n_items, dtype=i32)
    item_cls = jnp.minimum(jnp.sum(item[:, None] >= item_end[None, :], axis=1), N_CLASSES - 1)
    pick = lambda table: jnp.sum(
        jnp.where(item_cls[:, None] == classes[None, :], jnp.asarray(table, i32)[None, :], 0),
        axis=1).astype(i32)
    valid = item < n_used
    last_tile = routed.shape[0] // MOE_TILE - 1
    tile = jnp.where(valid, pick(first_tile) + item - pick(item_start), last_tile)
    row0 = tile * MOE_TILE
    lo = jnp.where(valid, jnp.maximum(pick(starts), row0) - row0, 0)
    hi = jnp.where(valid, jnp.minimum(pick(ends), row0 + MOE_TILE) - row0, 0)
    prev_tile = jnp.concatenate([jnp.full((1,), -1, i32), tile[:-1]])
    first = (valid & (tile != prev_tile)).astype(i32)
    plan = (tile.astype(i32), pick([e for e, _ in _CLASS_EXPERTS]), pick([e for _, e in _CLASS_EXPERTS]),
            lo.astype(i32), hi.astype(i32), first, jnp.reshape(n_used, (1,)).astype(i32))
    return (starts.astype(i32), cls, rank), plan


def _row_copy(src, src_row, dst, dst_row, sem):
    return pltpu.make_async_copy(src.at[pl.ds(src_row, 1), :], dst.at[pl.ds(dst_row, 1), :], sem)


def _dispatch_kernel(starts_ref, cls_ref, rank_ref, src_ref, dst_hbm, dest_ref, sem, *, rows):
    for r in range(rows):
        dest = starts_ref[cls_ref[r]] + rank_ref[r]
        dest_ref[r] = dest
        _row_copy(src_ref, r, dst_hbm, dest, sem).start(priority=r % DMA_THREADS)
    for r in range(rows):
        _row_copy(src_ref, r, dst_hbm, 0, sem).wait()


def _dispatch(routed, starts, cls, rank, rows):
    t, width = routed.shape
    per_row = pl.BlockSpec((rows,), lambda i, starts: (i,), memory_space=pltpu.SMEM)
    return pl.pallas_call(
        functools.partial(_dispatch_kernel, rows=rows),
        grid_spec=pltpu.PrefetchScalarGridSpec(
            num_scalar_prefetch=1,
            grid=(t // rows,),
            in_specs=[per_row, per_row, pl.BlockSpec((rows, width), lambda i, starts: (i, 0))],
            out_specs=[pl.BlockSpec(memory_space=pl.ANY), per_row],
            scratch_shapes=[pltpu.SemaphoreType.DMA(())]),
        out_shape=[jax.ShapeDtypeStruct((t, width), F32), jax.ShapeDtypeStruct((t,), jnp.int32)],
        compiler_params=_compiler_params(1),
        name="moe_dispatch",
    )(starts, cls, rank, routed)


def _expert_kernel(tile_ref, ea_ref, eb_ref, lo_ref, hi_ref, first_ref, used_ref,
                   xs_ref, gain_ref, wga_ref, wua_ref, wda_ref, wgb_ref, wub_ref, wdb_ref, o_ref):
    i = pl.program_id(0)

    @pl.when(i < used_ref[0])
    def _():
        x1 = xs_ref[:, :D_MODEL]
        gates = xs_ref[:, D_MODEL:]
        h = _rms_norm(x1, gain_ref[...]).astype(BF16)
        lane = lax.broadcasted_iota(jnp.int32, gates.shape, 1)
        row = lax.broadcasted_iota(jnp.int32, gates.shape, 0)
        mine = (row >= lo_ref[i]) & (row < hi_ref[i])
        moe = None
        for e_ref, wg_ref, wu_ref, wd_ref in ((ea_ref, wga_ref, wua_ref, wda_ref),
                                              (eb_ref, wgb_ref, wub_ref, wdb_ref)):
            gate = jnp.sum(jnp.where(mine & (lane == e_ref[i]), gates, 0.0), axis=-1, keepdims=True)
            a = jnp.dot(h, wg_ref[...], preferred_element_type=F32)
            u = jnp.dot(h, wu_ref[...], preferred_element_type=F32)
            hid = (jax.nn.silu(a) * u * gate).astype(BF16)
            term = jnp.dot(hid, wd_ref[...], preferred_element_type=F32)
            moe = term if moe is None else moe + term

        @pl.when(first_ref[i] == 1)
        def _():
            o_ref[...] = x1 + moe

        @pl.when(first_ref[i] == 0)
        def _():
            o_ref[...] += moe


def _experts(xs, plan, gain, w_e_gate, w_e_up, w_e_down):
    n_items = plan[0].shape[0]
    by = lambda which, r, c: pl.BlockSpec((None, r, c), lambda i, *p: (p[which][i], 0, 0))
    tile_rows = lambda width: pl.BlockSpec((MOE_TILE, width), lambda i, *p: (p[0][i], 0))
    wg, wu, wd = (w.astype(BF16) for w in (w_e_gate, w_e_up, w_e_down))
    return pl.pallas_call(
        _expert_kernel,
        grid_spec=pltpu.PrefetchScalarGridSpec(
            num_scalar_prefetch=len(plan),
            grid=(n_items,),
            in_specs=[tile_rows(D_ROUTED),
                      pl.BlockSpec((1, D_MODEL), lambda i, *p: (0, 0)),
                      by(1, D_MODEL, D_EXPERT), by(1, D_MODEL, D_EXPERT), by(1, D_EXPERT, D_MODEL),
                      by(2, D_MODEL, D_EXPERT), by(2, D_MODEL, D_EXPERT), by(2, D_EXPERT, D_MODEL)],
            out_specs=tile_rows(D_MODEL)),
        out_shape=jax.ShapeDtypeStruct((xs.shape[0], D_MODEL), F32),
        compiler_params=_compiler_params(1),
        name="moe_experts",
    )(*plan, xs, gain[None, :], wg, wu, wd, wg, wu, wd)


def _unsort_kernel(dest_ref, src_hbm, dst_ref, sem, *, rows):
    for r in range(rows):
        _row_copy(src_hbm, dest_ref[r], dst_ref, r, sem).start(priority=r % DMA_THREADS)
    for r in range(rows):
        _row_copy(src_hbm, 0, dst_ref, r, sem).wait()


def _unsort(xo, dest, rows):
    t = dest.shape[0]
    return pl.pallas_call(
        functools.partial(_unsort_kernel, rows=rows),
        grid=(t // rows,),
        in_specs=[pl.BlockSpec((rows,), lambda i: (i,), memory_space=pltpu.SMEM),
                  pl.BlockSpec(memory_space=pl.ANY)],
        out_specs=pl.BlockSpec((rows, xo.shape[1]), lambda i: (i, 0)),
        out_shape=jax.ShapeDtypeStruct((t, xo.shape[1]), F32),
        scratch_shapes=[pltpu.SemaphoreType.DMA(())],
        compiler_params=_compiler_params(1),
        name="moe_unsort",
    )(dest, xo)


def _moe(routed, counts, ffn_gain, w_e_gate, w_e_up, w_e_down, rows):
    t = routed.shape[0]
    assert t % MOE_TILE == 0 and t % rows == 0
    n_items = t // MOE_TILE + N_CLASSES
    placement, plan = _moe_plan(routed, counts, n_items)
    xs, dest = _dispatch(routed, *placement, rows)
    xo = _experts(xs, plan, ffn_gain, w_e_gate, w_e_up, w_e_down)
    return _unsort(xo, dest, rows)


def kernel(x, mix_norm_gain, w_in, b_gate, q_gain, k_gain, rel_bias, ssm_lambda_re, ssm_lambda_im, ssm_log_step, ssm_b_re, ssm_b_im, ssm_c_re, ssm_c_im, ssm_d, w_glu, b_glu, w_branch, w_out, ffn_norm_gain, w_group_router, group_bias, w_expert_router, expert_bias, w_e_gate, w_e_up, w_e_down):
    bsz, seq, _ = x.shape
    assert seq % (SSM_KBLOCK * SSM_CHUNK) == 0 and seq % (4 * CHUNK) == 0
    depth = w_in.shape[0]
    tiles = _tiles(bsz * seq)
    x2 = x.reshape(bsz * seq, D_MODEL)
    for i in range(depth):
        w_qkv = w_in[i][:, :D_QKV]
        w_u = w_in[i][:, D_QKV:D_QKV + D_SSM]
        w_gate = w_in[i][:, D_QKV + D_SSM:]
        y_attn = _attention(x2, mix_norm_gain[i], w_qkv, q_gain[i], k_gain[i],
                            _rel_bias_table(rel_bias[i]), bsz, seq)
        u_t = _ssm_inproj(x2, mix_norm_gain[i], w_u, bsz, seq)
        tables = _ssm_tables(ssm_lambda_re[i], ssm_lambda_im[i], ssm_log_step[i],
                             ssm_b_re[i], ssm_b_im[i], ssm_c_re[i], ssm_c_im[i], ssm_d[i])
        y_pre = _ssm_out(_ssm(u_t, tables, bsz), bsz, seq)
        routed, counts = _merge(x2, y_attn, y_pre, mix_norm_gain[i], w_gate, b_gate[i], w_glu[i],
                                b_glu[i], w_branch[i], w_out[i], ffn_norm_gain[i],
                                w_group_router[i], group_bias[i], w_expert_router[i],
                                expert_bias[i], tiles["merge"])
        x2 = _moe(routed, counts, ffn_norm_gain[i], w_e_gate[i], w_e_up[i], w_e_down[i],
                  tiles["moe_rows"])
    return x2.reshape(bsz, seq, D_MODEL)
```

```python
import functools

import jax
import jax.numpy as jnp
from jax import lax
from jax.experimental import pallas as pl
from jax.experimental.pallas import tpu as pltpu

F32 = jnp.float32
BF16 = jnp.bfloat16

D_MODEL = 1024
CHUNK = 64
LEFT_CHUNKS = 8
BAND = LEFT_CHUNKS + 1
BAND_KEYS = BAND * CHUNK
PAD_KEYS = LEFT_CHUNKS * CHUNK
ATTN_HEADS = 8
HEAD_DIM = 64
D_ATTN = ATTN_HEADS * HEAD_DIM
MAX_REL = 256
REL_FUTURE = CHUNK - 1
D_SSM = D_MODEL // 2
SSM_GROUP = 16
SSM_GROUPS = D_SSM // SSM_GROUP
SSM_STATE = 64
N_BRANCH = 2
D_QKV = 3 * D_ATTN
D_IN = D_QKV + D_SSM + N_BRANCH * D_MODEL
N_GROUPS = 4
EXPERTS_PER_GROUP = 4
N_EXPERTS = N_GROUPS * EXPERTS_PER_GROUP
D_EXPERT = D_MODEL // 4
EPS = 1e-6
NEG_INF = -1e30
LOG2_E = 1.4426950408889634

LANES = 128
HEAD_PAIRS = D_ATTN // LANES
PROJ_ROWS = 512
EVEN_SUM_LANE = HEAD_DIM
ODD_SUM_LANE = 0
SSM_CHUNK = 16
SSM_ROW = SSM_CHUNK * SSM_GROUP
SSM_KBLOCK = 16
SSM_BGROUP = LANES // SSM_KBLOCK
ROUTER_LANES = LANES
D_ROUTED = D_MODEL + ROUTER_LANES
CLASS_LANE = 32
RANK_LANE = 33
MOE_TILE = 512
DMA_THREADS = 2
_PAIRS = [(a, b) for a in range(EXPERTS_PER_GROUP) for b in range(a + 1, EXPERTS_PER_GROUP)]
PAIRS_PER_GROUP = len(_PAIRS)
N_CLASSES = N_GROUPS * PAIRS_PER_GROUP
_CLASS_EXPERTS = [(g * EXPERTS_PER_GROUP + a, g * EXPERTS_PER_GROUP + b)
                  for g in range(N_GROUPS) for a, b in _PAIRS]
VMEM_LIMIT = 56 * 1024 * 1024


def _tiles(n_tokens):
    return dict(merge=512, moe_rows=2048)


def _compiler_params(n_axes):
    return pltpu.CompilerParams(
        dimension_semantics=("arbitrary",) * n_axes,
        vmem_limit_bytes=VMEM_LIMIT)


def _full(shape):
    nd = len(shape)
    return pl.BlockSpec(shape, lambda *_: (0,) * nd)


def _rms_norm(x, gain):
    ms = jnp.mean(x * x, axis=-1, keepdims=True)
    return x * lax.rsqrt(ms + EPS) * gain


NT_DIMS = (((1,), (1,)), ((), ()))


def _attn_kernel(x_ref, gain_ref, w_ref, qg_ref, kg_ref, ones_ref, bias_ref, o_ref,
                 q_scr, kpad, v_even, v_odd, s_a, s_b):
    seq = x_ref.shape[0]
    n_chunks = seq // CHUNK
    zeros = jnp.zeros((PAD_KEYS, D_ATTN), BF16)
    kpad[0:PAD_KEYS, :] = zeros
    v_even[0:PAD_KEYS, :] = zeros
    v_odd[0:PAD_KEYS, :] = zeros

    lane_v = lax.broadcasted_iota(jnp.int32, (PROJ_ROWS, D_ATTN), 1)
    even_v = (lane_v % LANES) < HEAD_DIM
    ones_even = (lane_v % LANES) == EVEN_SUM_LANE
    ones_odd = (lane_v % LANES) == ODD_SUM_LANE
    for c in range(seq // PROJ_ROWS):
        rows = slice(c * PROJ_ROWS, (c + 1) * PROJ_ROWS)
        kv_rows = slice(PAD_KEYS + c * PROJ_ROWS, PAD_KEYS + (c + 1) * PROJ_ROWS)
        h = _rms_norm(x_ref[rows, :], gain_ref[...]).astype(BF16)

        def proj(lo):
            return jnp.dot(h, w_ref[:, lo:lo + D_ATTN], preferred_element_type=F32)

        def head_norm(t, g_ref):
            ss = jnp.dot((t * t).astype(BF16), ones_ref[...], preferred_element_type=F32)
            return t * lax.rsqrt(ss * (1.0 / HEAD_DIM) + EPS) * g_ref[...]

        q_scr[rows, :] = head_norm(proj(0), qg_ref).astype(BF16)
        kpad[kv_rows, :] = head_norm(proj(D_ATTN), kg_ref).astype(BF16)
        v = proj(2 * D_ATTN).astype(BF16)
        v_even[kv_rows, :] = jnp.where(even_v, v, jnp.where(ones_even, 1.0, 0.0).astype(BF16))
        v_odd[kv_rows, :] = jnp.where(even_v, jnp.where(ones_odd, 1.0, 0.0).astype(BF16), v)

    lane = lax.broadcasted_iota(jnp.int32, (CHUNK, LANES), 1)
    first_head = lane < HEAD_DIM

    def scores(n, s_buf, skip=0):
        r0 = n * CHUNK if isinstance(n, int) else pl.multiple_of(n * CHUNK, CHUNK)
        for j in range(HEAD_PAIRS):
            cols = slice(j * LANES, (j + 1) * LANES)
            qp = q_scr[pl.ds(r0, CHUNK), cols]
            zq = jnp.zeros_like(qp)
            q2 = jnp.concatenate([jnp.where(first_head, qp, zq), jnp.where(first_head, zq, qp)], axis=0)
            kb = kpad[pl.ds(r0 + skip, BAND_KEYS - skip), cols]
            s_buf[j * 2 * CHUNK:(j + 1) * 2 * CHUNK, skip:] = lax.dot_general(
                q2, kb, NT_DIMS, preferred_element_type=F32)

    def softmax_pv(n, s_buf, skip=0, first_valid=0):
        r0 = n * CHUNK if isinstance(n, int) else pl.multiple_of(n * CHUNK, CHUNK)
        width = BAND_KEYS - skip
        for j in range(HEAD_PAIRS):
            cols = slice(j * LANES, (j + 1) * LANES)
            rows = slice(j * 2 * CHUNK, (j + 1) * 2 * CHUNK)
            s = s_buf[rows, skip:] + bias_ref[rows, skip:]
            if first_valid > skip:
                key_col = lax.broadcasted_iota(jnp.int32, s.shape, 1) + skip
                s = jnp.where(key_col >= first_valid, s, NEG_INF)
            m = jnp.max(s, axis=-1, keepdims=True)
            pb = jnp.exp2(s - m).astype(BF16)
            o0 = jnp.dot(pb[:CHUNK], v_even[pl.ds(r0 + skip, width), cols], preferred_element_type=F32)
            o1 = jnp.dot(pb[CHUNK:], v_odd[pl.ds(r0 + skip, width), cols], preferred_element_type=F32)
            l0 = o0[:, EVEN_SUM_LANE:EVEN_SUM_LANE + 1]
            l1 = o1[:, ODD_SUM_LANE:ODD_SUM_LANE + 1]
            o_ref[pl.ds(r0, CHUNK), cols] = jnp.where(first_head, o0 / l0, o1 / l1).astype(BF16)

    assert n_chunks % 4 == 0 and LEFT_CHUNKS % 4 == 0 and n_chunks > LEFT_CHUNKS

    def band_start(n):
        first_valid = max(LEFT_CHUNKS - n, 0) * CHUNK
        return first_valid // LANES * LANES, first_valid

    bufs = (s_a, s_b)
    scores(0, s_a, band_start(0)[0])
    for n in range(LEFT_CHUNKS):
        scores(n + 1, bufs[(n + 1) % 2], band_start(n + 1)[0])
        softmax_pv(n, bufs[n % 2], *band_start(n))

    def pair_body(i, carry):
        n0 = 4 * i
        scores(n0 + 1, s_b)
        softmax_pv(n0, s_a)
        scores(n0 + 2, s_a)
        softmax_pv(n0 + 1, s_b)
        scores(n0 + 3, s_b)
        softmax_pv(n0 + 2, s_a)
        scores(jnp.minimum(n0 + 4, n_chunks - 1), s_a)
        softmax_pv(n0 + 3, s_b)
        return carry

    lax.fori_loop(LEFT_CHUNKS // 4, n_chunks // 4, pair_body, 0)


def _attention(x2, gain, w_qkv, q_gain, k_gain, bias, bsz, seq):
    assert seq % PROJ_ROWS == 0
    ones = jnp.kron(jnp.eye(ATTN_HEADS, dtype=F32), jnp.ones((HEAD_DIM, HEAD_DIM), F32)).astype(BF16)
    qg = jnp.tile(q_gain, ATTN_HEADS)[None, :] * (HEAD_DIM ** -0.5 * LOG2_E)
    kg = jnp.tile(k_gain, ATTN_HEADS)[None, :]
    seq_spec = lambda width: pl.BlockSpec((None, seq, width), lambda b: (b, 0, 0))
    padded = pltpu.VMEM((PAD_KEYS + seq, D_ATTN), BF16)
    out = pl.pallas_call(
        _attn_kernel,
        grid=(bsz,),
        in_specs=[seq_spec(D_MODEL), _full((1, D_MODEL)), _full((D_MODEL, D_QKV)),
                  _full((1, D_ATTN)), _full((1, D_ATTN)), _full((D_ATTN, D_ATTN)),
                  _full((ATTN_HEADS * CHUNK, BAND_KEYS))],
        out_specs=seq_spec(D_ATTN),
        out_shape=jax.ShapeDtypeStruct((bsz, seq, D_ATTN), BF16),
        scratch_shapes=[pltpu.VMEM((seq, D_ATTN), BF16), padded, padded, padded,
                        pltpu.VMEM((ATTN_HEADS * CHUNK, BAND_KEYS), F32),
                        pltpu.VMEM((ATTN_HEADS * CHUNK, BAND_KEYS), F32)],
        compiler_params=_compiler_params(1),
        name="attention",
    )(x2.reshape(bsz, seq, D_MODEL), gain[None, :], w_qkv.astype(BF16), qg, kg, ones, bias)
    return out.reshape(bsz * seq, D_ATTN)


def _rel_bias_table(rel_bias):
    n_rel = REL_FUTURE + MAX_REL + 1
    far = jnp.broadcast_to(rel_bias[:, n_rel - 1:], (ATTN_HEADS, PAD_KEYS - MAX_REL + CHUNK))
    ext = jnp.concatenate([far, rel_bias[:, n_rel - 2::-1]], axis=1) * LOG2_E
    rows = [ext[:, CHUNK - 1 - q:CHUNK - 1 - q + BAND_KEYS] for q in range(CHUNK)]
    return jnp.stack(rows, axis=1).reshape(ATTN_HEADS * CHUNK, BAND_KEYS)


SSM_BLOCK_LEN = SSM_KBLOCK * SSM_CHUNK
SSM_BLOCK_TOK = SSM_BGROUP * SSM_KBLOCK


def _token_block_spec(width):
    return pl.BlockSpec((SSM_BGROUP, SSM_BLOCK_LEN, width), lambda kb, bg: (bg, kb, 0))


def _transposed_spec(n_bgroups):
    return pl.BlockSpec((SSM_GROUPS, SSM_ROW, SSM_BLOCK_TOK),
                        lambda kb, bg: (0, 0, kb * n_bgroups + bg))


def _token_grid(bsz, seq):
    assert bsz % SSM_BGROUP == 0 and seq % SSM_BLOCK_LEN == 0
    return (seq // SSM_BLOCK_LEN, bsz // SSM_BGROUP)


def _ssm_inproj_kernel(x_ref, gain_ref, wu_ref, ut_ref, u_scr):
    n_rows = SSM_BGROUP * SSM_BLOCK_LEN
    x = x_ref[...].reshape(n_rows, D_MODEL)
    h = _rms_norm(x, gain_ref[...]).astype(BF16)
    u = jnp.dot(h, wu_ref[...], preferred_element_type=F32)
    for q in range(D_SSM // LANES):
        u_scr[q] = u[:, q * LANES:(q + 1) * LANES]
    groups_per_tile = LANES // SSM_GROUP
    for t in range(SSM_CHUNK):
        for q in range(D_SSM // LANES):
            ut = u_scr[q, pl.ds(t, SSM_BLOCK_TOK, stride=SSM_CHUNK), :].T.astype(BF16)
            for gl in range(groups_per_tile):
                ut_ref[q * groups_per_tile + gl, t * SSM_GROUP:(t + 1) * SSM_GROUP, :] = (
                    ut[gl * SSM_GROUP:(gl + 1) * SSM_GROUP, :])


def _ssm_inproj(x2, gain, w_u, bsz, seq):
    grid = _token_grid(bsz, seq)
    return pl.pallas_call(
        _ssm_inproj_kernel,
        grid=grid,
        in_specs=[_token_block_spec(D_MODEL), _full((1, D_MODEL)), _full((D_MODEL, D_SSM))],
        out_specs=_transposed_spec(grid[1]),
        out_shape=jax.ShapeDtypeStruct((SSM_GROUPS, SSM_ROW, bsz * seq // SSM_CHUNK), BF16),
        scratch_shapes=[pltpu.VMEM((D_SSM // LANES, SSM_BGROUP * SSM_BLOCK_LEN, LANES), F32)],
        compiler_params=_compiler_params(2),
        name="ssm_inproj",
    )(x2.reshape(bsz, seq, D_MODEL), gain[None, :], w_u.astype(BF16))


def _ssm_out_kernel(yt_ref, y_ref, y_scr):
    groups_per_tile = LANES // SSM_GROUP
    for t in range(SSM_CHUNK):
        for q in range(D_SSM // LANES):
            yt = jnp.concatenate(
                [yt_ref[q * groups_per_tile + gl, t * SSM_GROUP:(t + 1) * SSM_GROUP, :]
                 for gl in range(groups_per_tile)], axis=0)
            y_scr[q, pl.ds(t, SSM_BLOCK_TOK, stride=SSM_CHUNK), :] = yt.astype(F32).T
    for q in range(D_SSM // LANES):
        y_ref[:, :, q * LANES:(q + 1) * LANES] = y_scr[q].astype(BF16).reshape(
            SSM_BGROUP, SSM_BLOCK_LEN, LANES)


def _ssm_out(y_t, bsz, seq):
    grid = _token_grid(bsz, seq)
    y = pl.pallas_call(
        _ssm_out_kernel,
        grid=grid,
        in_specs=[_transposed_spec(grid[1])],
        out_specs=_token_block_spec(D_SSM),
        out_shape=jax.ShapeDtypeStruct((bsz, seq, D_SSM), BF16),
        scratch_shapes=[pltpu.VMEM((D_SSM // LANES, SSM_BGROUP * SSM_BLOCK_LEN, LANES), F32)],
        compiler_params=_compiler_params(2),
        name="ssm_out",
    )(y_t)
    return y.reshape(bsz * seq, D_SSM)


def _ssm_tables(lambda_re, lambda_im, log_step, b_re, b_im, c_re, c_im, d_skip):
    step = jnp.exp(log_step)[:, None]
    mag = jnp.exp(lambda_re * step)
    ang = lambda_im * step
    a_re = mag * jnp.cos(ang)
    a_im = mag * jnp.sin(ang)
    num_re = a_re - 1.0
    num_im = a_im
    den = lambda_re * lambda_re + lambda_im * lambda_im
    f_re = (num_re * lambda_re + num_im * lambda_im) / den
    f_im = (num_im * lambda_re - num_re * lambda_im) / den
    bb_re = f_re[..., None] * b_re - f_im[..., None] * b_im
    bb_im = f_re[..., None] * b_im + f_im[..., None] * b_re
    log_mag = lambda_re * step

    def powers(tau):
        tau = jnp.asarray(tau, F32)[:, None, None]
        pmag = jnp.exp(tau * log_mag[None])
        return pmag * jnp.cos(tau * ang[None]), pmag * jnp.sin(tau * ang[None])

    steps = jnp.arange(SSM_CHUNK)
    up_re, up_im = powers(steps + 1)
    cp_re = c_re[None] * up_re[:, :, None, :] - c_im[None] * up_im[:, :, None, :]
    cp_im = c_re[None] * up_im[:, :, None, :] + c_im[None] * up_re[:, :, None, :]
    readout = jnp.concatenate([cp_re, -cp_im], axis=-1)
    readout = jnp.transpose(readout, (1, 0, 2, 3)).reshape(SSM_GROUPS, SSM_ROW, 2 * SSM_STATE)

    def inject(tau):
        p_re, p_im = powers(tau)
        bt_re = jnp.transpose(bb_re, (0, 2, 1))[None]
        bt_im = jnp.transpose(bb_im, (0, 2, 1))[None]
        i_re = p_re[:, :, None, :] * bt_re - p_im[:, :, None, :] * bt_im
        i_im = p_re[:, :, None, :] * bt_im + p_im[:, :, None, :] * bt_re
        both = jnp.concatenate([i_re, i_im], axis=-1)
        return jnp.transpose(both, (1, 3, 0, 2)).reshape(SSM_GROUPS, 2 * SSM_STATE, SSM_ROW)

    inject_back = inject(-(steps + 1.0))
    inject_end = inject(SSM_CHUNK - 1.0 - steps).astype(BF16)
    d_row = jnp.tile(d_skip.reshape(SSM_GROUPS, 1, SSM_GROUP), (1, 1, SSM_CHUNK))
    (ar,), (ai,) = powers([float(SSM_CHUNK)])
    a_chunk = jnp.stack([jnp.concatenate([ar, ar], -1), jnp.concatenate([-ai, ai], -1),
                         jnp.concatenate([ai, -ai], -1)], axis=1)
    return readout, inject_back, inject_end, d_row, a_chunk


def _ssm_kernel(ut_ref, ro_ref, ib_ref, ie_ref, d_ref, a_ref, yt_ref, p_scr, q_scr, s_scr, *, bsz):
    ut = ut_ref[...]
    n_tok = ut.shape[1]
    r_hi, r_lo = _split_bf16(ro_ref[...])
    i_hi, i_lo = _split_bf16(ib_ref[...])
    kern = (jnp.dot(r_hi, i_hi, preferred_element_type=F32)
            + jnp.dot(r_hi, i_lo, preferred_element_type=F32)
            + jnp.dot(r_lo, i_hi, preferred_element_type=F32))
    row = lax.broadcasted_iota(jnp.int32, kern.shape, 0)
    col = lax.broadcasted_iota(jnp.int32, kern.shape, 1)
    group_shift = SSM_GROUP.bit_length() - 1
    causal = jnp.right_shift(row, group_shift) >= jnp.right_shift(col, group_shift)
    kern = jnp.where(causal, kern, 0.0) + jnp.where(row == col, d_ref[...], 0.0)
    wt = jnp.dot(ie_ref[...], ut, preferred_element_type=F32)
    w = wt.T
    p_scr[...] = w
    q_scr[...] = pltpu.roll(w, SSM_STATE, 1)
    a1 = a_ref[0:1, :]
    a2 = a_ref[1:2, :]
    a3 = a_ref[2:3, :]

    def block(kb, carry):
        cp, cq = carry
        base = pl.multiple_of(kb * (bsz * SSM_KBLOCK), bsz * SSM_KBLOCK)
        for k in range(SSM_KBLOCK):
            rows = pl.ds(base + k, bsz, stride=SSM_KBLOCK)
            s_scr[rows, :] = cp
            cp, cq = (a1 * cp + a2 * cq + p_scr[rows, :],
                      a1 * cq + a3 * cp + q_scr[rows, :])
        return cp, cq

    zero = jnp.zeros((bsz, 2 * SSM_STATE), F32)
    lax.fori_loop(0, n_tok // (bsz * SSM_KBLOCK), block, (zero, zero))
    yt = jnp.dot(kern.astype(BF16), ut, preferred_element_type=F32)
    yt = yt + lax.dot_general(r_hi, s_scr[...].astype(BF16), NT_DIMS, preferred_element_type=F32)
    yt_ref[...] = yt.astype(BF16)


def _ssm(ut, tables, bsz):
    readout, inject_back, inject_end, d_row, a_chunk = tables
    n_tok = ut.shape[2]
    per_group = lambda a, b: pl.BlockSpec((None, a, b), lambda g: (g, 0, 0))
    state = pltpu.VMEM((n_tok, 2 * SSM_STATE), F32)
    return pl.pallas_call(
        functools.partial(_ssm_kernel, bsz=bsz),
        grid=(SSM_GROUPS,),
        in_specs=[per_group(SSM_ROW, n_tok), per_group(SSM_ROW, 2 * SSM_STATE),
                  per_group(2 * SSM_STATE, SSM_ROW), per_group(2 * SSM_STATE, SSM_ROW),
                  per_group(1, SSM_ROW), per_group(3, 2 * SSM_STATE)],
        out_specs=per_group(SSM_ROW, n_tok),
        out_shape=jax.ShapeDtypeStruct((SSM_GROUPS, SSM_ROW, n_tok), BF16),
        scratch_shapes=[state, state, state],
        compiler_params=_compiler_params(1),
        name="ssm",
    )(ut, readout, inject_back, inject_end, d_row, a_chunk)


def _split_bf16(a):
    hi = a.astype(BF16)
    lo = (a - hi.astype(F32)).astype(BF16)
    return hi, lo


def _route(logits):
    lane = lax.broadcasted_iota(jnp.int32, logits.shape, 1)
    lane_f = lane.astype(F32)
    big = float(ROUTER_LANES)

    def first_argmax(vals, vmax):
        return jnp.min(jnp.where(vals == vmax, lane_f, big), axis=-1, keepdims=True)

    g_mask = (lane >= N_EXPERTS) & (lane < N_EXPERTS + N_GROUPS)
    g_log = jnp.where(g_mask, logits, NEG_INF)
    g_max = jnp.max(g_log, axis=-1, keepdims=True)
    g_sum = jnp.sum(jnp.where(g_mask, jnp.exp(g_log - g_max), 0.0), axis=-1, keepdims=True)
    g_prob = 1.0 / g_sum
    g_idx = first_argmax(g_log, g_max) - float(N_EXPERTS)
    e_lo = g_idx * float(EXPERTS_PER_GROUP)
    e_mask = (lane_f >= e_lo) & (lane_f < e_lo + float(EXPERTS_PER_GROUP))
    e_log = jnp.where(e_mask, logits, NEG_INF)
    e_max1 = jnp.max(e_log, axis=-1, keepdims=True)
    idx1 = first_argmax(e_log, e_max1)
    e_log2 = jnp.where(lane_f == idx1, NEG_INF, e_log)
    e_max2 = jnp.max(e_log2, axis=-1, keepdims=True)
    idx2 = first_argmax(e_log2, e_max2)
    r = jnp.exp(e_max2 - e_max1)
    w1 = g_prob / (1.0 + r)
    w2 = g_prob * r / (1.0 + r)
    gates = jnp.where(lane_f == idx1, w1, 0.0) + jnp.where(lane_f == idx2, w2, 0.0)
    lo = jnp.minimum(idx1, idx2) - e_lo
    hi = jnp.maximum(idx1, idx2) - e_lo
    pair = lo * (2.0 * EXPERTS_PER_GROUP - 1.0 - lo) * 0.5 + (hi - lo - 1.0)
    return gates, g_idx * float(PAIRS_PER_GROUP) + pair


def _merge_kernel(x_ref, ya_ref, ys_ref, gain1_ref, wgate_ref, bgate_ref, wglu_ref, bglu_ref,
                  wba_ref, wbs_ref, wout_ref, gain2_ref, rcat_ref, rbias_ref, ltri_ref,
                  xg_ref, counts_ref, count_scr):
    x = x_ref[...]
    hn = _rms_norm(x, gain1_ref[...]).astype(BF16)
    z = jax.nn.gelu(ys_ref[...].astype(F32))
    glu = jnp.dot(z.astype(BF16), wglu_ref[...], preferred_element_type=F32) + bglu_ref[...]
    y_ssm = (z * jax.nn.sigmoid(glu)).astype(BF16)
    merged = None
    for s, (y_b, wb_ref) in enumerate(((ya_ref[...], wba_ref), (y_ssm, wbs_ref))):
        cols = slice(s * D_MODEL, (s + 1) * D_MODEL)
        gate = jax.nn.sigmoid(jnp.dot(hn, wgate_ref[:, cols], preferred_element_type=F32)
                              + bgate_ref[:, cols])
        term = gate * jnp.dot(y_b, wb_ref[...], preferred_element_type=F32)
        merged = term if merged is None else merged + term
    x1 = x + jnp.dot(merged.astype(BF16), wout_ref[...], preferred_element_type=F32)
    xg_ref[:, :D_MODEL] = x1

    h = _rms_norm(x1, gain2_ref[...])
    h_hi, h_lo = _split_bf16(h)
    hi_terms = jnp.dot(h_hi, rcat_ref[...], preferred_element_type=F32)
    logits = (hi_terms[:, :ROUTER_LANES] + hi_terms[:, ROUTER_LANES:]
              + jnp.dot(h_lo, rcat_ref[:, :ROUTER_LANES], preferred_element_type=F32)
              + rbias_ref[...])
    gates, cls = _route(logits)

    @pl.when(pl.program_id(0) == 0)
    def _():
        count_scr[...] = jnp.zeros_like(count_scr)

    lane_f = lax.broadcasted_iota(jnp.int32, gates.shape, 1).astype(F32)
    onehot = jnp.where(lane_f == cls, 1.0, 0.0)
    before = jnp.dot(ltri_ref[...], onehot.astype(BF16), preferred_element_type=F32)
    seen = count_scr[...]
    rank = jnp.sum(onehot * (before + seen), axis=-1, keepdims=True)
    count_scr[...] = seen + jnp.sum(onehot, axis=0, keepdims=True)
    counts_ref[...] = count_scr[...]
    xg_ref[:, D_MODEL:] = (gates + jnp.where(lane_f == float(CLASS_LANE), cls, 0.0)
                           + jnp.where(lane_f == float(RANK_LANE), rank, 0.0))


def _merge(x2, y_attn, y_pre, gain1, w_gate, b_gate, w_glu, b_glu, w_branch, w_out, ffn_gain,
           w_group_router, group_bias, w_expert_router, expert_bias, tm):
    t = x2.shape[0]
    pad = ROUTER_LANES - N_EXPERTS - N_GROUPS
    w_r = jnp.concatenate([w_expert_router, w_group_router,
                           jnp.zeros((D_MODEL, pad), F32)], axis=1)
    r_cat = jnp.concatenate(_split_bf16(w_r), axis=1)
    r_bias = jnp.concatenate([expert_bias, group_bias, jnp.zeros((pad,), F32)])[None, :]
    ltri = jnp.tril(jnp.ones((tm, tm), F32), -1).astype(BF16)
    row = lambda n: pl.BlockSpec((tm, n), lambda i: (i, 0))
    resident = lambda shape: pl.BlockSpec(shape, lambda i: (0, 0), pipeline_mode=pl.Buffered(1))
    return pl.pallas_call(
        _merge_kernel,
        grid=(t // tm,),
        in_specs=[row(D_MODEL), row(D_ATTN), row(D_SSM),
                  resident((1, D_MODEL)), resident((D_MODEL, N_BRANCH * D_MODEL)),
                  resident((1, N_BRANCH * D_MODEL)),
                  resident((D_SSM, D_SSM)), resident((1, D_SSM)),
                  resident((D_ATTN, D_MODEL)), resident((D_SSM, D_MODEL)),
                  resident((D_MODEL, D_MODEL)), resident((1, D_MODEL)),
                  resident((D_MODEL, 2 * ROUTER_LANES)),
                  resident((1, ROUTER_LANES)), resident((tm, tm))],
        out_specs=[row(D_ROUTED), pl.BlockSpec((1, ROUTER_LANES), lambda i: (0, 0))],
        out_shape=[jax.ShapeDtypeStruct((t, D_ROUTED), F32),
                   jax.ShapeDtypeStruct((1, ROUTER_LANES), F32)],
        scratch_shapes=[pltpu.VMEM((1, ROUTER_LANES), F32)],
        compiler_params=_compiler_params(1),
        name="merge",
    )(x2, y_attn, y_pre, gain1[None, :], w_gate.astype(BF16), b_gate[None, :],
      w_glu.astype(BF16), b_glu[None, :],
      w_branch[:D_ATTN].astype(BF16), w_branch[D_ATTN:].astype(BF16),
      w_out.astype(BF16), ffn_gain[None, :], r_cat, r_bias, ltri)


def _moe_plan(routed, counts, n_items):
    i32 = jnp.int32
    cnt = counts[0, :N_CLASSES].astype(i32)
    ends = jnp.cumsum(cnt)
    starts = ends - cnt
    classes = jnp.arange(N_CLASSES, dtype=i32)
    cls = routed[:, D_MODEL + CLASS_LANE].astype(i32)
    rank = routed[:, D_MODEL + RANK_LANE].astype(i32)
    dest = rank + jnp.sum(jnp.where(cls[:, None] == classes[None, :], starts[None, :], 0), axis=1)

    first_tile = starts // MOE_TILE
    n_class_items = jnp.where(cnt > 0, (ends - 1) // MOE_TILE - first_tile + 1, 0)
    item_end = jnp.cumsum(n_class_items)
    item_start = item_end - n_class_items
    n_used = item_end[-1]
    item = jnp.arange(n_items, dtype=i32)
    item_cls = jnp.minimum(jnp.sum(item[:, None] >= item_end[None, :], axis=1), N_CLASSES - 1)
    pick = lambda table: jnp.sum(
        jnp.where(item_cls[:, None] == classes[None, :], jnp.asarray(table, i32)[None, :], 0),
        axis=1).astype(i32)
    valid = item < n_used
    last_tile = routed.shape[0] // MOE_TILE - 1
    tile = jnp.where(valid, pick(first_tile) + item - pick(item_start), last_tile)
    row0 = tile * MOE_TILE
    lo = jnp.where(valid, jnp.maximum(pick(starts), row0) - row0, 0)
    hi = jnp.where(valid, jnp.minimum(pick(ends), row0 + MOE_TILE) - row0, 0)
    prev_tile = jnp.concatenate([jnp.full((1,), -1, i32), tile[:-1]])
    first = (valid & (tile != prev_tile)).astype(i32)
    plan = (tile.astype(i32), pick([e for e, _ in _CLASS_EXPERTS]), pick([e for _, e in _CLASS_EXPERTS]),
            lo.astype(i32), hi.astype(i32), first, jnp.reshape(n_used, (1,)).astype(i32))
    return dest.astype(i32), plan


def _row_copy(src, src_row, dst, dst_row, sem):
    return pltpu.make_async_copy(src.at[pl.ds(src_row, 1), :], dst.at[pl.ds(dst_row, 1), :], sem)


def _dispatch_kernel(dest_ref, src_ref, dst_hbm, sem, *, rows):
    for r in range(rows):
        _row_copy(src_ref, r, dst_hbm, dest_ref[r], sem).start(priority=r % DMA_THREADS)
    for r in range(rows):
        _row_copy(src_ref, r, dst_hbm, 0, sem).wait()


def _dispatch(routed, dest, rows):
    t, width = routed.shape
    return pl.pallas_call(
        functools.partial(_dispatch_kernel, rows=rows),
        grid=(t // rows,),
        in_specs=[pl.BlockSpec((rows,), lambda i: (i,), memory_space=pltpu.SMEM),
                  pl.BlockSpec((rows, width), lambda i: (i, 0))],
        out_specs=pl.BlockSpec(memory_space=pl.ANY),
        out_shape=jax.ShapeDtypeStruct((t, width), F32),
        scratch_shapes=[pltpu.SemaphoreType.DMA(())],
        compiler_params=_compiler_params(1),
        name="moe_dispatch",
    )(dest, routed)


def _expert_kernel(tile_ref, ea_ref, eb_ref, lo_ref, hi_ref, first_ref, used_ref,
                   xs_ref, gain_ref, wga_ref, wua_ref, wda_ref, wgb_ref, wub_ref, wdb_ref, o_ref):
    i = pl.program_id(0)

    @pl.when(i < used_ref[0])
    def _():
        x1 = xs_ref[:, :D_MODEL]
        gates = xs_ref[:, D_MODEL:]
        h = _rms_norm(x1, gain_ref[...]).astype(BF16)
        lane = lax.broadcasted_iota(jnp.int32, gates.shape, 1)
        row = lax.broadcasted_iota(jnp.int32, gates.shape, 0)
        mine = (row >= lo_ref[i]) & (row < hi_ref[i])
        moe = None
        for e_ref, wg_ref, wu_ref, wd_ref in ((ea_ref, wga_ref, wua_ref, wda_ref),
                                              (eb_ref, wgb_ref, wub_ref, wdb_ref)):
            gate = jnp.sum(jnp.where(mine & (lane == e_ref[i]), gates, 0.0), axis=-1, keepdims=True)
            a = jnp.dot(h, wg_ref[...], preferred_element_type=F32)
            u = jnp.dot(h, wu_ref[...], preferred_element_type=F32)
            hid = (jax.nn.silu(a) * u * gate).astype(BF16)
            term = jnp.dot(hid, wd_ref[...], preferred_element_type=F32)
            moe = term if moe is None else moe + term

        @pl.when(first_ref[i] == 1)
        def _():
            o_ref[...] = x1 + moe

        @pl.when(first_ref[i] == 0)
        def _():
            o_ref[...] += moe


def _experts(xs, plan, gain, w_e_gate, w_e_up, w_e_down):
    n_items = plan[0].shape[0]
    by = lambda which, r, c: pl.BlockSpec((None, r, c), lambda i, *p: (p[which][i], 0, 0))
    tile_rows = lambda width: pl.BlockSpec((MOE_TILE, width), lambda i, *p: (p[0][i], 0))
    wg, wu, wd = (w.astype(BF16) for w in (w_e_gate, w_e_up, w_e_down))
    return pl.pallas_call(
        _expert_kernel,
        grid_spec=pltpu.PrefetchScalarGridSpec(
            num_scalar_prefetch=len(plan),
            grid=(n_items,),
            in_specs=[tile_rows(D_ROUTED),
                      pl.BlockSpec((1, D_MODEL), lambda i, *p: (0, 0)),
                      by(1, D_MODEL, D_EXPERT), by(1, D_MODEL, D_EXPERT), by(1, D_EXPERT, D_MODEL),
                      by(2, D_MODEL, D_EXPERT), by(2, D_MODEL, D_EXPERT), by(2, D_EXPERT, D_MODEL)],
            out_specs=tile_rows(D_MODEL)),
        out_shape=jax.ShapeDtypeStruct((xs.shape[0], D_MODEL), F32),
        compiler_params=_compiler_params(1),
        name="moe_experts",
    )(*plan, xs, gain[None, :], wg, wu, wd, wg, wu, wd)


def _unsort_kernel(dest_ref, src_hbm, dst_ref, sem, *, rows):
    for r in range(rows):
        _row_copy(src_hbm, dest_ref[r], dst_ref, r, sem).start(priority=r % DMA_THREADS)
    for r in range(rows):
        _row_copy(src_hbm, 0, dst_ref, r, sem).wait()


def _unsort(xo, dest, rows):
    t = dest.shape[0]
    return pl.pallas_call(
        functools.partial(_unsort_kernel, rows=rows),
        grid=(t // rows,),
        in_specs=[pl.BlockSpec((rows,), lambda i: (i,), memory_space=pltpu.SMEM),
                  pl.BlockSpec(memory_space=pl.ANY)],
        out_specs=pl.BlockSpec((rows, xo.shape[1]), lambda i: (i, 0)),
        out_shape=jax.ShapeDtypeStruct((t, xo.shape[1]), F32),
        scratch_shapes=[pltpu.SemaphoreType.DMA(())],
        compiler_params=_compiler_params(1),
        name="moe_unsort",
    )(dest, xo)


def _moe(routed, counts, ffn_gain, w_e_gate, w_e_up, w_e_down, rows):
    t = routed.shape[0]
    assert t % MOE_TILE == 0 and t % rows == 0
    n_items = t // MOE_TILE + N_CLASSES
    dest, plan = _moe_plan(routed, counts, n_items)
    xs = _dispatch(routed, dest, rows)
    xo = _experts(xs, plan, ffn_gain, w_e_gate, w_e_up, w_e_down)
    return _unsort(xo, dest, rows)


def kernel(x, mix_norm_gain, w_in, b_gate, q_gain, k_gain, rel_bias, ssm_lambda_re, ssm_lambda_im, ssm_log_step, ssm_b_re, ssm_b_im, ssm_c_re, ssm_c_im, ssm_d, w_glu, b_glu, w_branch, w_out, ffn_norm_gain, w_group_router, group_bias, w_expert_router, expert_bias, w_e_gate, w_e_up, w_e_down):
    bsz, seq, _ = x.shape
    assert seq % (SSM_KBLOCK * SSM_CHUNK) == 0 and seq % (4 * CHUNK) == 0
    depth = w_in.shape[0]
    tiles = _tiles(bsz * seq)
    x2 = x.reshape(bsz * seq, D_MODEL)
    for i in range(depth):
        w_qkv = w_in[i][:, :D_QKV]
        w_u = w_in[i][:, D_QKV:D_QKV + D_SSM]
        w_gate = w_in[i][:, D_QKV + D_SSM:]
        y_attn = _attention(x2, mix_norm_gain[i], w_qkv, q_gain[i], k_gain[i],
                            _rel_bias_table(rel_bias[i]), bsz, seq)
        u_t = _ssm_inproj(x2, mix_norm_gain[i], w_u, bsz, seq)
        tables = _ssm_tables(ssm_lambda_re[i], ssm_lambda_im[i], ssm_log_step[i],
                             ssm_b_re[i], ssm_b_im[i], ssm_c_re[i], ssm_c_im[i], ssm_d[i])
        y_pre = _ssm_out(_ssm(u_t, tables, bsz), bsz, seq)
        routed, counts = _merge(x2, y_attn, y_pre, mix_norm_gain[i], w_gate, b_gate[i], w_glu[i],
                                b_glu[i], w_branch[i], w_out[i], ffn_norm_gain[i],
                                w_group_router[i], group_bias[i], w_expert_router[i],
                                expert_bias[i], tiles["merge"])
        x2 = _moe(routed, counts, ffn_norm_gain[i], w_e_gate[i], w_e_up[i], w_e_down[i],
                  tiles["moe_rows"])
    return x2.reshape(bsz, seq, D_MODEL)
```
